```python
import jax
import jax.numpy as jnp
from jax import lax
import numpy as np

D_MODEL = 1024
BATCH = 32
SEQ = 256
DEPTH = 2
DEC_BATCH = 2
DEC_SEQ = 4096
PAST_LEN = 512

GRID_W = 64
HEAD_DIM = 64
MLA_HEADS = 8
MLA_Q_LORA = 256
MLA_KV_LORA = 128
MLA_NOPE = 64
MLA_ROPE = 32
MLA_V = 64
SWA_HEADS = 4
SWA_KV_HEADS = 2
SWA_GROUP = SWA_HEADS // SWA_KV_HEADS
SWA_WINDOW = 128
SWA_BLOCK = 128
NA_HEADS = 4
NA_KH_MAX = 8
NA_KW = 16
N_GROUPS = 4
EXPERTS_PER_GROUP = 4
N_EXPERTS = N_GROUPS * EXPERTS_PER_GROUP
EXPERT_FF = 256
TOP_K = 2
N_MOD = 6
Q_BLOCK = 128
ROPE_THETA = 10000.0
EPS = 1e-6
MASK_VALUE = -1e30
MLA_SCALE = (MLA_NOPE + MLA_ROPE) ** -0.5
HEAD_SCALE = HEAD_DIM ** -0.5
MIX_WIDTH = MLA_HEADS * MLA_V + SWA_HEADS * HEAD_DIM + NA_HEADS * HEAD_DIM
IN_SWA = (SWA_HEADS + 2 * SWA_KV_HEADS) * HEAD_DIM
IN_NA = 3 * NA_HEADS * HEAD_DIM
D_IN = MLA_Q_LORA + MLA_KV_LORA + MLA_ROPE + IN_SWA + IN_NA

kernel_name = 'hybrid_dit_mla_swa_na_hmoe_step'


def rmsnorm(x, g):
    xf = x.astype(jnp.float32)
    y = xf * lax.rsqrt(jnp.mean(xf * xf, axis=-1, keepdims=True) + EPS)
    return (y * g.astype(jnp.float32)).astype(x.dtype)


def rope_1d(x, pos):
    half = x.shape[-1] // 2
    inv_freq = ROPE_THETA ** (-jnp.arange(half, dtype=jnp.float32) / half)
    ang = pos[:, None] * inv_freq[None, :]
    shape = (1, pos.shape[0]) + (1,) * (x.ndim - 3) + (half,)
    cos = jnp.cos(ang).reshape(shape).astype(x.dtype)
    sin = jnp.sin(ang).reshape(shape).astype(x.dtype)
    x1, x2 = x[..., :half], x[..., half:]
    return jnp.concatenate([x1 * cos - x2 * sin, x1 * sin + x2 * cos], axis=-1)


def axial_rope(x):
    t = x.shape[1]
    pos = jnp.arange(t)
    row = (pos // GRID_W).astype(jnp.float32)
    col = (pos % GRID_W).astype(jnp.float32)
    d2 = x.shape[-1] // 2
    return jnp.concatenate([rope_1d(x[..., :d2], row), rope_1d(x[..., d2:], col)], axis=-1)


def modulate(cvec, w_mod_l, b_mod_l):
    m = jnp.einsum('bd,de->be', jax.nn.silu(cvec), w_mod_l) + b_mod_l
    return jnp.split(m[:, None, :], N_MOD, axis=-1)


def mixer_projections(h, w_in_l, g_qa_l, w_uq_l, g_kva_l):
    b, t, _ = h.shape
    p = jnp.einsum('btd,de->bte', h, w_in_l)
    s1 = MLA_Q_LORA
    s2 = s1 + MLA_KV_LORA
    s3 = s2 + MLA_ROPE
    s4 = s3 + IN_SWA
    cq, ckv, kpe, swa, na = jnp.split(p, [s1, s2, s3, s4], axis=-1)
    q_mla = jnp.einsum('btr,re->bte', rmsnorm(cq, g_qa_l), w_uq_l).reshape(b, t, MLA_HEADS, MLA_NOPE + MLA_ROPE)
    ckv = rmsnorm(ckv, g_kva_l)
    q_swa, k_swa, v_swa = jnp.split(swa, [SWA_HEADS * HEAD_DIM, (SWA_HEADS + SWA_KV_HEADS) * HEAD_DIM], axis=-1)
    q_swa = q_swa.reshape(b, t, SWA_HEADS, HEAD_DIM)
    k_swa = k_swa.reshape(b, t, SWA_KV_HEADS, HEAD_DIM)
    v_swa = v_swa.reshape(b, t, SWA_KV_HEADS, HEAD_DIM)
    q_na, k_na, v_na = jnp.split(na, 3, axis=-1)
    q_na = q_na.reshape(b, t, NA_HEADS, HEAD_DIM)
    k_na = k_na.reshape(b, t, NA_HEADS, HEAD_DIM)
    v_na = v_na.reshape(b, t, NA_HEADS, HEAD_DIM)
    return q_mla, ckv, kpe, q_swa, k_swa, v_swa, q_na, k_na, v_na


def mla_expand(ckv, kpe, w_ukv_l):
    b, t, _ = ckv.shape
    kv = jnp.einsum('btc,ce->bte', ckv, w_ukv_l).reshape(b, t, MLA_HEADS, MLA_NOPE + MLA_V)
    k_nope, v = kv[..., :MLA_NOPE], kv[..., MLA_NOPE:]
    k = jnp.concatenate([k_nope, jnp.broadcast_to(kpe[:, :, None, :], (b, t, MLA_HEADS, MLA_ROPE))], axis=-1)
    return k, v


def dense_attention(q, k, v, scale, sink=None):
    b, tq, kh, g, d = q.shape
    nb = tq // Q_BLOCK
    qb = jnp.moveaxis(q.reshape(b, nb, Q_BLOCK, kh, g, d), 1, 0)

    def one_block(q_blk):
        s = jnp.einsum('bqkgd,bskd->bkgqs', q_blk, k).astype(jnp.float32) * scale
        if sink is not None:
            sk = jnp.broadcast_to(sink.astype(jnp.float32)[None, :, :, None, None], s.shape[:-1] + (1,))
            p = jax.nn.softmax(jnp.concatenate([s, sk], axis=-1), axis=-1)[..., :-1]
        else:
            p = jax.nn.softmax(s, axis=-1)
        return jnp.einsum('bkgqs,bskv->bqkgv', p.astype(v.dtype), v)

    o = lax.map(one_block, qb)
    return jnp.moveaxis(o, 0, 1).reshape(b, tq, kh * g * v.shape[-1])


def window_attention(q, k, v, k_ctx, v_ctx, sink, scale):
    b, t, kh, g, d = q.shape
    nb = t // SWA_BLOCK
    qb = q.reshape(b, nb, SWA_BLOCK, kh, g, d)

    def neighbours(a):
        pad = jnp.pad(a, ((0, 0), (SWA_BLOCK, SWA_BLOCK), (0, 0), (0, 0)))
        blk = pad.reshape(b, nb + 2, SWA_BLOCK, kh, a.shape[-1])
        return jnp.concatenate([blk[:, :-2], blk[:, 1:-1], blk[:, 2:]], axis=2)

    kb, vb = neighbours(k), neighbours(v)
    qi = jnp.arange(SWA_BLOCK)
    kj = jnp.arange(3 * SWA_BLOCK)
    rel = kj[None, :] - SWA_BLOCK - qi[:, None]
    kpos = jnp.arange(nb)[:, None, None] * SWA_BLOCK - SWA_BLOCK + kj[None, None, :]
    mask = (jnp.abs(rel)[None] <= SWA_WINDOW) & (kpos >= 0) & (kpos < t)
    s_loc = jnp.einsum('bnqkgd,bnskd->bnkgqs', qb, kb).astype(jnp.float32) * scale
    s_loc = jnp.where(mask[None, :, None, None], s_loc, MASK_VALUE)
    s_ctx = jnp.einsum('bnqkgd,bpkd->bnkgqp', qb, k_ctx).astype(jnp.float32) * scale
    sk = jnp.broadcast_to(sink.astype(jnp.float32)[None, None, :, :, None, None], s_loc.shape[:-1] + (1,))
    p = jax.nn.softmax(jnp.concatenate([s_loc, s_ctx, sk], axis=-1), axis=-1)
    n_loc = 3 * SWA_BLOCK
    n_ctx = k_ctx.shape[1]
    p_loc = p[..., :n_loc].astype(v.dtype)
    p_ctx = p[..., n_loc:n_loc + n_ctx].astype(v.dtype)
    o = jnp.einsum('bnkgqs,bnskv->bnqkgv', p_loc, vb) + jnp.einsum('bnkgqp,bpkv->bnqkgv', p_ctx, v_ctx)
    return o.reshape(b, t, kh * g * v.shape[-1])


def neighbourhood_attention(q, k, v, k_ctx, v_ctx, rpb, scale):
    b, t, h, d = q.shape
    rows = t // GRID_W
    kh = min(NA_KH_MAX, rows)
    r = jnp.arange(rows)
    cidx = jnp.arange(GRID_W)
    row_start = jnp.clip(r - kh // 2, 0, rows - kh)
    key_rows = row_start[:, None] + jnp.arange(kh)[None, :]
    col_start = jnp.clip(cidx - NA_KW // 2, 0, GRID_W - NA_KW)
    col_ok = (cidx[None, :] >= col_start[:, None]) & (cidx[None, :] < col_start[:, None] + NA_KW)
    off_r = key_rows - r[:, None] + NA_KH_MAX - 1
    off_c = jnp.clip(cidx[None, :] - cidx[:, None], -(NA_KW - 1), NA_KW - 1) + NA_KW - 1
    bias = rpb[:, off_r[:, None, :, None], off_c[None, :, None, :]].astype(jnp.float32)
    bias = jnp.where(col_ok[None, None, :, None, :], bias, MASK_VALUE)
    bias = bias.transpose(1, 0, 2, 3, 4).reshape(rows, h, GRID_W, kh * GRID_W)
    qg = q.reshape(b, rows, GRID_W, h, d)
    kg = k.reshape(b, rows, GRID_W, h, d)[:, key_rows].reshape(b, rows, kh * GRID_W, h, d)
    vg = v.reshape(b, rows, GRID_W, h, v.shape[-1])[:, key_rows].reshape(b, rows, kh * GRID_W, h, v.shape[-1])
    s_loc = jnp.einsum('brqhd,brshd->brhqs', qg, kg).astype(jnp.float32) * scale + bias[None]
    s_ctx = jnp.einsum('brqhd,bphd->brhqp', qg, k_ctx).astype(jnp.float32) * scale
    p = jax.nn.softmax(jnp.concatenate([s_loc, s_ctx], axis=-1), axis=-1)
    n_loc = kh * GRID_W
    o = (jnp.einsum('brhqs,brshv->brqhv', p[..., :n_loc].astype(v.dtype), vg)
         + jnp.einsum('brhqp,bphv->brqhv', p[..., n_loc:].astype(v.dtype), v_ctx))
    return o.reshape(b, t, h * v.shape[-1])


def hier_moe(h, w_rg_l, w_re_l, w_gate_l, w_up_l, w_down_l):
    b, t, _ = h.shape
    grp_logits = jnp.einsum('btd,dg->btg', h, w_rg_l).astype(jnp.float32)
    grp_prob = jax.nn.softmax(grp_logits, axis=-1)
    grp_idx = jnp.argmax(grp_logits, axis=-1)
    grp_gate = jnp.max(grp_prob, axis=-1, keepdims=True)
    exp_logits = jnp.einsum('btd,de->bte', h, w_re_l).astype(jnp.float32).reshape(b, t, N_GROUPS, EXPERTS_PER_GROUP)
    grp_onehot = jax.nn.one_hot(grp_idx, N_GROUPS, dtype=jnp.float32)
    exp_logits = jnp.sum(exp_logits * grp_onehot[..., None], axis=2)
    top_val, top_idx = lax.top_k(exp_logits, TOP_K)
    top_w = jax.nn.softmax(top_val, axis=-1) * grp_gate
    expert_id = grp_idx[..., None] * EXPERTS_PER_GROUP + top_idx
    gates = jnp.sum(jax.nn.one_hot(expert_id, N_EXPERTS, dtype=jnp.float32) * top_w[..., None], axis=-2)
    hidden = jax.nn.silu(jnp.einsum('btd,edf->btef', h, w_gate_l)) * jnp.einsum('btd,edf->btef', h, w_up_l)
    hidden = hidden * gates[..., None].astype(hidden.dtype)
    return jnp.einsum('btef,efd->btd', hidden, w_down_l)


def residual_tail(x, heads_out, g1, sh2, sc2, g2, norm2_l, w_out_l, w_rg_l, w_re_l, w_gate_l, w_up_l, w_down_l):
    x = x + g1 * jnp.einsum('bte,ed->btd', heads_out, w_out_l)
    h2 = rmsnorm(x, norm2_l) * (1 + sc2) + sh2
    return x + g2 * hier_moe(h2, w_rg_l, w_re_l, w_gate_l, w_up_l, w_down_l)


def context_layer(x, c_ctx, w_mod_l, b_mod_l, norm1_l, norm2_l, w_in_l, g_qa_l, w_uq_l, g_kva_l, w_ukv_l,
                  sink_l, w_out_l, w_rg_l, w_re_l, w_gate_l, w_up_l, w_down_l):
    b, t, _ = x.shape
    sh1, sc1, g1, sh2, sc2, g2 = modulate(c_ctx[None, :], w_mod_l, b_mod_l)
    h = rmsnorm(x, norm1_l) * (1 + sc1) + sh1
    q_mla, ckv, kpe, q_swa, k_swa, v_swa, q_na, k_na, v_na = mixer_projections(h, w_in_l, g_qa_l, w_uq_l, g_kva_l)
    k_mla, v_mla = mla_expand(ckv, kpe, w_ukv_l)
    o_mla = dense_attention(q_mla[:, :, :, None, :], k_mla, v_mla, MLA_SCALE)
    o_swa = dense_attention(q_swa.reshape(b, t, SWA_KV_HEADS, SWA_GROUP, HEAD_DIM), k_swa, v_swa, HEAD_SCALE,
                            sink_l.reshape(SWA_KV_HEADS, SWA_GROUP))
    o_na = dense_attention(q_na[:, :, :, None, :], k_na, v_na, HEAD_SCALE)
    heads_out = jnp.concatenate([o_mla, o_swa, o_na], axis=-1)
    x = residual_tail(x, heads_out, g1, sh2, sc2, g2, norm2_l, w_out_l, w_rg_l, w_re_l, w_gate_l, w_up_l, w_down_l)
    return x, ckv, kpe, k_swa, v_swa, k_na, v_na


def latent_layer(x, c, ckv_c, kpe_c, kswa_c, vswa_c, kna_c, vna_c, rpb_l,
                 w_mod_l, b_mod_l, norm1_l, norm2_l, w_in_l, g_qa_l, w_uq_l, g_kva_l, w_ukv_l,
                 sink_l, w_out_l, w_rg_l, w_re_l, w_gate_l, w_up_l, w_down_l):
    b, t, _ = x.shape
    sh1, sc1, g1, sh2, sc2, g2 = modulate(c, w_mod_l, b_mod_l)
    h = rmsnorm(x, norm1_l) * (1 + sc1) + sh1
    q_mla, ckv, kpe, q_swa, k_swa, v_swa, q_na, k_na, v_na = mixer_projections(h, w_in_l, g_qa_l, w_uq_l, g_kva_l)
    q_mla = jnp.concatenate([q_mla[..., :MLA_NOPE], axial_rope(q_mla[..., MLA_NOPE:])], axis=-1)
    kpe = axial_rope(kpe[:, :, None, :])[:, :, 0, :]
    k_lat, v_lat = mla_expand(ckv, kpe, w_ukv_l)
    k_ctx, v_ctx = mla_expand(ckv_c, kpe_c, w_ukv_l)
    o_mla = dense_attention(q_mla[:, :, :, None, :], jnp.concatenate([k_lat, k_ctx], axis=1),
                            jnp.concatenate([v_lat, v_ctx], axis=1), MLA_SCALE)
    q_swa = axial_rope(q_swa).reshape(b, t, SWA_KV_HEADS, SWA_GROUP, HEAD_DIM)
    k_swa = axial_rope(k_swa)
    o_swa = window_attention(q_swa, k_swa, v_swa, kswa_c, vswa_c, sink_l.reshape(SWA_KV_HEADS, SWA_GROUP), HEAD_SCALE)
    o_na = neighbourhood_attention(q_na, k_na, v_na, kna_c, vna_c, rpb_l, HEAD_SCALE)
    heads_out = jnp.concatenate([o_mla, o_swa, o_na], axis=-1)
    return residual_tail(x, heads_out, g1, sh2, sc2, g2, norm2_l, w_out_l, w_rg_l, w_re_l, w_gate_l, w_up_l, w_down_l)


def setup_inputs(seed: int = 0) -> dict:
    key = jax.random.key(seed)
    ks = jax.random.split(key, 28)

    def nrm(k, shape, scale=1.0):
        return jax.random.normal(k, shape, jnp.float32) * scale

    return {
        'x_prompt': nrm(ks[0], (BATCH, SEQ, D_MODEL)),
        'x_sample': nrm(ks[1], (DEC_BATCH, DEC_SEQ, D_MODEL)),
        'cache_mla_ckv': nrm(ks[2], (DEC_BATCH, DEPTH, PAST_LEN, MLA_KV_LORA)),
        'cache_mla_kpe': nrm(ks[3], (DEC_BATCH, DEPTH, PAST_LEN, MLA_ROPE)),
        'cache_swa_k': nrm(ks[4], (DEC_BATCH, DEPTH, PAST_LEN, SWA_KV_HEADS, HEAD_DIM)),
        'cache_swa_v': nrm(ks[5], (DEC_BATCH, DEPTH, PAST_LEN, SWA_KV_HEADS, HEAD_DIM)),
        'cache_na_k': nrm(ks[6], (DEC_BATCH, DEPTH, PAST_LEN, NA_HEADS, HEAD_DIM)),
        'cache_na_v': nrm(ks[7], (DEC_BATCH, DEPTH, PAST_LEN, NA_HEADS, HEAD_DIM)),
        'c': nrm(ks[8], (DEC_BATCH, D_MODEL)),
        'c_ctx': nrm(ks[9], (D_MODEL,)),
        'w_mod': nrm(ks[10], (DEPTH, D_MODEL, N_MOD * D_MODEL), 0.5 * D_MODEL ** -0.5),
        'b_mod': nrm(ks[11], (DEPTH, N_MOD * D_MODEL), 0.02),
        'norm1': 1.0 + nrm(ks[12], (DEPTH, D_MODEL), 0.02),
        'norm2': 1.0 + nrm(ks[13], (DEPTH, D_MODEL), 0.02),
        'w_in': nrm(ks[14], (DEPTH, D_MODEL, D_IN), D_MODEL ** -0.5),
        'g_qa': 1.0 + nrm(ks[15], (DEPTH, MLA_Q_LORA), 0.02),
        'w_uq': nrm(ks[16], (DEPTH, MLA_Q_LORA, MLA_HEADS * (MLA_NOPE + MLA_ROPE)), MLA_Q_LORA ** -0.5),
        'g_kva': 1.0 + nrm(ks[17], (DEPTH, MLA_KV_LORA), 0.02),
        'w_ukv': nrm(ks[18], (DEPTH, MLA_KV_LORA, MLA_HEADS * (MLA_NOPE + MLA_V)), MLA_KV_LORA ** -0.5),
        'swa_sink': nrm(ks[19], (DEPTH, SWA_HEADS)),
        'na_rpb': nrm(ks[20], (DEPTH, NA_HEADS, 2 * NA_KH_MAX - 1, 2 * NA_KW - 1), 0.1),
        'w_out': nrm(ks[21], (DEPTH, MIX_WIDTH, D_MODEL), MIX_WIDTH ** -0.5),
        'w_router_grp': nrm(ks[22], (DEPTH, D_MODEL, N_GROUPS), D_MODEL ** -0.5),
        'w_router_exp': nrm(ks[23], (DEPTH, D_MODEL, N_EXPERTS), D_MODEL ** -0.5),
        'w_gate': nrm(ks[24], (DEPTH, N_EXPERTS, D_MODEL, EXPERT_FF), D_MODEL ** -0.5),
        'w_up': nrm(ks[25], (DEPTH, N_EXPERTS, D_MODEL, EXPERT_FF), D_MODEL ** -0.5),
        'w_down': nrm(ks[26], (DEPTH, N_EXPERTS, EXPERT_FF, D_MODEL), EXPERT_FF ** -0.5),
        'norm_final': 1.0 + nrm(ks[27], (D_MODEL,), 0.02),
    }


def reference(x_prompt, x_sample, cache_mla_ckv, cache_mla_kpe, cache_swa_k, cache_swa_v, cache_na_k, cache_na_v,
              c, c_ctx, w_mod, b_mod, norm1, norm2, w_in, g_qa, w_uq, g_kva, w_ukv, swa_sink, na_rpb, w_out,
              w_router_grp, w_router_exp, w_gate, w_up, w_down, norm_final):
    xp, xs = x_prompt, x_sample
    new_ckv, new_kpe, new_ks, new_vs, new_kn, new_vn = [], [], [], [], [], []
    for l in range(DEPTH):
        shared = (w_mod[l], b_mod[l], norm1[l], norm2[l], w_in[l], g_qa[l], w_uq[l], g_kva[l], w_ukv[l],
                  swa_sink[l], w_out[l], w_router_grp[l], w_router_exp[l], w_gate[l], w_up[l], w_down[l])
        xp, ckv, kpe, ks_, vs_, kn_, vn_ = context_layer(xp, c_ctx, *shared)
        new_ckv.append(ckv)
        new_kpe.append(kpe)
        new_ks.append(ks_)
        new_vs.append(vs_)
        new_kn.append(kn_)
        new_vn.append(vn_)
        xs = latent_layer(xs, c, cache_mla_ckv[:, l], cache_mla_kpe[:, l], cache_swa_k[:, l], cache_swa_v[:, l],
                          cache_na_k[:, l], cache_na_v[:, l], na_rpb[l], *shared)
    y_prompt = rmsnorm(xp, norm_final)
    y_sample = rmsnorm(xs, norm_final)
    return (y_prompt, y_sample, jnp.stack(new_ckv, axis=1), jnp.stack(new_kpe, axis=1), jnp.stack(new_ks, axis=1),
            jnp.stack(new_vs, axis=1), jnp.stack(new_kn, axis=1), jnp.stack(new_vn, axis=1))
```

```python
import functools

import jax
import jax.numpy as jnp
import numpy as np
from jax import lax
from jax.experimental import pallas as pl
from jax.experimental.pallas import tpu as pltpu

D_MODEL = 1024
BATCH = 32
SEQ = 256
DEPTH = 2
DEC_BATCH = 2
DEC_SEQ = 4096
PAST_LEN = 512
GRID_W = 64
HEAD_DIM = 64
MLA_HEADS = 8
MLA_Q_LORA = 256
MLA_KV_LORA = 128
MLA_NOPE = 64
MLA_ROPE = 32
MLA_V = 64
SWA_HEADS = 4
SWA_KV_HEADS = 2
SWA_WINDOW = 128
NA_HEADS = 4
NA_KH = 8
NA_KW = 16
N_GROUPS = 4
EXPERTS_PER_GROUP = 4
N_EXPERTS = 16
EXPERT_FF = 256
N_MOD = 6
ROPE_THETA = 10000.0
EPS = 1e-6
MASK_VALUE = -1e30
MLA_SCALE = (MLA_NOPE + MLA_ROPE) ** -0.5
HEAD_SCALE = HEAD_DIM ** -0.5

LANES = 128
N_PAIRS = MLA_HEADS // 2
MLA_QK_BLK = 2 * LANES
N_CTX_TOK = BATCH * SEQ
N_LAT_TOK = DEC_BATCH * DEC_SEQ

_C_CQ, _C_CKV, _C_QS, _C_KS, _C_VS, _C_QN, _C_KN, _C_VN, _C_KPE, _C_QSR, _C_KSR, _C_KPER, _C_END = (
    0, 256, 384, 640, 768, 896, 1152, 1408, 1664, 1792, 2048, 2176, 2304)

TM_PROJ = 512
TM_TAIL = 512
TM_MOE = 512
TQ = 256
SWA_KWIN = TQ + 2 * SWA_WINDOW
NA_ROWS_PER_TILE = TQ // GRID_W
NA_KEY_ROWS = 12
NA_KWIN = NA_KEY_ROWS * GRID_W
VMEM_LIMIT = 56 * 1024 * 1024

F32 = jnp.float32
BF16 = jnp.bfloat16


def _dot(a, b):
    return jnp.dot(a, b, preferred_element_type=F32)


def _dot_nt(a, b):
    return lax.dot_general(a, b, (((1,), (1,)), ((), ())), preferred_element_type=F32)


def _rms(x, g):
    return x * lax.rsqrt(jnp.mean(x * x, axis=-1, keepdims=True) + EPS) * g


def _cparams(sem):
    return pltpu.CompilerParams(dimension_semantics=sem, vmem_limit_bytes=VMEM_LIMIT)


def _mod_kernel(c_ref, w_ref, b_ref, o_ref):
    c = c_ref[...]
    s = c * (1.0 / (1.0 + jnp.exp(-c)))
    o_ref[0] = jnp.dot(s, w_ref[0], preferred_element_type=F32, precision=lax.Precision.HIGHEST) + b_ref[0]


def _modulation(cpad, w_mod, b_mod):
    nt = 1024
    return pl.pallas_call(
        _mod_kernel,
        grid=(DEPTH, N_MOD * D_MODEL // nt),
        in_specs=[
            pl.BlockSpec((8, D_MODEL), lambda l, n: (0, 0)),
            pl.BlockSpec((1, D_MODEL, nt), lambda l, n: (l, 0, n)),
            pl.BlockSpec((1, 1, nt), lambda l, n: (l, 0, n)),
        ],
        out_specs=pl.BlockSpec((1, 8, nt), lambda l, n: (l, 0, n)),
        out_shape=jax.ShapeDtypeStruct((DEPTH, 8, N_MOD * D_MODEL), F32),
        compiler_params=_cparams(("arbitrary", "arbitrary")),
        name="modulation",
    )(cpad, w_mod, b_mod.reshape(DEPTH, 1, N_MOD * D_MODEL))


def _proj_kernel(rope, *refs):
    if rope:
        (x_ref, mod_ref, n1_ref, win_ref, gqa_ref, wuq_ref, gkva_ref, wukv_ref, c64_ref, s64_ref, c32_ref, s32_ref,
         qm_ref, km_ref, vm_ref, qs_ref, ks_ref, vs_ref, qn_ref, kn_ref, vn_ref) = refs
    else:
        (x_ref, mod_ref, n1_ref, win_ref, gqa_ref, wuq_ref, gkva_ref, wukv_ref,
         qm_ref, km_ref, vm_ref, qs_ref, ks_ref, vs_ref, qn_ref, kn_ref, vn_ref,
         ckv_o, kpe_o, ks_o, vs_o, kn_o, vn_o) = refs
    m = mod_ref[0]
    h = _rms(x_ref[...], n1_ref[...]) * (1.0 + m[1:2]) + m[0:1]
    p = _dot(h.astype(BF16), win_ref[...])
    qm = _dot(_rms(p[:, _C_CQ:_C_CKV], gqa_ref[...]).astype(BF16), wuq_ref[...])
    ckv = _rms(p[:, _C_CKV:_C_QS], gkva_ref[...])
    kv = _dot(ckv.astype(BF16), wukv_ref[...])
    qs = p[:, _C_QS:_C_KS]
    ks = p[:, _C_KS:_C_VS]
    kpe = p[:, _C_KPE:_C_QSR]
    if rope:
        c64, s64, c32, s32 = c64_ref[...], s64_ref[...], c32_ref[...], s32_ref[...]
        qsr = p[:, _C_QSR:_C_KSR]
        qs = jnp.concatenate(
            [qs[:, j * LANES:(j + 1) * LANES] * c64 + qsr[:, j * LANES:(j + 1) * LANES] * s64 for j in range(2)], axis=1)
        ks = ks * c64 + p[:, _C_KSR:_C_KPER] * s64
        kpe = kpe * c32 + p[:, _C_KPER:_C_END] * s32
    for i in range(N_PAIRS):
        lo = i * MLA_QK_BLK
        qrope = qm[:, lo + LANES:lo + MLA_QK_BLK]
        if rope:
            r0 = N_PAIRS * MLA_QK_BLK + i * LANES
            qrope = qrope * c32 + qm[:, r0:r0 + LANES] * s32
        qm_ref[:, lo:lo + LANES] = qm[:, lo:lo + LANES].astype(BF16)
        qm_ref[:, lo + LANES:lo + MLA_QK_BLK] = qrope.astype(BF16)
        km_ref[:, lo:lo + LANES] = kv[:, i * LANES:(i + 1) * LANES].astype(BF16)
        km_ref[:, lo + LANES:lo + MLA_QK_BLK] = kpe.astype(BF16)
    vm_ref[...] = kv[:, N_PAIRS * LANES:].astype(BF16)
    qs_ref[...] = qs.astype(BF16)
    ks_ref[...] = ks.astype(BF16)
    vs_ref[...] = p[:, _C_VS:_C_QN].astype(BF16)
    qn_ref[...] = p[:, _C_QN:_C_KN].astype(BF16)
    kn_ref[...] = p[:, _C_KN:_C_VN].astype(BF16)
    vn_ref[...] = p[:, _C_VN:_C_KPE].astype(BF16)
    if not rope:
        ckv_o[...] = ckv
        kpe_o[...] = kpe[:, :MLA_ROPE]
        ks_o[...] = ks
        vs_o[...] = p[:, _C_VS:_C_QN]
        kn_o[...] = p[:, _C_KN:_C_VN]
        vn_o[...] = p[:, _C_VN:_C_KPE]


def _projections(x, mod_l, lw, tabs, rope):
    n_tok = x.shape[0]
    tm = TM_PROJ
    tiles_per_seq = DEC_SEQ // tm
    if rope:
        mod_idx = lambda i: (1 + i // tiles_per_seq, 0, 0)
        win, wuq = lw["win_lat"], lw["wuq_lat"]
    else:
        mod_idx = lambda i: (0, 0, 0)
        win, wuq = lw["win_ctx"], lw["wuq_ctx"]
    row = lambda w: pl.BlockSpec((tm, w), lambda i: (i, 0))
    whole = lambda a: pl.BlockSpec(a.shape, lambda i: (0,) * a.ndim)
    in_specs = [row(D_MODEL), pl.BlockSpec((1, N_MOD, D_MODEL), mod_idx), whole(lw["norm1"]), whole(win),
                whole(lw["g_qa"]), whole(wuq), whole(lw["g_kva"]), whole(lw["wukv"])]
    args = [x, mod_l, lw["norm1"], win, lw["g_qa"], wuq, lw["g_kva"], lw["wukv"]]
    widths = [N_PAIRS * MLA_QK_BLK, N_PAIRS * MLA_QK_BLK, N_PAIRS * LANES, 256, 128, 128, 256, 256, 256]
    out_specs = [row(w) for w in widths]
    out_shape = [jax.ShapeDtypeStruct((n_tok, w), BF16) for w in widths]
    if rope:
        tab_spec = pl.BlockSpec((tm, LANES), lambda i: (i % tiles_per_seq, 0))
        in_specs += [tab_spec] * 4
        args += list(tabs)
    else:
        cache_w = [MLA_KV_LORA, MLA_ROPE, 128, 128, 256, 256]
        out_specs += [row(w) for w in cache_w]
        out_shape += [jax.ShapeDtypeStruct((n_tok, w), F32) for w in cache_w]
    return pl.pallas_call(
        functools.partial(_proj_kernel, rope),
        grid=(n_tok // tm,),
        in_specs=in_specs,
        out_specs=out_specs,
        out_shape=out_shape,
        compiler_params=_cparams(("arbitrary",)),
        name="proj_lat" if rope else "proj_ctx",
    )(*args)


def _ctxkv_kernel(ckv_ref, kpe_ref, wukv_ref, km_ref, vm_ref):
    kv = _dot(ckv_ref[...].astype(BF16), wukv_ref[...])
    kpe = kpe_ref[...].astype(BF16)
    for i in range(N_PAIRS):
        lo = i * MLA_QK_BLK
        km_ref[:, lo:lo + LANES] = kv[:, i * LANES:(i + 1) * LANES].astype(BF16)
        km_ref[:, lo + LANES:lo + MLA_QK_BLK] = kpe
    vm_ref[...] = kv[:, N_PAIRS * LANES:].astype(BF16)


def _expand_cached_mla(ckv_c, kpe_dup, wukv):
    n = ckv_c.shape[0]
    whole = lambda a: pl.BlockSpec(a.shape, lambda i: (0,) * a.ndim)
    return pl.pallas_call(
        _ctxkv_kernel,
        grid=(1,),
        in_specs=[whole(ckv_c), whole(kpe_dup), whole(wukv)],
        out_specs=[pl.BlockSpec((n, N_PAIRS * MLA_QK_BLK), lambda i: (0, 0)),
                   pl.BlockSpec((n, N_PAIRS * LANES), lambda i: (0, 0))],
        out_shape=[jax.ShapeDtypeStruct((n, N_PAIRS * MLA_QK_BLK), BF16),
                   jax.ShapeDtypeStruct((n, N_PAIRS * LANES), BF16)],
        compiler_params=_cparams(("arbitrary",)),
        name="expand_cached_mla",
    )(ckv_c, kpe_dup, wukv)


def _softmax_pv(s_list, v_list, sink=None):
    m = jnp.max(s_list[0], axis=-1, keepdims=True)
    for s in s_list[1:]:
        m = jnp.maximum(m, jnp.max(s, axis=-1, keepdims=True))
    if sink is not None:
        m = jnp.maximum(m, sink)
    l = jnp.exp(sink - m) if sink is not None else 0.0
    o = None
    for s, v in zip(s_list, v_list):
        p = jnp.exp(s - m)
        l = l + jnp.sum(p, axis=-1, keepdims=True)
        pv = _dot(p.astype(BF16), v)
        o = pv if o is None else o + pv
    return o / l


def _pair_masks(width):
    lane = lax.broadcasted_iota(jnp.int32, (1, width), 1)
    if width == LANES:
        return [lane < HEAD_DIM, lane >= HEAD_DIM]
    m0 = (lane < MLA_NOPE) | ((lane >= LANES) & (lane < LANES + MLA_ROPE))
    m1 = ((lane >= MLA_NOPE) & (lane < LANES)) | ((lane >= LANES + MLA_ROPE) & (lane < LANES + 2 * MLA_ROPE))
    return [m0, m1]


def _pair_attention(q, ks, vs, sinks=(None, None), score_fn=None):
    masks = _pair_masks(q.shape[1])
    outs = []
    for a in range(2):
        qa = jnp.where(masks[a], q, jnp.zeros_like(q))
        s_list = [_dot_nt(qa, k) for k in ks]
        if score_fn is not None:
            s_list = score_fn(a, s_list)
        outs.append(_softmax_pv(s_list, vs, sinks[a]))
    lane = lax.broadcasted_iota(jnp.int32, outs[0].shape, 1)
    return jnp.where(lane < HEAD_DIM, outs[0], outs[1])


def _ctx_attn_kernel(sink_ref, qm_ref, km_ref, vm_ref, qs_ref, ks_ref, vs_ref, qn_ref, kn_ref, vn_ref,
                     om_ref, os_ref, on_ref):
    for i in range(N_PAIRS):
        q = qm_ref[:, i * MLA_QK_BLK:(i + 1) * MLA_QK_BLK]
        k = km_ref[:, i * MLA_QK_BLK:(i + 1) * MLA_QK_BLK]
        v = vm_ref[:, i * LANES:(i + 1) * LANES]
        om_ref[:, i * LANES:(i + 1) * LANES] = _pair_attention(q, [k], [v]).astype(BF16)
    ks, vs = ks_ref[...], vs_ref[...]
    for j in range(2):
        q = qs_ref[:, j * LANES:(j + 1) * LANES]
        sinks = (sink_ref[j], sink_ref[j + 2])
        os_ref[:, j * LANES:(j + 1) * LANES] = _pair_attention(q, [ks], [vs], sinks).astype(BF16)
    for j in range(2):
        sl = slice(j * LANES, (j + 1) * LANES)
        on_ref[:, sl] = _pair_attention(qn_ref[:, sl], [kn_ref[:, sl]], [vn_ref[:, sl]]).astype(BF16)


def _context_attention(sink_l, qm, km, vm, qs, ks, vs, qn, kn, vn):
    row = lambda a: pl.BlockSpec((SEQ, a.shape[1]), lambda b: (b, 0))
    ins = [qm, km, vm, qs, ks, vs, qn, kn, vn]
    widths = [N_PAIRS * LANES, 256, 256]
    return pl.pallas_call(
        _ctx_attn_kernel,
        grid=(BATCH,),
        in_specs=[pl.BlockSpec(memory_space=pltpu.SMEM)] + [row(a) for a in ins],
        out_specs=[pl.BlockSpec((SEQ, w), lambda b: (b, 0)) for w in widths],
        out_shape=[jax.ShapeDtypeStruct((N_CTX_TOK, w), BF16) for w in widths],
        compiler_params=_cparams(("arbitrary",)),
        name="attn_ctx",
    )(sink_l, *ins)


def _lat_mla_kernel(q_ref, kl_ref, vl_ref, kc_ref, vc_ref, o_ref):
    o_ref[...] = _pair_attention(q_ref[...], [kl_ref[...], kc_ref[...]], [vl_ref[...], vc_ref[...]]).astype(BF16)


def _latent_mla(qm, km, vm, kmc, vmc):
    nq = DEC_SEQ // TQ
    return pl.pallas_call(
        _lat_mla_kernel,
        grid=(DEC_BATCH, N_PAIRS, nq),
        in_specs=[
            pl.BlockSpec((TQ, MLA_QK_BLK), lambda b, i, t: (b * nq + t, i)),
            pl.BlockSpec((DEC_SEQ, MLA_QK_BLK), lambda b, i, t: (b, i)),
            pl.BlockSpec((DEC_SEQ, LANES), lambda b, i, t: (b, i)),
            pl.BlockSpec((PAST_LEN, MLA_QK_BLK), lambda b, i, t: (b, i)),
            pl.BlockSpec((PAST_LEN, LANES), lambda b, i, t: (b, i)),
        ],
        out_specs=pl.BlockSpec((TQ, LANES), lambda b, i, t: (b * nq + t, i)),
        out_shape=jax.ShapeDtypeStruct((N_LAT_TOK, N_PAIRS * LANES), BF16),
        compiler_params=_cparams(("arbitrary", "arbitrary", "arbitrary")),
        name="attn_lat_mla",
    )(qm, km, vm, kmc, vmc)


def _lat_swa_kernel(sink_ref, q_ref, k_ref, v_ref, kc_ref, vc_ref, o_ref):
    t = pl.program_id(1)
    q0 = t * TQ
    ws = pl.multiple_of(jnp.clip(q0 - SWA_WINDOW, 0, DEC_SEQ - SWA_KWIN), SWA_WINDOW)
    kw = k_ref[pl.ds(ws, SWA_KWIN), :]
    vw = v_ref[pl.ds(ws, SWA_KWIN), :]
    kc = kc_ref[...].astype(BF16)
    vc = vc_ref[...].astype(BF16)
    qpos = q0 + lax.broadcasted_iota(jnp.int32, (TQ, SWA_KWIN), 0)
    kpos = ws + lax.broadcasted_iota(jnp.int32, (TQ, SWA_KWIN), 1)
    in_band = jnp.abs(kpos - qpos) <= SWA_WINDOW

    def score_fn(a, s_list):
        return [jnp.where(in_band, s_list[0], MASK_VALUE), s_list[1]]

    for j in range(2):
        sinks = (sink_ref[j], sink_ref[j + 2])
        o_ref[:, j * LANES:(j + 1) * LANES] = _pair_attention(
            q_ref[:, j * LANES:(j + 1) * LANES], [kw, kc], [vw, vc], sinks, score_fn).astype(BF16)


def _latent_swa(sink_l, qs, ks, vs, ksc, vsc):
    nq = DEC_SEQ // TQ
    return pl.pallas_call(
        _lat_swa_kernel,
        grid=(DEC_BATCH, nq),
        in_specs=[
            pl.BlockSpec(memory_space=pltpu.SMEM),
            pl.BlockSpec((TQ, 256), lambda b, t: (b * nq + t, 0)),
            pl.BlockSpec((DEC_SEQ, LANES), lambda b, t: (b, 0)),
            pl.BlockSpec((DEC_SEQ, LANES), lambda b, t: (b, 0)),
            pl.BlockSpec((PAST_LEN, LANES), lambda b, t: (b, 0)),
            pl.BlockSpec((PAST_LEN, LANES), lambda b, t: (b, 0)),
        ],
        out_specs=pl.BlockSpec((TQ, 256), lambda b, t: (b * nq + t, 0)),
        out_shape=jax.ShapeDtypeStruct((N_LAT_TOK, 256), BF16),
        compiler_params=_cparams(("arbitrary", "arbitrary")),
        name="attn_lat_swa",
    )(sink_l, qs, ks, vs, ksc, vsc)


def _lat_na_kernel(q_ref, k_ref, v_ref, kc_ref, vc_ref, bias_ref, o_ref):
    t = pl.program_id(1)
    r0 = t * NA_ROWS_PER_TILE
    ws = pl.multiple_of(jnp.clip(r0 - NA_KH // 2, 0, DEC_SEQ // GRID_W - NA_KEY_ROWS) * GRID_W, GRID_W)
    for j in range(2):
        sl = slice(j * LANES, (j + 1) * LANES)
        kw = k_ref[pl.ds(ws, NA_KWIN), sl]
        vw = v_ref[pl.ds(ws, NA_KWIN), sl]
        kc = kc_ref[:, sl].astype(BF16)
        vc = vc_ref[:, sl].astype(BF16)

        def score_fn(a, s_list, j=j):
            return [s_list[0] + bias_ref[0, 2 * j + a], s_list[1]]

        o_ref[:, sl] = _pair_attention(q_ref[:, sl], [kw, kc], [vw, vc], score_fn=score_fn).astype(BF16)


def _latent_na(qn, kn, vn, knc, vnc, bias):
    nq = DEC_SEQ // TQ
    kind = lambda t: jnp.where(t == 0, 0, jnp.where(t == nq - 1, 2, 1))
    return pl.pallas_call(
        _lat_na_kernel,
        grid=(DEC_BATCH, nq),
        in_specs=[
            pl.BlockSpec((TQ, 256), lambda b, t: (b * nq + t, 0)),
            pl.BlockSpec((DEC_SEQ, 256), lambda b, t: (b, 0)),
            pl.BlockSpec((DEC_SEQ, 256), lambda b, t: (b, 0)),
            pl.BlockSpec((PAST_LEN, 256), lambda b, t: (b, 0)),
            pl.BlockSpec((PAST_LEN, 256), lambda b, t: (b, 0)),
            pl.BlockSpec((1, NA_HEADS, TQ, NA_KWIN), lambda b, t: (kind(t), 0, 0, 0)),
        ],
        out_specs=pl.BlockSpec((TQ, 256), lambda b, t: (b * nq + t, 0)),
        out_shape=jax.ShapeDtypeStruct((N_LAT_TOK, 256), BF16),
        compiler_params=_cparams(("arbitrary", "arbitrary")),
        name="attn_lat_na",
    )(qn, kn, vn, knc, vnc, bias)


_GRP_LANE0 = 0
_EXP_LANE0 = N_GROUPS


def _lane_first_max(x, valid, lane):
    xm = jnp.where(valid, x, -jnp.inf)
    mx = jnp.max(xm, axis=-1, keepdims=True)
    idx = jnp.min(jnp.where(valid & (xm == mx), lane, LANES), axis=-1, keepdims=True)
    return mx, idx


def _tail_kernel(x_ref, om_ref, os_ref, on_ref, mod_ref, n2_ref, wo_ref, wr_ref, x1_ref, h2_ref, gates_ref):
    m = mod_ref[0]
    wo = wo_ref
    attn = (_dot(om_ref[...], wo[0:512, :]) + _dot(os_ref[...], wo[512:768, :]) + _dot(on_ref[...], wo[768:1024, :]))
    x1 = x_ref[...] + m[2:3] * attn
    x1_ref[...] = x1
    h2 = _rms(x1, n2_ref[...]) * (1.0 + m[4:5]) + m[3:4]
    h2_ref[...] = h2.astype(BF16)
    hi = h2.astype(BF16)
    lo = (h2 - hi.astype(F32)).astype(BF16)
    a = _dot(hi, wr_ref[...])
    logits = a[:, :LANES] + a[:, LANES:] + _dot(lo, wr_ref[:, :LANES])
    lane = lax.broadcasted_iota(jnp.int32, logits.shape, 1)
    is_grp = lane < N_GROUPS
    gmax, gidx = _lane_first_max(logits, is_grp, lane)
    gden = jnp.sum(jnp.where(is_grp, jnp.exp(logits - gmax), 0.0), axis=-1, keepdims=True)
    grp_gate = 1.0 / gden
    in_grp = (lane >= _EXP_LANE0) & (lane < _EXP_LANE0 + N_EXPERTS) & ((lane // EXPERTS_PER_GROUP - 1) == gidx)
    v1, i1 = _lane_first_max(logits, in_grp, lane)
    v2, i2 = _lane_first_max(logits, in_grp & (lane != i1), lane)
    e2 = jnp.exp(v2 - v1)
    w1 = grp_gate / (1.0 + e2)
    w2 = grp_gate * e2 / (1.0 + e2)
    gates_ref[...] = jnp.where(lane == i1, w1, 0.0) + jnp.where(lane == i2, w2, 0.0)


def _tail(x, om, osw, ona, mod_l, lw, latent):
    n_tok = x.shape[0]
    tm = TM_TAIL
    tiles_per_seq = DEC_SEQ // tm
    mod_idx = (lambda i: (1 + i // tiles_per_seq, 0, 0)) if latent else (lambda i: (0, 0, 0))
    row = lambda w: pl.BlockSpec((tm, w), lambda i: (i, 0))
    whole = lambda a: pl.BlockSpec(a.shape, lambda i: (0,) * a.ndim)
    return pl.pallas_call(
        _tail_kernel,
        grid=(n_tok // tm,),
        in_specs=[row(D_MODEL), row(512), row(256), row(256), pl.BlockSpec((1, N_MOD, D_MODEL), mod_idx),
                  whole(lw["norm2"]), whole(lw["wout"]), whole(lw["wr"])],
        out_specs=[row(D_MODEL), row(D_MODEL), row(LANES)],
        out_shape=[jax.ShapeDtypeStruct((n_tok, D_MODEL), F32), jax.ShapeDtypeStruct((n_tok, D_MODEL), BF16),
                   jax.ShapeDtypeStruct((n_tok, LANES), F32)],
        compiler_params=_cparams(("arbitrary",)),
        name="tail_lat" if latent else "tail_ctx",
    )(x, om, osw, ona, mod_l, lw["norm2"], lw["wout"], lw["wr"])


def _moe_kernel(final, h2_ref, gates_ref, x1_ref, mod_ref, nf_ref, wgu_ref, wd_ref, o_ref, acc_ref):
    g = pl.program_id(1)
    gu = _dot(h2_ref[...], wgu_ref[0])
    ff = EXPERTS_PER_GROUP * EXPERT_FF
    hg, hu = gu[:, :ff], gu[:, ff:]
    gates = gates_ref[...]
    lane = lax.broadcasted_iota(jnp.int32, gates.shape, 1)
    cols = []
    for e in range(EXPERTS_PER_GROUP):
        ge = jnp.sum(jnp.where(lane == _EXP_LANE0 + g * EXPERTS_PER_GROUP + e, gates, 0.0), axis=-1, keepdims=True)
        cols.append(jnp.broadcast_to(ge, (gates.shape[0], EXPERT_FF)))
    gate_full = jnp.concatenate(cols, axis=1)
    act = hg * (1.0 / (1.0 + jnp.exp(-hg))) * hu * gate_full
    contrib = _dot(act.astype(BF16), wd_ref[0])

    @pl.when(g == 0)
    def _():
        acc_ref[...] = contrib

    @pl.when(g > 0)
    def _():
        acc_ref[...] += contrib

    @pl.when(g == N_GROUPS - 1)
    def _():
        y = x1_ref[...] + mod_ref[0][5:6] * acc_ref[...]
        if final:
            y = _rms(y, nf_ref[...])
        o_ref[...] = y


def _moe(h2, gates, x1, mod_l, lw, norm_final, latent, final):
    n_tok = h2.shape[0]
    tm = TM_MOE
    tiles_per_seq = DEC_SEQ // tm
    mod_idx = (lambda i, g: (1 + i // tiles_per_seq, 0, 0)) if latent else (lambda i, g: (0, 0, 0))
    ff = EXPERTS_PER_GROUP * EXPERT_FF
    return pl.pallas_call(
        functools.partial(_moe_kernel, final),
        grid=(n_tok // tm, N_GROUPS),
        in_specs=[
            pl.BlockSpec((tm, D_MODEL), lambda i, g: (i, 0)),
            pl.BlockSpec((tm, LANES), lambda i, g: (i, 0)),
            pl.BlockSpec((tm, D_MODEL), lambda i, g: (i, 0)),
            pl.BlockSpec((1, N_MOD, D_MODEL), mod_idx),
            pl.BlockSpec((1, D_MODEL), lambda i, g: (0, 0)),
            pl.BlockSpec((1, D_MODEL, 2 * ff), lambda i, g: (g, 0, 0)),
            pl.BlockSpec((1, ff, D_MODEL), lambda i, g: (g, 0, 0)),
        ],
        out_specs=pl.BlockSpec((tm, D_MODEL), lambda i, g: (i, 0)),
        out_shape=jax.ShapeDtypeStruct((n_tok, D_MODEL), F32),
        scratch_shapes=[pltpu.VMEM((tm, D_MODEL), F32)],
        compiler_params=_cparams(("arbitrary", "arbitrary")),
        name=("moe_lat" if latent else "moe_ctx") + ("_final" if final else ""),
    )(h2, gates, x1, mod_l, norm_final, lw["wgu"], lw["wd"])


def _rot_cols(w, d):
    k, n = w.shape
    hh = d // 4
    w4 = w.reshape(k, n // (2 * hh), 2, hh)
    return jnp.stack([-w4[:, :, 1], w4[:, :, 0]], axis=2).reshape(k, n)


def _heads(w, d, order):
    k = w.shape[0]
    return w.reshape(k, -1, d)[:, list(order)].reshape(k, -1)


def _layer_weights(l, norm1, norm2, w_in, g_qa, w_uq, g_kva, w_ukv, w_out, w_router_grp, w_router_exp, w_gate, w_up,
                   w_down):
    wi = w_in[l]
    z64 = jnp.zeros((D_MODEL, 64), F32)
    cq, ckv, kpe = wi[:, 0:256], wi[:, 256:384], wi[:, 384:416]
    qs = _heads(wi[:, 416:672], HEAD_DIM, (0, 2, 1, 3)) * HEAD_SCALE
    ks, vs = wi[:, 672:800], wi[:, 800:928]
    qn, kn, vn = wi[:, 928:1184] * HEAD_SCALE, wi[:, 1184:1440], wi[:, 1440:1696]
    kped = jnp.concatenate([kpe, kpe, z64], axis=1)
    kper = _rot_cols(kpe, MLA_ROPE)
    ctx_cols = [cq, ckv, qs, ks, vs, qn, kn, vn, kped]
    lat_cols = ctx_cols + [_rot_cols(qs, HEAD_DIM), _rot_cols(ks, HEAD_DIM), jnp.concatenate([kper, kper, z64], axis=1)]
    wq = w_uq[l].reshape(MLA_Q_LORA, MLA_HEADS, MLA_NOPE + MLA_ROPE) * MLA_SCALE
    nope, ropew = wq[:, :, :MLA_NOPE], wq[:, :, MLA_NOPE:]
    z64q = jnp.zeros((MLA_Q_LORA, 64), F32)
    blocks, rots = [], []
    for i in range(N_PAIRS):
        blocks += [nope[:, 2 * i], nope[:, 2 * i + 1], ropew[:, 2 * i], ropew[:, 2 * i + 1], z64q]
        rots += [_rot_cols(ropew[:, 2 * i], MLA_ROPE), _rot_cols(ropew[:, 2 * i + 1], MLA_ROPE), z64q]
    wkv = w_ukv[l].reshape(MLA_KV_LORA, MLA_HEADS, MLA_NOPE + MLA_V)
    wukv = jnp.concatenate([wkv[:, :, :MLA_NOPE].reshape(MLA_KV_LORA, -1), wkv[:, :, MLA_NOPE:].reshape(MLA_KV_LORA, -1)],
                           axis=1)
    wo = w_out[l]
    wout = jnp.concatenate([wo[:512], wo[512:768].reshape(4, 64, D_MODEL)[jnp.array([0, 2, 1, 3])].reshape(256, D_MODEL),
                            wo[768:]], axis=0)
    wr = jnp.concatenate([w_router_grp[l], w_router_exp[l], jnp.zeros((D_MODEL, LANES - N_GROUPS - N_EXPERTS), F32)],
                         axis=1)
    wr_hi = wr.astype(BF16)
    wr_lo = (wr - wr_hi.astype(F32)).astype(BF16)
    ff = EXPERTS_PER_GROUP * EXPERT_FF
    wg = w_gate[l].reshape(N_GROUPS, EXPERTS_PER_GROUP, D_MODEL, EXPERT_FF).transpose(0, 2, 1, 3).reshape(N_GROUPS, D_MODEL, ff)
    wu = w_up[l].reshape(N_GROUPS, EXPERTS_PER_GROUP, D_MODEL, EXPERT_FF).transpose(0, 2, 1, 3).reshape(N_GROUPS, D_MODEL, ff)
    return {
        "norm1": norm1[l][None], "norm2": norm2[l][None], "g_qa": g_qa[l][None], "g_kva": g_kva[l][None],
        "win_ctx": jnp.concatenate(ctx_cols, axis=1).astype(BF16),
        "win_lat": jnp.concatenate(lat_cols, axis=1).astype(BF16),
        "wuq_ctx": jnp.concatenate(blocks, axis=1).astype(BF16),
        "wuq_lat": jnp.concatenate(blocks + rots, axis=1).astype(BF16),
        "wukv": wukv.astype(BF16),
        "wout": wout.astype(BF16),
        "wr": jnp.concatenate([wr_hi, wr_lo], axis=1),
        "wgu": jnp.concatenate([wg, wu], axis=2).astype(BF16),
        "wd": w_down[l].reshape(N_GROUPS, ff, D_MODEL).astype(BF16),
    }


def _rope_tables():
    pos = jnp.arange(DEC_SEQ)
    row = (pos // GRID_W).astype(F32)
    col = (pos % GRID_W).astype(F32)

    def tab(d):
        hh = d // 4
        inv = ROPE_THETA ** (-jnp.arange(hh, dtype=F32) / hh)
        ar, ac = row[:, None] * inv[None, :], col[:, None] * inv[None, :]
        c = jnp.concatenate([jnp.cos(ar), jnp.cos(ar), jnp.cos(ac), jnp.cos(ac)], axis=1)
        s = jnp.concatenate([jnp.sin(ar), jnp.sin(ar), jnp.sin(ac), jnp.sin(ac)], axis=1)
        return c, s

    c64, s64 = tab(HEAD_DIM)
    c32, s32 = tab(MLA_ROPE)
    z = jnp.zeros((DEC_SEQ, 64), F32)
    return (jnp.concatenate([c64, c64], axis=1), jnp.concatenate([s64, s64], axis=1),
            jnp.concatenate([c32, c32, z], axis=1), jnp.concatenate([s32, s32, z], axis=1))


def _na_bias_tables(rpb_l):
    cidx = np.arange(GRID_W)
    col_start = np.clip(cidx - NA_KW // 2, 0, GRID_W - NA_KW)
    col_ok = (cidx[None, :] >= col_start[:, None]) & (cidx[None, :] < col_start[:, None] + NA_KW)
    off_c = np.clip(cidx[None, :] - cidx[:, None], -(NA_KW - 1), NA_KW - 1) + NA_KW - 1
    tt = jnp.where(col_ok[None, None], rpb_l[:, :, off_c], MASK_VALUE)
    tt = jnp.concatenate([tt, jnp.full((NA_HEADS, 1, GRID_W, GRID_W), MASK_VALUE, F32)], axis=1)
    n_rows = DEC_SEQ // GRID_W
    dr = np.full((3, NA_ROWS_PER_TILE, NA_KEY_ROWS), 2 * NA_KH - 1, np.int32)
    for kind, r0 in enumerate((0, NA_ROWS_PER_TILE, n_rows - NA_ROWS_PER_TILE)):
        ws = int(np.clip(r0 - NA_KH // 2, 0, n_rows - NA_KEY_ROWS))
        for a in range(NA_ROWS_PER_TILE):
            r = r0 + a
            rs = int(np.clip(r - NA_KH // 2, 0, n_rows - NA_KH))
            for jj in range(NA_KEY_ROWS):
                kr = ws + jj
                if rs <= kr < rs + NA_KH:
                    dr[kind, a, jj] = kr - r + NA_KH - 1
    b = tt[:, dr]
    return b.transpose(1, 0, 2, 4, 3, 5).reshape(3, NA_HEADS, TQ, NA_KWIN)


def kernel(x_prompt, x_sample, cache_mla_ckv, cache_mla_kpe, cache_swa_k, cache_swa_v, cache_na_k, cache_na_v, c, c_ctx, w_mod, b_mod, norm1, norm2, w_in, g_qa, w_uq, g_kva, w_ukv, swa_sink, na_rpb, w_out, w_router_grp, w_router_exp, w_gate, w_up, w_down, norm_final):
    cpad = jnp.concatenate([c_ctx[None], c, jnp.zeros((8 - 1 - DEC_BATCH, D_MODEL), F32)], axis=0)
    mod = _modulation(cpad, w_mod, b_mod).reshape(DEPTH, 8, N_MOD, D_MODEL)
    tabs = _rope_tables()
    nf = norm_final[None]
    xp = x_prompt.reshape(N_CTX_TOK, D_MODEL)
    xs = x_sample.reshape(N_LAT_TOK, D_MODEL)
    caches = [[] for _ in range(6)]
    for l in range(DEPTH):
        lw = _layer_weights(l, norm1, norm2, w_in, g_qa, w_uq, g_kva, w_ukv, w_out, w_router_grp, w_router_exp,
                            w_gate, w_up, w_down)
        final = l == DEPTH - 1
        outs = _projections(xp, mod[l], lw, None, rope=False)
        om, osw, ona = _context_attention(swa_sink[l], *outs[:9])
        for dst, a in zip(caches, outs[9:]):
            dst.append(a)
        x1, h2, gates = _tail(xp, om, osw, ona, mod[l], lw, latent=False)
        xp = _moe(h2, gates, x1, mod[l], lw, nf, latent=False, final=final)
        qm, km, vm, qs, ks, vs, qn, kn, vn = _projections(xs, mod[l], lw, tabs, rope=True)
        kpe_c = cache_mla_kpe[:, l].reshape(DEC_BATCH * PAST_LEN, MLA_ROPE)
        kpe_dup = jnp.concatenate([kpe_c, kpe_c, jnp.zeros((DEC_BATCH * PAST_LEN, 64), F32)], axis=1)
        kmc, vmc = _expand_cached_mla(cache_mla_ckv[:, l].reshape(DEC_BATCH * PAST_LEN, MLA_KV_LORA), kpe_dup, lw["wukv"])
        om = _latent_mla(qm, km, vm, kmc, vmc)
        flat = lambda a: a[:, l].reshape(DEC_BATCH * PAST_LEN, -1)
        osw = _latent_swa(swa_sink[l], qs, ks, vs, flat(cache_swa_k), flat(cache_swa_v))
        ona = _latent_na(qn, kn, vn, flat(cache_na_k), flat(cache_na_v), _na_bias_tables(na_rpb[l]))
        x1, h2, gates = _tail(xs, om, osw, ona, mod[l], lw, latent=True)
        xs = _moe(h2, gates, x1, mod[l], lw, nf, latent=True, final=final)
    stack = lambda parts, tail: jnp.stack([p.reshape((BATCH, SEQ) + tail) for p in parts], axis=1)
    return (xp.reshape(BATCH, SEQ, D_MODEL), xs.reshape(DEC_BATCH, DEC_SEQ, D_MODEL),
            stack(caches[0], (MLA_KV_LORA,)), stack(caches[1], (MLA_ROPE,)),
            stack(caches[2], (SWA_KV_HEADS, HEAD_DIM)), stack(caches[3], (SWA_KV_HEADS, HEAD_DIM)),
            stack(caches[4], (NA_HEADS, HEAD_DIM)), stack(caches[5], (NA_HEADS, HEAD_DIM)))
```

```python
import functools

import jax
import jax.numpy as jnp
import numpy as np
from jax import lax
from jax.experimental import pallas as pl
from jax.experimental.pallas import tpu as pltpu

D_MODEL = 1024
BATCH = 32
SEQ = 256
DEPTH = 2
DEC_BATCH = 2
DEC_SEQ = 4096
PAST_LEN = 512
GRID_W = 64
HEAD_DIM = 64
MLA_HEADS = 8
MLA_Q_LORA = 256
MLA_KV_LORA = 128
MLA_NOPE = 64
MLA_ROPE = 32
MLA_V = 64
SWA_HEADS = 4
SWA_KV_HEADS = 2
SWA_WINDOW = 128
NA_HEADS = 4
NA_KH = 8
NA_KW = 16
N_GROUPS = 4
EXPERTS_PER_GROUP = 4
N_EXPERTS = 16
EXPERT_FF = 256
N_MOD = 6
ROPE_THETA = 10000.0
EPS = 1e-6
MASK_VALUE = -1e30
MLA_SCALE = (MLA_NOPE + MLA_ROPE) ** -0.5
HEAD_SCALE = HEAD_DIM ** -0.5
LOG2E = 1.4426950408889634

LANES = 128
N_PAIRS = MLA_HEADS // 2
MLA_QK_BLK = 2 * LANES
N_CTX_TOK = BATCH * SEQ
N_LAT_TOK = DEC_BATCH * DEC_SEQ

_C_CQ, _C_CKV, _C_QS, _C_KS, _C_VS, _C_QN, _C_KN, _C_VN, _C_KPE, _C_QSR, _C_KSR, _C_KPER, _C_END = (
    0, 256, 384, 640, 768, 896, 1152, 1408, 1664, 1792, 2048, 2176, 2304)

TM_PROJ = 512
TM_TAIL = 512
TM_MOE = 512
TQ = 256
KV_CHUNK = 2048
ONES_ROWS = 16
SWA_KWIN = TQ + 2 * SWA_WINDOW
NA_ROWS_PER_TILE = TQ // GRID_W
NA_KEY_ROWS = 12
NA_KWIN = NA_KEY_ROWS * GRID_W
VMEM_LIMIT = 56 * 1024 * 1024

F32 = jnp.float32
BF16 = jnp.bfloat16


def _dot(a, b):
    return jnp.dot(a, b, preferred_element_type=F32)


def _dot_nt(a, b):
    return lax.dot_general(a, b, (((1,), (1,)), ((), ())), preferred_element_type=F32)


def _rms(x, g):
    return x * lax.rsqrt(jnp.mean(x * x, axis=-1, keepdims=True) + EPS) * g


def _cparams(sem):
    return pltpu.CompilerParams(dimension_semantics=sem, vmem_limit_bytes=VMEM_LIMIT)


def _mod_kernel(c_ref, w_ref, b_ref, o_ref):
    c = c_ref[...]
    s = c * (1.0 / (1.0 + jnp.exp(-c)))
    o_ref[0] = jnp.dot(s, w_ref[0], preferred_element_type=F32, precision=lax.Precision.HIGHEST) + b_ref[0]


def _modulation(cpad, w_mod, b_mod):
    nt = 1024
    return pl.pallas_call(
        _mod_kernel,
        grid=(DEPTH, N_MOD * D_MODEL // nt),
        in_specs=[
            pl.BlockSpec((8, D_MODEL), lambda l, n: (0, 0)),
            pl.BlockSpec((1, D_MODEL, nt), lambda l, n: (l, 0, n)),
            pl.BlockSpec((1, 1, nt), lambda l, n: (l, 0, n)),
        ],
        out_specs=pl.BlockSpec((1, 8, nt), lambda l, n: (l, 0, n)),
        out_shape=jax.ShapeDtypeStruct((DEPTH, 8, N_MOD * D_MODEL), F32),
        compiler_params=_cparams(("arbitrary", "arbitrary")),
        name="modulation",
    )(cpad, w_mod, b_mod.reshape(DEPTH, 1, N_MOD * D_MODEL))


def _proj_kernel(rope, *refs):
    if rope:
        (x_ref, mod_ref, n1_ref, win_ref, gqa_ref, wuq_ref, gkva_ref, wuk_ref, wuvt_ref,
         c64_ref, s64_ref, c32_ref, s32_ref,
         qm_ref, km_ref, vmt_ref, qs_ref, ks_ref, vs_ref, qn_ref, kn_ref, vn_ref) = refs
    else:
        (x_ref, mod_ref, n1_ref, win_ref, gqa_ref, wuq_ref, gkva_ref, wuk_ref, wuvt_ref,
         qm_ref, km_ref, vmt_ref, qs_ref, ks_ref, vs_ref, qn_ref, kn_ref, vn_ref,
         ckv_o, kpe_o, ks_o, vs_o, kn_o, vn_o) = refs
    m = mod_ref[0]
    h = _rms(x_ref[...], n1_ref[...]) * (1.0 + m[1:2]) + m[0:1]
    p = _dot(h.astype(BF16), win_ref[...])
    qm = _dot(_rms(p[:, _C_CQ:_C_CKV], gqa_ref[...]).astype(BF16), wuq_ref[...])
    ckv = _rms(p[:, _C_CKV:_C_QS], gkva_ref[...])
    ckv_b = ckv.astype(BF16)
    kn_mla = _dot(ckv_b, wuk_ref[...])
    vmt_ref[...] = _dot_nt(wuvt_ref[...], ckv_b).astype(BF16)
    qs = p[:, _C_QS:_C_KS]
    ks = p[:, _C_KS:_C_VS]
    kpe = p[:, _C_KPE:_C_QSR]
    if rope:
        c64, s64, c32, s32 = c64_ref[...], s64_ref[...], c32_ref[...], s32_ref[...]
        qsr = p[:, _C_QSR:_C_KSR]
        qs = jnp.concatenate(
            [qs[:, j * LANES:(j + 1) * LANES] * c64 + qsr[:, j * LANES:(j + 1) * LANES] * s64 for j in range(2)], axis=1)
        ks = ks * c64 + p[:, _C_KSR:_C_KPER] * s64
        kpe = kpe * c32 + p[:, _C_KPER:_C_END] * s32
    for i in range(N_PAIRS):
        lo = i * MLA_QK_BLK
        qrope = qm[:, lo + LANES:lo + MLA_QK_BLK]
        if rope:
            r0 = N_PAIRS * MLA_QK_BLK + i * LANES
            qrope = qrope * c32 + qm[:, r0:r0 + LANES] * s32
        qm_ref[:, lo:lo + LANES] = qm[:, lo:lo + LANES].astype(BF16)
        qm_ref[:, lo + LANES:lo + MLA_QK_BLK] = qrope.astype(BF16)
        km_ref[:, lo:lo + LANES] = kn_mla[:, i * LANES:(i + 1) * LANES].astype(BF16)
        km_ref[:, lo + LANES:lo + MLA_QK_BLK] = kpe.astype(BF16)
    qs_ref[...] = qs.astype(BF16)
    ks_ref[...] = ks.astype(BF16)
    vs_ref[...] = p[:, _C_VS:_C_QN].astype(BF16)
    qn_ref[...] = p[:, _C_QN:_C_KN].astype(BF16)
    kn_ref[...] = p[:, _C_KN:_C_VN].astype(BF16)
    vn_ref[...] = p[:, _C_VN:_C_KPE].astype(BF16)
    if not rope:
        ckv_o[...] = ckv
        kpe_o[...] = kpe[:, :MLA_ROPE]
        ks_o[...] = ks
        vs_o[...] = p[:, _C_VS:_C_QN]
        kn_o[...] = p[:, _C_KN:_C_VN]
        vn_o[...] = p[:, _C_VN:_C_KPE]


def _projections(x, mod_l, lw, tabs, rope):
    n_tok = x.shape[0]
    tm = TM_PROJ
    tiles_per_seq = DEC_SEQ // tm
    if rope:
        mod_idx = lambda i: (1 + i // tiles_per_seq, 0, 0)
        win, wuq = lw["win_lat"], lw["wuq_lat"]
    else:
        mod_idx = lambda i: (0, 0, 0)
        win, wuq = lw["win_ctx"], lw["wuq_ctx"]
    row = lambda w: pl.BlockSpec((tm, w), lambda i: (i, 0))
    whole = lambda a: pl.BlockSpec(a.shape, lambda i: (0,) * a.ndim)
    in_specs = [row(D_MODEL), pl.BlockSpec((1, N_MOD, D_MODEL), mod_idx), whole(lw["norm1"]), whole(win),
                whole(lw["g_qa"]), whole(wuq), whole(lw["g_kva"]), whole(lw["wuk"]), whole(lw["wuvt"])]
    args = [x, mod_l, lw["norm1"], win, lw["g_qa"], wuq, lw["g_kva"], lw["wuk"], lw["wuvt"]]
    widths = [N_PAIRS * MLA_QK_BLK, N_PAIRS * MLA_QK_BLK, None, 256, 128, 128, 256, 256, 256]
    vmt_spec = pl.BlockSpec((N_PAIRS * LANES, tm), lambda i: (0, i))
    out_specs = [vmt_spec if w is None else row(w) for w in widths]
    out_shape = [jax.ShapeDtypeStruct((N_PAIRS * LANES, n_tok) if w is None else (n_tok, w), BF16) for w in widths]
    if rope:
        tab_spec = pl.BlockSpec((tm, LANES), lambda i: (i % tiles_per_seq, 0))
        in_specs += [tab_spec] * 4
        args += list(tabs)
    else:
        cache_w = [MLA_KV_LORA, MLA_ROPE, 128, 128, 256, 256]
        out_specs += [row(w) for w in cache_w]
        out_shape += [jax.ShapeDtypeStruct((n_tok, w), F32) for w in cache_w]
    return pl.pallas_call(
        functools.partial(_proj_kernel, rope),
        grid=(n_tok // tm,),
        in_specs=in_specs,
        out_specs=out_specs,
        out_shape=out_shape,
        compiler_params=_cparams(("arbitrary",)),
        name="proj_lat" if rope else "proj_ctx",
    )(*args)


def _ctxkv_kernel(ckv_ref, kpe_ref, wuk_ref, wuvt_ref, km_ref, vmt_ref):
    ckv_b = ckv_ref[...].astype(BF16)
    kn_mla = _dot(ckv_b, wuk_ref[...])
    kpe = kpe_ref[...].astype(BF16)
    for i in range(N_PAIRS):
        lo = i * MLA_QK_BLK
        km_ref[:, lo:lo + LANES] = kn_mla[:, i * LANES:(i + 1) * LANES].astype(BF16)
        km_ref[:, lo + LANES:lo + MLA_QK_BLK] = kpe
    vmt_ref[...] = _dot_nt(wuvt_ref[...], ckv_b).astype(BF16)


def _expand_cached_mla(ckv_c, kpe_dup, wuk, wuvt):
    n = ckv_c.shape[0]
    whole = lambda a: pl.BlockSpec(a.shape, lambda i: (0,) * a.ndim)
    return pl.pallas_call(
        _ctxkv_kernel,
        grid=(1,),
        in_specs=[whole(ckv_c), whole(kpe_dup), whole(wuk), whole(wuvt)],
        out_specs=[pl.BlockSpec((n, N_PAIRS * MLA_QK_BLK), lambda i: (0, 0)),
                   pl.BlockSpec((N_PAIRS * LANES, n), lambda i: (0, 0))],
        out_shape=[jax.ShapeDtypeStruct((n, N_PAIRS * MLA_QK_BLK), BF16),
                   jax.ShapeDtypeStruct((N_PAIRS * LANES, n), BF16)],
        compiler_params=_cparams(("arbitrary",)),
        name="expand_cached_mla",
    )(ckv_c, kpe_dup, wuk, wuvt)


def _feature_major(v):
    return v.astype(F32).T.astype(BF16)


def _pair_masks(width):
    lane = lax.broadcasted_iota(jnp.int32, (1, width), 1)
    if width == LANES:
        return [lane < HEAD_DIM, lane >= HEAD_DIM]
    m0 = (lane < MLA_NOPE) | ((lane >= LANES) & (lane < LANES + MLA_ROPE))
    m1 = ((lane >= MLA_NOPE) & (lane < LANES)) | ((lane >= LANES + MLA_ROPE) & (lane < LANES + 2 * MLA_ROPE))
    return [m0, m1]


def _pair_attention(q, blocks, sinks=(None, None)):
    masks = _pair_masks(q.shape[1])
    qa = [jnp.where(masks[a], q, jnp.zeros_like(q)) for a in range(2)]
    chunks = [(k, vt, post, lo, min(lo + KV_CHUNK, k.shape[0]))
              for k, vt, post in blocks for lo in range(0, k.shape[0], KV_CHUNK)]

    def scores(c):
        k, _, post, lo, hi = c
        kc = k[lo:hi, :]
        s = [_dot_nt(kc, qa[a]) for a in range(2)]
        return s if post is None else [post(a, s[a], lo, hi) for a in range(2)]

    m, acc = [None, None], [None, None]
    s_next = scores(chunks[0])
    for idx, (_, vt, _, lo, hi) in enumerate(chunks):
        s_cur, s_next = s_next, (scores(chunks[idx + 1]) if idx + 1 < len(chunks) else None)
        ones = jnp.ones((ONES_ROWS, hi - lo), BF16)
        for a in range(2):
            vta = jnp.concatenate([vt[a * HEAD_DIM:(a + 1) * HEAD_DIM, lo:hi], ones], axis=0)
            s = s_cur[a]
            mc = jnp.max(s, axis=0, keepdims=True)
            mn = mc if m[a] is None else jnp.maximum(m[a], mc)
            pv = _dot(vta, jnp.exp2(s - mn).astype(BF16))
            acc[a] = pv if m[a] is None else jnp.exp2(m[a] - mn) * acc[a] + pv
            m[a] = mn
    outs = []
    for a in range(2):
        l = acc[a][HEAD_DIM:HEAD_DIM + 1, :]
        if sinks[a] is not None:
            mf = jnp.maximum(m[a], sinks[a])
            scale = jnp.exp2(m[a] - mf)
            l = scale * l + jnp.exp2(sinks[a] - mf)
            outs.append(acc[a][:HEAD_DIM, :] * (scale / l))
        else:
            outs.append(acc[a][:HEAD_DIM, :] / l)
    return jnp.concatenate(outs, axis=0).T


def _ctx_attn_kernel(sink_ref, qm_ref, km_ref, vmt_ref, qs_ref, ks_ref, vs_ref, qn_ref, kn_ref, vn_ref,
                     om_ref, os_ref, on_ref):
    for i in range(N_PAIRS):
        q = qm_ref[:, i * MLA_QK_BLK:(i + 1) * MLA_QK_BLK]
        k = km_ref[:, i * MLA_QK_BLK:(i + 1) * MLA_QK_BLK]
        vt = vmt_ref[i * LANES:(i + 1) * LANES, :]
        om_ref[:, i * LANES:(i + 1) * LANES] = _pair_attention(q, [(k, vt, None)]).astype(BF16)
    ks, vst = ks_ref[...], _feature_major(vs_ref[...])
    for j in range(2):
        q = qs_ref[:, j * LANES:(j + 1) * LANES]
        sinks = (sink_ref[j] * LOG2E, sink_ref[j + 2] * LOG2E)
        os_ref[:, j * LANES:(j + 1) * LANES] = _pair_attention(q, [(ks, vst, None)], sinks).astype(BF16)
    for j in range(2):
        sl = slice(j * LANES, (j + 1) * LANES)
        blocks = [(kn_ref[:, sl], _feature_major(vn_ref[:, sl]), None)]
        on_ref[:, sl] = _pair_attention(qn_ref[:, sl], blocks).astype(BF16)


def _context_attention(sink_l, qm, km, vmt, qs, ks, vs, qn, kn, vn):
    row = lambda a: (pl.BlockSpec((a.shape[0], SEQ), lambda b: (0, b)) if a is vmt
                     else pl.BlockSpec((SEQ, a.shape[1]), lambda b: (b, 0)))
    ins = [qm, km, vmt, qs, ks, vs, qn, kn, vn]
    widths = [N_PAIRS * LANES, 256, 256]
    return pl.pallas_call(
        _ctx_attn_kernel,
        grid=(BATCH,),
        in_specs=[pl.BlockSpec(memory_space=pltpu.SMEM)] + [row(a) for a in ins],
        out_specs=[pl.BlockSpec((SEQ, w), lambda b: (b, 0)) for w in widths],
        out_shape=[jax.ShapeDtypeStruct((N_CTX_TOK, w), BF16) for w in widths],
        compiler_params=_cparams(("arbitrary",)),
        name="attn_ctx",
    )(sink_l, *ins)


def _lat_mla_kernel(q_ref, kl_ref, vlt_ref, kc_ref, vct_ref, o_ref):
    blocks = [(kl_ref, vlt_ref, None), (kc_ref, vct_ref, None)]
    o_ref[...] = _pair_attention(q_ref[...], blocks).astype(BF16)


def _latent_mla(qm, km, vmt, kmc, vmct):
    nq = DEC_SEQ // TQ
    return pl.pallas_call(
        _lat_mla_kernel,
        grid=(DEC_BATCH, N_PAIRS, nq),
        in_specs=[
            pl.BlockSpec((TQ, MLA_QK_BLK), lambda b, i, t: (b * nq + t, i)),
            pl.BlockSpec((DEC_SEQ, MLA_QK_BLK), lambda b, i, t: (b, i)),
            pl.BlockSpec((LANES, DEC_SEQ), lambda b, i, t: (i, b)),
            pl.BlockSpec((PAST_LEN, MLA_QK_BLK), lambda b, i, t: (b, i)),
            pl.BlockSpec((LANES, PAST_LEN), lambda b, i, t: (i, b)),
        ],
        out_specs=pl.BlockSpec((TQ, LANES), lambda b, i, t: (b * nq + t, i)),
        out_shape=jax.ShapeDtypeStruct((N_LAT_TOK, N_PAIRS * LANES), BF16),
        compiler_params=_cparams(("arbitrary", "arbitrary", "arbitrary")),
        name="attn_lat_mla",
    )(qm, km, vmt, kmc, vmct)


def _lat_swa_kernel(sink_ref, q_ref, k_ref, v_ref, kc_ref, vc_ref, o_ref):
    t = pl.program_id(1)
    q0 = t * TQ
    ws = pl.multiple_of(jnp.clip(q0 - SWA_WINDOW, 0, DEC_SEQ - SWA_KWIN), SWA_WINDOW)
    kw = k_ref[pl.ds(ws, SWA_KWIN), :]
    vw = _feature_major(v_ref[pl.ds(ws, SWA_KWIN), :])
    kc = kc_ref[...].astype(BF16)
    vc = _feature_major(vc_ref[...])
    kpos = ws + lax.broadcasted_iota(jnp.int32, (SWA_KWIN, TQ), 0)
    qpos = q0 + lax.broadcasted_iota(jnp.int32, (SWA_KWIN, TQ), 1)
    in_band = jnp.abs(kpos - qpos) <= SWA_WINDOW

    def band(a, s, lo, hi):
        return jnp.where(in_band[lo:hi, :], s, MASK_VALUE)

    for j in range(2):
        sinks = (sink_ref[j] * LOG2E, sink_ref[j + 2] * LOG2E)
        o_ref[:, j * LANES:(j + 1) * LANES] = _pair_attention(
            q_ref[:, j * LANES:(j + 1) * LANES], [(kw, vw, band), (kc, vc, None)], sinks).astype(BF16)


def _latent_swa(sink_l, qs, ks, vs, ksc, vsc):
    nq = DEC_SEQ // TQ
    return pl.pallas_call(
        _lat_swa_kernel,
        grid=(DEC_BATCH, nq),
        in_specs=[
            pl.BlockSpec(memory_space=pltpu.SMEM),
            pl.BlockSpec((TQ, 256), lambda b, t: (b * nq + t, 0)),
            pl.BlockSpec((DEC_SEQ, LANES), lambda b, t: (b, 0)),
            pl.BlockSpec((DEC_SEQ, LANES), lambda b, t: (b, 0)),
            pl.BlockSpec((PAST_LEN, LANES), lambda b, t: (b, 0)),
            pl.BlockSpec((PAST_LEN, LANES), lambda b, t: (b, 0)),
        ],
        out_specs=pl.BlockSpec((TQ, 256), lambda b, t: (b * nq + t, 0)),
        out_shape=jax.ShapeDtypeStruct((N_LAT_TOK, 256), BF16),
        compiler_params=_cparams(("arbitrary", "arbitrary")),
        name="attn_lat_swa",
    )(sink_l, qs, ks, vs, ksc, vsc)


def _lat_na_kernel(q_ref, k_ref, v_ref, kc_ref, vc_ref, bias_ref, o_ref):
    t = pl.program_id(1)
    r0 = t * NA_ROWS_PER_TILE
    ws = pl.multiple_of(jnp.clip(r0 - NA_KH // 2, 0, DEC_SEQ // GRID_W - NA_KEY_ROWS) * GRID_W, GRID_W)
    for j in range(2):
        sl = slice(j * LANES, (j + 1) * LANES)
        kw = k_ref[pl.ds(ws, NA_KWIN), sl]
        vw = _feature_major(v_ref[pl.ds(ws, NA_KWIN), sl])
        kc = kc_ref[:, sl].astype(BF16)
        vc = _feature_major(vc_ref[:, sl])

        def add_bias(a, s, lo, hi, j=j):
            return s + bias_ref[0, 2 * j + a, lo:hi, :]

        o_ref[:, sl] = _pair_attention(q_ref[:, sl], [(kw, vw, add_bias), (kc, vc, None)]).astype(BF16)


def _latent_na(qn, kn, vn, knc, vnc, bias):
    nq = DEC_SEQ // TQ
    kind = lambda t: jnp.where(t == 0, 0, jnp.where(t == nq - 1, 2, 1))
    return pl.pallas_call(
        _lat_na_kernel,
        grid=(DEC_BATCH, nq),
        in_specs=[
            pl.BlockSpec((TQ, 256), lambda b, t: (b * nq + t, 0)),
            pl.BlockSpec((DEC_SEQ, 256), lambda b, t: (b, 0)),
            pl.BlockSpec((DEC_SEQ, 256), lambda b, t: (b, 0)),
            pl.BlockSpec((PAST_LEN, 256), lambda b, t: (b, 0)),
            pl.BlockSpec((PAST_LEN, 256), lambda b, t: (b, 0)),
            pl.BlockSpec((1, NA_HEADS, NA_KWIN, TQ), lambda b, t: (kind(t), 0, 0, 0)),
        ],
        out_specs=pl.BlockSpec((TQ, 256), lambda b, t: (b * nq + t, 0)),
        out_shape=jax.ShapeDtypeStruct((N_LAT_TOK, 256), BF16),
        compiler_params=_cparams(("arbitrary", "arbitrary")),
        name="attn_lat_na",
    )(qn, kn, vn, knc, vnc, bias)


_GRP_LANE0 = 0
_EXP_LANE0 = N_GROUPS


def _lane_first_max(x, valid, lane):
    xm = jnp.where(valid, x, -jnp.inf)
    mx = jnp.max(xm, axis=-1, keepdims=True)
    idx = jnp.min(jnp.where(valid & (xm == mx), lane, LANES), axis=-1, keepdims=True)
    return mx, idx


def _tail_kernel(x_ref, om_ref, os_ref, on_ref, mod_ref, n2_ref, wo_ref, wr_ref, x1_ref, h2_ref, gates_ref):
    m = mod_ref[0]
    wo = wo_ref
    attn = (_dot(om_ref[...], wo[0:512, :]) + _dot(os_ref[...], wo[512:768, :]) + _dot(on_ref[...], wo[768:1024, :]))
    x1 = x_ref[...] + m[2:3] * attn
    x1_ref[...] = x1
    h2 = _rms(x1, n2_ref[...]) * (1.0 + m[4:5]) + m[3:4]
    h2_ref[...] = h2.astype(BF16)
    hi = h2.astype(BF16)
    lo = (h2 - hi.astype(F32)).astype(BF16)
    a = _dot(hi, wr_ref[...])
    logits = a[:, :LANES] + a[:, LANES:] + _dot(lo, wr_ref[:, :LANES])
    lane = lax.broadcasted_iota(jnp.int32, logits.shape, 1)
    is_grp = lane < N_GROUPS
    gmax, gidx = _lane_first_max(logits, is_grp, lane)
    gden = jnp.sum(jnp.where(is_grp, jnp.exp(logits - gmax), 0.0), axis=-1, keepdims=True)
    grp_gate = 1.0 / gden
    in_grp = (lane >= _EXP_LANE0) & (lane < _EXP_LANE0 + N_EXPERTS) & ((lane // EXPERTS_PER_GROUP - 1) == gidx)
    v1, i1 = _lane_first_max(logits, in_grp, lane)
    v2, i2 = _lane_first_max(logits, in_grp & (lane != i1), lane)
    e2 = jnp.exp(v2 - v1)
    w1 = grp_gate / (1.0 + e2)
    w2 = grp_gate * e2 / (1.0 + e2)
    gates_ref[...] = jnp.where(lane == i1, w1, 0.0) + jnp.where(lane == i2, w2, 0.0)


def _tail(x, om, osw, ona, mod_l, lw, latent):
    n_tok = x.shape[0]
    tm = TM_TAIL
    tiles_per_seq = DEC_SEQ // tm
    mod_idx = (lambda i: (1 + i // tiles_per_seq, 0, 0)) if latent else (lambda i: (0, 0, 0))
    row = lambda w: pl.BlockSpec((tm, w), lambda i: (i, 0))
    whole = lambda a: pl.BlockSpec(a.shape, lambda i: (0,) * a.ndim)
    return pl.pallas_call(
        _tail_kernel,
        grid=(n_tok // tm,),
        in_specs=[row(D_MODEL), row(512), row(256), row(256), pl.BlockSpec((1, N_MOD, D_MODEL), mod_idx),
                  whole(lw["norm2"]), whole(lw["wout"]), whole(lw["wr"])],
        out_specs=[row(D_MODEL), row(D_MODEL), row(LANES)],
        out_shape=[jax.ShapeDtypeStruct((n_tok, D_MODEL), F32), jax.ShapeDtypeStruct((n_tok, D_MODEL), BF16),
                   jax.ShapeDtypeStruct((n_tok, LANES), F32)],
        compiler_params=_cparams(("arbitrary",)),
        name="tail_lat" if latent else "tail_ctx",
    )(x, om, osw, ona, mod_l, lw["norm2"], lw["wout"], lw["wr"])


def _moe_kernel(final, h2_ref, gates_ref, x1_ref, mod_ref, nf_ref, wg_ref, wu_ref, wd_ref, o_ref, acc_ref):
    g = pl.program_id(1)
    h2 = h2_ref[...]
    gates = gates_ref[...]
    lane = lax.broadcasted_iota(jnp.int32, gates.shape, 1)
    contrib = None
    for e in range(EXPERTS_PER_GROUP):
        ge = jnp.sum(jnp.where(lane == _EXP_LANE0 + g * EXPERTS_PER_GROUP + e, gates, 0.0), axis=-1, keepdims=True)
        hg = _dot(h2, wg_ref[0, e])
        hu = _dot(h2, wu_ref[0, e])
        act = hg * (1.0 / (1.0 + jnp.exp(-hg))) * hu * ge
        c = _dot(act.astype(BF16), wd_ref[0, e])
        contrib = c if contrib is None else contrib + c

    @pl.when(g == 0)
    def _():
        acc_ref[...] = contrib

    @pl.when(g > 0)
    def _():
        acc_ref[...] += contrib

    @pl.when(g == N_GROUPS - 1)
    def _():
        y = x1_ref[...] + mod_ref[0][5:6] * acc_ref[...]
        if final:
            y = _rms(y, nf_ref[...])
        o_ref[...] = y


def _moe(h2, gates, x1, mod_l, l, moe_w, norm_final, latent, final):
    n_tok = h2.shape[0]
    tm = TM_MOE
    tiles_per_seq = DEC_SEQ // tm
    mod_idx = (lambda i, g: (1 + i // tiles_per_seq, 0, 0)) if latent else (lambda i, g: (0, 0, 0))
    epg = EXPERTS_PER_GROUP
    return pl.pallas_call(
        functools.partial(_moe_kernel, final),
        grid=(n_tok // tm, N_GROUPS),
        in_specs=[
            pl.BlockSpec((tm, D_MODEL), lambda i, g: (i, 0)),
            pl.BlockSpec((tm, LANES), lambda i, g: (i, 0)),
            pl.BlockSpec((tm, D_MODEL), lambda i, g: (i, 0)),
            pl.BlockSpec((1, N_MOD, D_MODEL), mod_idx),
            pl.BlockSpec((1, D_MODEL), lambda i, g: (0, 0)),
            pl.BlockSpec((1, epg, D_MODEL, EXPERT_FF), lambda i, g: (l, g, 0, 0)),
            pl.BlockSpec((1, epg, D_MODEL, EXPERT_FF), lambda i, g: (l, g, 0, 0)),
            pl.BlockSpec((1, epg, EXPERT_FF, D_MODEL), lambda i, g: (l, g, 0, 0)),
        ],
        out_specs=pl.BlockSpec((tm, D_MODEL), lambda i, g: (i, 0)),
        out_shape=jax.ShapeDtypeStruct((n_tok, D_MODEL), F32),
        scratch_shapes=[pltpu.VMEM((tm, D_MODEL), F32)],
        compiler_params=_cparams(("arbitrary", "arbitrary")),
        name=("moe_lat" if latent else "moe_ctx") + ("_final" if final else ""),
    )(h2, gates, x1, mod_l, norm_final, *moe_w)


def _rot_cols(w, d):
    k, n = w.shape
    hh = d // 4
    w4 = w.reshape(k, n // (2 * hh), 2, hh)
    return jnp.stack([-w4[:, :, 1], w4[:, :, 0]], axis=2).reshape(k, n)


def _heads(w, d, order, axis):
    parts = [lax.slice_in_dim(w, h * d, (h + 1) * d, axis=axis) for h in order]
    return jnp.concatenate(parts, axis=axis)


def _layer_weights(l, norm1, norm2, w_in, g_qa, w_uq, g_kva, w_ukv, w_out, w_router_grp, w_router_exp):
    wi = w_in[l]
    z64 = jnp.zeros((D_MODEL, 64), F32)
    cq, ckv, kpe = wi[:, 0:256], wi[:, 256:384], wi[:, 384:416]
    qs = _heads(wi[:, 416:672], HEAD_DIM, (0, 2, 1, 3), 1) * (HEAD_SCALE * LOG2E)
    ks, vs = wi[:, 672:800], wi[:, 800:928]
    qn, kn, vn = wi[:, 928:1184] * (HEAD_SCALE * LOG2E), wi[:, 1184:1440], wi[:, 1440:1696]
    kped = jnp.concatenate([kpe, kpe, z64], axis=1)
    kper = _rot_cols(kpe, MLA_ROPE)
    ctx_cols = [cq, ckv, qs, ks, vs, qn, kn, vn, kped]
    lat_cols = ctx_cols + [_rot_cols(qs, HEAD_DIM), _rot_cols(ks, HEAD_DIM), jnp.concatenate([kper, kper, z64], axis=1)]
    wq = w_uq[l].reshape(MLA_Q_LORA, MLA_HEADS, MLA_NOPE + MLA_ROPE) * (MLA_SCALE * LOG2E)
    nope, ropew = wq[:, :, :MLA_NOPE], wq[:, :, MLA_NOPE:]
    z64q = jnp.zeros((MLA_Q_LORA, 64), F32)
    blocks, rots = [], []
    for i in range(N_PAIRS):
        blocks += [nope[:, 2 * i], nope[:, 2 * i + 1], ropew[:, 2 * i], ropew[:, 2 * i + 1], z64q]
        rots += [_rot_cols(ropew[:, 2 * i], MLA_ROPE), _rot_cols(ropew[:, 2 * i + 1], MLA_ROPE), z64q]
    wkv = w_ukv[l].reshape(MLA_KV_LORA, MLA_HEADS, MLA_NOPE + MLA_V)
    wuk = wkv[:, :, :MLA_NOPE].reshape(MLA_KV_LORA, -1)
    wuvt = wkv[:, :, MLA_NOPE:].reshape(MLA_KV_LORA, -1).T
    wo = w_out[l]
    wout = jnp.concatenate([wo[:512], _heads(wo[512:768], HEAD_DIM, (0, 2, 1, 3), 0), wo[768:]], axis=0)
    wr = jnp.concatenate([w_router_grp[l], w_router_exp[l], jnp.zeros((D_MODEL, LANES - N_GROUPS - N_EXPERTS), F32)],
                         axis=1)
    wr_hi = wr.astype(BF16)
    wr_lo = (wr - wr_hi.astype(F32)).astype(BF16)
    return {
        "norm1": norm1[l][None], "norm2": norm2[l][None], "g_qa": g_qa[l][None], "g_kva": g_kva[l][None],
        "win_ctx": jnp.concatenate(ctx_cols, axis=1).astype(BF16),
        "win_lat": jnp.concatenate(lat_cols, axis=1).astype(BF16),
        "wuq_ctx": jnp.concatenate(blocks, axis=1).astype(BF16),
        "wuq_lat": jnp.concatenate(blocks + rots, axis=1).astype(BF16),
        "wuk": wuk.astype(BF16),
        "wuvt": wuvt.astype(BF16),
        "wout": wout.astype(BF16),
        "wr": jnp.concatenate([wr_hi, wr_lo], axis=1),
    }


def _rope_tables():
    n_rows = DEC_SEQ // GRID_W
    coord = jnp.arange(max(n_rows, GRID_W), dtype=F32)

    def tab(d):
        hh = d // 4
        inv = ROPE_THETA ** (-jnp.arange(hh, dtype=F32) / hh)
        ang = coord[:, None] * inv[None, :]
        by_row = lambda t: jnp.repeat(t[:n_rows], GRID_W, axis=0)
        by_col = lambda t: jnp.tile(t[:GRID_W], (n_rows, 1))
        cos, sin = jnp.cos(ang), jnp.sin(ang)
        c = jnp.concatenate([by_row(cos), by_row(cos), by_col(cos), by_col(cos)], axis=1)
        s = jnp.concatenate([by_row(sin), by_row(sin), by_col(sin), by_col(sin)], axis=1)
        return c, s

    c64, s64 = tab(HEAD_DIM)
    c32, s32 = tab(MLA_ROPE)
    z = jnp.zeros((DEC_SEQ, 64), F32)
    return (jnp.concatenate([c64, c64], axis=1), jnp.concatenate([s64, s64], axis=1),
            jnp.concatenate([c32, c32, z], axis=1), jnp.concatenate([s32, s32, z], axis=1))


def _na_bias_tables(rpb_l):
    cidx = np.arange(GRID_W)
    col_start = np.clip(cidx - NA_KW // 2, 0, GRID_W - NA_KW)
    col_ok = (cidx[None, :] >= col_start[:, None]) & (cidx[None, :] < col_start[:, None] + NA_KW)
    pad = GRID_W - NA_KW
    ext = jnp.concatenate([jnp.repeat(rpb_l[:, :, :1], pad, axis=2), rpb_l, jnp.repeat(rpb_l[:, :, -1:], pad, axis=2)],
                          axis=2) * LOG2E
    toep = jnp.stack([ext[:, :, GRID_W - 1 - q:2 * GRID_W - 1 - q] for q in range(GRID_W)], axis=3)
    tt = jnp.where(col_ok.T[None, None], toep, MASK_VALUE)
    masked = jnp.full((NA_HEADS, GRID_W, GRID_W), MASK_VALUE, F32)
    n_rows = DEC_SEQ // GRID_W
    kinds = []
    for r0 in (0, NA_ROWS_PER_TILE, n_rows - NA_ROWS_PER_TILE):
        ws = int(np.clip(r0 - NA_KH // 2, 0, n_rows - NA_KEY_ROWS))
        key_rows = []
        for jj in range(NA_KEY_ROWS):
            kr = ws + jj
            blks = []
            for a in range(NA_ROWS_PER_TILE):
                r = r0 + a
                rs = int(np.clip(r - NA_KH // 2, 0, n_rows - NA_KH))
                blks.append(tt[:, kr - r + NA_KH - 1] if rs <= kr < rs + NA_KH else masked)
            key_rows.append(jnp.concatenate(blks, axis=2))
        kinds.append(jnp.concatenate(key_rows, axis=1))
    return jnp.stack(kinds, axis=0)


def kernel(x_prompt, x_sample, cache_mla_ckv, cache_mla_kpe, cache_swa_k, cache_swa_v, cache_na_k, cache_na_v, c, c_ctx, w_mod, b_mod, norm1, norm2, w_in, g_qa, w_uq, g_kva, w_ukv, swa_sink, na_rpb, w_out, w_router_grp, w_router_exp, w_gate, w_up, w_down, norm_final):
    cpad = jnp.concatenate([c_ctx[None], c, jnp.zeros((8 - 1 - DEC_BATCH, D_MODEL), F32)], axis=0)
    mod = _modulation(cpad, w_mod, b_mod).reshape(DEPTH, 8, N_MOD, D_MODEL)
    tabs = _rope_tables()
    nf = norm_final[None]
    xp = x_prompt.reshape(N_CTX_TOK, D_MODEL)
    xs = x_sample.reshape(N_LAT_TOK, D_MODEL)
    caches = [[] for _ in range(6)]
    moe_w = (w_gate.astype(BF16), w_up.astype(BF16), w_down.astype(BF16))
    for l in range(DEPTH):
        lw = _layer_weights(l, norm1, norm2, w_in, g_qa, w_uq, g_kva, w_ukv, w_out, w_router_grp, w_router_exp)
        final = l == DEPTH - 1
        outs = _projections(xp, mod[l], lw, None, rope=False)
        om, osw, ona = _context_attention(swa_sink[l], *outs[:9])
        for dst, a in zip(caches, outs[9:]):
            dst.append(a)
        x1, h2, gates = _tail(xp, om, osw, ona, mod[l], lw, latent=False)
        xp = _moe(h2, gates, x1, mod[l], l, moe_w, nf, latent=False, final=final)
        qm, km, vm, qs, ks, vs, qn, kn, vn = _projections(xs, mod[l], lw, tabs, rope=True)
        kpe_c = cache_mla_kpe[:, l].reshape(DEC_BATCH * PAST_LEN, MLA_ROPE)
        kpe_dup = jnp.concatenate([kpe_c, kpe_c, jnp.zeros((DEC_BATCH * PAST_LEN, 64), F32)], axis=1)
        kmc, vmc = _expand_cached_mla(cache_mla_ckv[:, l].reshape(DEC_BATCH * PAST_LEN, MLA_KV_LORA), kpe_dup,
                                      lw["wuk"], lw["wuvt"])
        om = _latent_mla(qm, km, vm, kmc, vmc)
        flat = lambda a: a[:, l].reshape(DEC_BATCH * PAST_LEN, -1)
        osw = _latent_swa(swa_sink[l], qs, ks, vs, flat(cache_swa_k), flat(cache_swa_v))
        ona = _latent_na(qn, kn, vn, flat(cache_na_k), flat(cache_na_v), _na_bias_tables(na_rpb[l]))
        x1, h2, gates = _tail(xs, om, osw, ona, mod[l], lw, latent=True)
        xs = _moe(h2, gates, x1, mod[l], l, moe_w, nf, latent=True, final=final)
    stack = lambda parts, tail: jnp.stack([p.reshape((BATCH, SEQ) + tail) for p in parts], axis=1)
    return (xp.reshape(BATCH, SEQ, D_MODEL), xs.reshape(DEC_BATCH, DEC_SEQ, D_MODEL),
            stack(caches[0], (MLA_KV_LORA,)), stack(caches[1], (MLA_ROPE,)),
            stack(caches[2], (SWA_KV_HEADS, HEAD_DIM)), stack(caches[3], (SWA_KV_HEADS, HEAD_DIM)),
            stack(caches[4], (NA_HEADS, HEAD_DIM)), stack(caches[5], (NA_HEADS, HEAD_DIM)))
```

```python
import functools

import jax
import jax.numpy as jnp
import numpy as np
from jax import lax
from jax.experimental import pallas as pl
from jax.experimental.pallas import tpu as pltpu

D_MODEL = 1024
BATCH = 32
SEQ = 256
DEPTH = 2
DEC_BATCH = 2
DEC_SEQ = 4096
PAST_LEN = 512
GRID_W = 64
HEAD_DIM = 64
MLA_HEADS = 8
MLA_Q_LORA = 256
MLA_KV_LORA = 128
MLA_NOPE = 64
MLA_ROPE = 32
MLA_V = 64
SWA_HEADS = 4
SWA_KV_HEADS = 2
SWA_WINDOW = 128
NA_HEADS = 4
NA_KH = 8
NA_KW = 16
N_GROUPS = 4
EXPERTS_PER_GROUP = 4
N_EXPERTS = 16
EXPERT_FF = 256
N_MOD = 6
ROPE_THETA = 10000.0
EPS = 1e-6
MASK_VALUE = -1e30
MLA_SCALE = (MLA_NOPE + MLA_ROPE) ** -0.5
HEAD_SCALE = HEAD_DIM ** -0.5
LOG2E = 1.4426950408889634

LANES = 128
N_PAIRS = MLA_HEADS // 2
MLA_QK_BLK = 2 * LANES
N_CTX_TOK = BATCH * SEQ
N_LAT_TOK = DEC_BATCH * DEC_SEQ

_C_CQ, _C_CKV, _C_QS, _C_KS, _C_VS, _C_QN, _C_KN, _C_VN, _C_KPE, _C_QSR, _C_KSR, _C_KPER, _C_END = (
    0, 256, 384, 640, 768, 896, 1152, 1408, 1664, 1792, 2048, 2176, 2304)

TM_PROJ = 512
TM_TAIL = 512
TM_MOE = 1024
TQ = 256
KV_CHUNK = 2048
ONES_ROWS = 16
SWA_KWIN = TQ + 2 * SWA_WINDOW
NA_ROWS_PER_TILE = TQ // GRID_W
NA_KEY_ROWS = 12
NA_KWIN = NA_KEY_ROWS * GRID_W
VMEM_LIMIT = 56 * 1024 * 1024

F32 = jnp.float32
BF16 = jnp.bfloat16


def _dot(a, b):
    return jnp.dot(a, b, preferred_element_type=F32)


def _dot_nt(a, b):
    return lax.dot_general(a, b, (((1,), (1,)), ((), ())), preferred_element_type=F32)


def _rms(x, g):
    return x * lax.rsqrt(jnp.mean(x * x, axis=-1, keepdims=True) + EPS) * g


def _cparams(sem):
    return pltpu.CompilerParams(dimension_semantics=sem, vmem_limit_bytes=VMEM_LIMIT)


def _mod_kernel(c_ref, w_ref, b_ref, o_ref):
    c = c_ref[...]
    s = c * (1.0 / (1.0 + jnp.exp(-c)))
    o_ref[0] = jnp.dot(s, w_ref[0], preferred_element_type=F32, precision=lax.Precision.HIGHEST) + b_ref[0]


def _modulation(cpad, w_mod, b_mod):
    nt = 1024
    return pl.pallas_call(
        _mod_kernel,
        grid=(DEPTH, N_MOD * D_MODEL // nt),
        in_specs=[
            pl.BlockSpec((8, D_MODEL), lambda l, n: (0, 0)),
            pl.BlockSpec((1, D_MODEL, nt), lambda l, n: (l, 0, n)),
            pl.BlockSpec((1, 1, nt), lambda l, n: (l, 0, n)),
        ],
        out_specs=pl.BlockSpec((1, 8, nt), lambda l, n: (l, 0, n)),
        out_shape=jax.ShapeDtypeStruct((DEPTH, 8, N_MOD * D_MODEL), F32),
        compiler_params=_cparams(("arbitrary", "arbitrary")),
        name="modulation",
    )(cpad, w_mod, b_mod.reshape(DEPTH, 1, N_MOD * D_MODEL))


def _proj_kernel(rope, *refs):
    if rope:
        (x_ref, mod_ref, n1_ref, win_ref, gqa_ref, wuq_ref, gkva_ref, wuk_ref, wuvt_ref,
         c64_ref, s64_ref, c32_ref, s32_ref,
         qm_ref, km_ref, vmt_ref, qs_ref, ks_ref, vs_ref, qn_ref, kn_ref, vn_ref) = refs
    else:
        (x_ref, mod_ref, n1_ref, win_ref, gqa_ref, wuq_ref, gkva_ref, wuk_ref, wuvt_ref,
         qm_ref, km_ref, vmt_ref, qs_ref, ks_ref, vs_ref, qn_ref, kn_ref, vn_ref,
         ckv_o, kpe_o, ks_o, vs_o, kn_o, vn_o) = refs
    m = mod_ref[0]
    h = _rms(x_ref[...], n1_ref[...]) * (1.0 + m[1:2]) + m[0:1]
    p = _dot(h.astype(BF16), win_ref[...])
    qm = _dot(_rms(p[:, _C_CQ:_C_CKV], gqa_ref[...]).astype(BF16), wuq_ref[...])
    ckv = _rms(p[:, _C_CKV:_C_QS], gkva_ref[...])
    ckv_b = ckv.astype(BF16)
    kn_mla = _dot(ckv_b, wuk_ref[...])
    vmt_ref[...] = _dot_nt(wuvt_ref[...], ckv_b).astype(BF16)
    qs = p[:, _C_QS:_C_KS]
    ks = p[:, _C_KS:_C_VS]
    kpe = p[:, _C_KPE:_C_QSR]
    if rope:
        c64, s64, c32, s32 = c64_ref[...], s64_ref[...], c32_ref[...], s32_ref[...]
        qsr = p[:, _C_QSR:_C_KSR]
        qs = jnp.concatenate(
            [qs[:, j * LANES:(j + 1) * LANES] * c64 + qsr[:, j * LANES:(j + 1) * LANES] * s64 for j in range(2)], axis=1)
        ks = ks * c64 + p[:, _C_KSR:_C_KPER] * s64
        kpe = kpe * c32 + p[:, _C_KPER:_C_END] * s32
    for i in range(N_PAIRS):
        lo = i * MLA_QK_BLK
        qrope = qm[:, lo + LANES:lo + MLA_QK_BLK]
        if rope:
            r0 = N_PAIRS * MLA_QK_BLK + i * LANES
            qrope = qrope * c32 + qm[:, r0:r0 + LANES] * s32
        qm_ref[:, lo:lo + LANES] = qm[:, lo:lo + LANES].astype(BF16)
        qm_ref[:, lo + LANES:lo + MLA_QK_BLK] = qrope.astype(BF16)
        km_ref[:, lo:lo + LANES] = kn_mla[:, i * LANES:(i + 1) * LANES].astype(BF16)
        km_ref[:, lo + LANES:lo + MLA_QK_BLK] = kpe.astype(BF16)
    qs_ref[...] = qs.astype(BF16)
    ks_ref[...] = ks.astype(BF16)
    vs_ref[...] = p[:, _C_VS:_C_QN].astype(BF16)
    qn_ref[...] = p[:, _C_QN:_C_KN].astype(BF16)
    kn_ref[...] = p[:, _C_KN:_C_VN].astype(BF16)
    vn_ref[...] = p[:, _C_VN:_C_KPE].astype(BF16)
    if not rope:
        ckv_o[...] = ckv
        kpe_o[...] = kpe[:, :MLA_ROPE]
        ks_o[...] = ks
        vs_o[...] = p[:, _C_VS:_C_QN]
        kn_o[...] = p[:, _C_KN:_C_VN]
        vn_o[...] = p[:, _C_VN:_C_KPE]


def _projections(x, mod_l, lw, tabs, rope):
    n_tok = x.shape[0]
    tm = TM_PROJ
    tiles_per_seq = DEC_SEQ // tm
    win, wuq = lw["win"], lw["wuq"]
    if rope:
        mod_idx = lambda i: (1 + i // tiles_per_seq, 0, 0)
        win_cols, wuq_cols = _C_END, wuq.shape[1]
    else:
        mod_idx = lambda i: (0, 0, 0)
        win_cols, wuq_cols = _C_QSR, N_PAIRS * MLA_QK_BLK
    row = lambda w: pl.BlockSpec((tm, w), lambda i: (i, 0))
    whole = lambda a: pl.BlockSpec(a.shape, lambda i: (0,) * a.ndim)
    lead = lambda a, n: pl.BlockSpec((a.shape[0], n), lambda i: (0, 0))
    in_specs = [row(D_MODEL), pl.BlockSpec((1, N_MOD, D_MODEL), mod_idx), whole(lw["norm1"]), lead(win, win_cols),
                whole(lw["g_qa"]), lead(wuq, wuq_cols), whole(lw["g_kva"]), whole(lw["wuk"]), whole(lw["wuvt"])]
    args = [x, mod_l, lw["norm1"], win, lw["g_qa"], wuq, lw["g_kva"], lw["wuk"], lw["wuvt"]]
    widths = [N_PAIRS * MLA_QK_BLK, N_PAIRS * MLA_QK_BLK, None, 256, 128, 128, 256, 256, 256]
    vmt_spec = pl.BlockSpec((N_PAIRS * LANES, tm), lambda i: (0, i))
    out_specs = [vmt_spec if w is None else row(w) for w in widths]
    out_shape = [jax.ShapeDtypeStruct((N_PAIRS * LANES, n_tok) if w is None else (n_tok, w), BF16) for w in widths]
    if rope:
        tab_spec = pl.BlockSpec((tm, LANES), lambda i: (i % tiles_per_seq, 0))
        in_specs += [tab_spec] * 4
        args += list(tabs)
    else:
        cache_w = [MLA_KV_LORA, MLA_ROPE, 128, 128, 256, 256]
        out_specs += [row(w) for w in cache_w]
        out_shape += [jax.ShapeDtypeStruct((n_tok, w), F32) for w in cache_w]
    return pl.pallas_call(
        functools.partial(_proj_kernel, rope),
        grid=(n_tok // tm,),
        in_specs=in_specs,
        out_specs=out_specs,
        out_shape=out_shape,
        compiler_params=_cparams(("arbitrary",)),
        name="proj_lat" if rope else "proj_ctx",
    )(*args)


def _ctxkv_kernel(ckv_ref, kpe_ref, wuk_ref, wuvt_ref, km_ref, vmt_ref):
    ckv_b = ckv_ref[...].astype(BF16)
    kn_mla = _dot(ckv_b, wuk_ref[...])
    kpe = kpe_ref[...].astype(BF16)
    for i in range(N_PAIRS):
        lo = i * MLA_QK_BLK
        km_ref[:, lo:lo + LANES] = kn_mla[:, i * LANES:(i + 1) * LANES].astype(BF16)
        km_ref[:, lo + LANES:lo + MLA_QK_BLK] = kpe
    vmt_ref[...] = _dot_nt(wuvt_ref[...], ckv_b).astype(BF16)


def _expand_cached_mla(ckv_c, kpe_dup, wuk, wuvt):
    n = ckv_c.shape[0]
    whole = lambda a: pl.BlockSpec(a.shape, lambda i: (0,) * a.ndim)
    return pl.pallas_call(
        _ctxkv_kernel,
        grid=(1,),
        in_specs=[whole(ckv_c), whole(kpe_dup), whole(wuk), whole(wuvt)],
        out_specs=[pl.BlockSpec((n, N_PAIRS * MLA_QK_BLK), lambda i: (0, 0)),
                   pl.BlockSpec((N_PAIRS * LANES, n), lambda i: (0, 0))],
        out_shape=[jax.ShapeDtypeStruct((n, N_PAIRS * MLA_QK_BLK), BF16),
                   jax.ShapeDtypeStruct((N_PAIRS * LANES, n), BF16)],
        compiler_params=_cparams(("arbitrary",)),
        name="expand_cached_mla",
    )(ckv_c, kpe_dup, wuk, wuvt)


def _feature_major(v):
    return v.astype(F32).T.astype(BF16)


def _pair_masks(width):
    lane = lax.broadcasted_iota(jnp.int32, (1, width), 1)
    if width == LANES:
        return [lane < HEAD_DIM, lane >= HEAD_DIM]
    m0 = (lane < MLA_NOPE) | ((lane >= LANES) & (lane < LANES + MLA_ROPE))
    m1 = ((lane >= MLA_NOPE) & (lane < LANES)) | ((lane >= LANES + MLA_ROPE) & (lane < LANES + 2 * MLA_ROPE))
    return [m0, m1]


def _attention(jobs):
    steps, qa = [], []
    for j, (q, blocks, _) in enumerate(jobs):
        masks = _pair_masks(q.shape[1])
        qa.append([jnp.where(masks[a], q, jnp.zeros_like(q)) for a in range(2)])
        chunks = [(k, vt, post, lo, min(lo + KV_CHUNK, k.shape[0]))
                  for k, vt, post in blocks for lo in range(0, k.shape[0], KV_CHUNK)]
        steps += [(j, c, ci == len(chunks) - 1) for ci, c in enumerate(chunks)]

    def scores(step):
        j, (k, _, post, lo, hi), _ = step
        kc = k[lo:hi, :]
        s = [_dot_nt(kc, qa[j][a]) for a in range(2)]
        return s if post is None else [post(a, s[a], lo, hi) for a in range(2)]

    outs = [None] * len(jobs)
    m, acc = [None, None], [None, None]
    s_next = scores(steps[0])
    for n, (j, (_, vt, _, lo, hi), last) in enumerate(steps):
        s_cur, s_next = s_next, (scores(steps[n + 1]) if n + 1 < len(steps) else None)
        ones = jnp.ones((ONES_ROWS, hi - lo), BF16)
        for a in range(2):
            vta = jnp.concatenate([vt[a * HEAD_DIM:(a + 1) * HEAD_DIM, lo:hi], ones], axis=0)
            s = s_cur[a]
            mc = jnp.max(s, axis=0, keepdims=True)
            mn = mc if m[a] is None else jnp.maximum(m[a], mc)
            pv = _dot(vta, jnp.exp2(s - mn).astype(BF16))
            acc[a] = pv if m[a] is None else jnp.exp2(m[a] - mn) * acc[a] + pv
            m[a] = mn
        if last:
            sinks, heads = jobs[j][2], []
            for a in range(2):
                l = acc[a][HEAD_DIM:HEAD_DIM + 1, :]
                if sinks[a] is not None:
                    mf = jnp.maximum(m[a], sinks[a])
                    scale = jnp.exp2(m[a] - mf)
                    l = scale * l + jnp.exp2(sinks[a] - mf)
                    heads.append(acc[a][:HEAD_DIM, :] * (scale / l))
                else:
                    heads.append(acc[a][:HEAD_DIM, :] / l)
            outs[j] = jnp.concatenate(heads, axis=0).T
            m, acc = [None, None], [None, None]
    return outs


_NO_SINKS = (None, None)


def _ctx_attn_kernel(sink_ref, qm_ref, km_ref, vmt_ref, qs_ref, ks_ref, vs_ref, qn_ref, kn_ref, vn_ref,
                     om_ref, os_ref, on_ref):
    jobs = []
    for i in range(N_PAIRS):
        q = qm_ref[:, i * MLA_QK_BLK:(i + 1) * MLA_QK_BLK]
        k = km_ref[:, i * MLA_QK_BLK:(i + 1) * MLA_QK_BLK]
        vt = vmt_ref[i * LANES:(i + 1) * LANES, :]
        jobs.append((q, [(k, vt, None)], _NO_SINKS))
    ks, vst = ks_ref[...], _feature_major(vs_ref[...])
    for j in range(2):
        sinks = (sink_ref[j] * LOG2E, sink_ref[j + 2] * LOG2E)
        jobs.append((qs_ref[:, j * LANES:(j + 1) * LANES], [(ks, vst, None)], sinks))
    for j in range(2):
        sl = slice(j * LANES, (j + 1) * LANES)
        jobs.append((qn_ref[:, sl], [(kn_ref[:, sl], _feature_major(vn_ref[:, sl]), None)], _NO_SINKS))
    outs = _attention(jobs)
    for i in range(N_PAIRS):
        om_ref[:, i * LANES:(i + 1) * LANES] = outs[i].astype(BF16)
    for j in range(2):
        os_ref[:, j * LANES:(j + 1) * LANES] = outs[N_PAIRS + j].astype(BF16)
        on_ref[:, j * LANES:(j + 1) * LANES] = outs[N_PAIRS + 2 + j].astype(BF16)


def _context_attention(sink_l, qm, km, vmt, qs, ks, vs, qn, kn, vn):
    row = lambda a: (pl.BlockSpec((a.shape[0], SEQ), lambda b: (0, b)) if a is vmt
                     else pl.BlockSpec((SEQ, a.shape[1]), lambda b: (b, 0)))
    ins = [qm, km, vmt, qs, ks, vs, qn, kn, vn]
    widths = [N_PAIRS * LANES, 256, 256]
    return pl.pallas_call(
        _ctx_attn_kernel,
        grid=(BATCH,),
        in_specs=[pl.BlockSpec(memory_space=pltpu.SMEM)] + [row(a) for a in ins],
        out_specs=[pl.BlockSpec((SEQ, w), lambda b: (b, 0)) for w in widths],
        out_shape=[jax.ShapeDtypeStruct((N_CTX_TOK, w), BF16) for w in widths],
        compiler_params=_cparams(("arbitrary",)),
        name="attn_ctx",
    )(sink_l, *ins)


def _lat_mla_kernel(q_ref, kl_ref, vlt_ref, kc_ref, vct_ref, o_ref):
    blocks = [(kl_ref, vlt_ref, None), (kc_ref, vct_ref, None)]
    o_ref[...] = _attention([(q_ref[...], blocks, _NO_SINKS)])[0].astype(BF16)


def _latent_mla(qm, km, vmt, kmc, vmct):
    nq = DEC_SEQ // TQ
    return pl.pallas_call(
        _lat_mla_kernel,
        grid=(DEC_BATCH, N_PAIRS, nq),
        in_specs=[
            pl.BlockSpec((TQ, MLA_QK_BLK), lambda b, i, t: (b * nq + t, i)),
            pl.BlockSpec((DEC_SEQ, MLA_QK_BLK), lambda b, i, t: (b, i)),
            pl.BlockSpec((LANES, DEC_SEQ), lambda b, i, t: (i, b)),
            pl.BlockSpec((PAST_LEN, MLA_QK_BLK), lambda b, i, t: (b, i)),
            pl.BlockSpec((LANES, PAST_LEN), lambda b, i, t: (i, b)),
        ],
        out_specs=pl.BlockSpec((TQ, LANES), lambda b, i, t: (b * nq + t, i)),
        out_shape=jax.ShapeDtypeStruct((N_LAT_TOK, N_PAIRS * LANES), BF16),
        compiler_params=_cparams(("arbitrary", "arbitrary", "arbitrary")),
        name="attn_lat_mla",
    )(qm, km, vmt, kmc, vmct)


def _lat_swa_kernel(sink_ref, q_ref, k_ref, v_ref, kc_ref, vc_ref, o_ref):
    t = pl.program_id(1)
    q0 = t * TQ
    ws = pl.multiple_of(jnp.clip(q0 - SWA_WINDOW, 0, DEC_SEQ - SWA_KWIN), SWA_WINDOW)
    kw = k_ref[pl.ds(ws, SWA_KWIN), :]
    vw = _feature_major(v_ref[pl.ds(ws, SWA_KWIN), :])
    kc = kc_ref[...].astype(BF16)
    vc = _feature_major(vc_ref[...])
    kpos = ws + lax.broadcasted_iota(jnp.int32, (SWA_KWIN, TQ), 0)
    qpos = q0 + lax.broadcasted_iota(jnp.int32, (SWA_KWIN, TQ), 1)
    in_band = jnp.abs(kpos - qpos) <= SWA_WINDOW

    def band(a, s, lo, hi):
        return jnp.where(in_band[lo:hi, :], s, MASK_VALUE)

    jobs = []
    for j in range(2):
        sinks = (sink_ref[j] * LOG2E, sink_ref[j + 2] * LOG2E)
        jobs.append((q_ref[:, j * LANES:(j + 1) * LANES], [(kw, vw, band), (kc, vc, None)], sinks))
    for j, o in enumerate(_attention(jobs)):
        o_ref[:, j * LANES:(j + 1) * LANES] = o.astype(BF16)


def _latent_swa(sink_l, qs, ks, vs, ksc, vsc):
    nq = DEC_SEQ // TQ
    return pl.pallas_call(
        _lat_swa_kernel,
        grid=(DEC_BATCH, nq),
        in_specs=[
            pl.BlockSpec(memory_space=pltpu.SMEM),
            pl.BlockSpec((TQ, 256), lambda b, t: (b * nq + t, 0)),
            pl.BlockSpec((DEC_SEQ, LANES), lambda b, t: (b, 0)),
            pl.BlockSpec((DEC_SEQ, LANES), lambda b, t: (b, 0)),
            pl.BlockSpec((PAST_LEN, LANES), lambda b, t: (b, 0)),
            pl.BlockSpec((PAST_LEN, LANES), lambda b, t: (b, 0)),
        ],
        out_specs=pl.BlockSpec((TQ, 256), lambda b, t: (b * nq + t, 0)),
        out_shape=jax.ShapeDtypeStruct((N_LAT_TOK, 256), BF16),
        compiler_params=_cparams(("arbitrary", "arbitrary")),
        name="attn_lat_swa",
    )(sink_l, qs, ks, vs, ksc, vsc)


def _lat_na_kernel(q_ref, k_ref, v_ref, kc_ref, vc_ref, bias_ref, o_ref):
    t = pl.program_id(1)
    r0 = t * NA_ROWS_PER_TILE
    ws = pl.multiple_of(jnp.clip(r0 - NA_KH // 2, 0, DEC_SEQ // GRID_W - NA_KEY_ROWS) * GRID_W, GRID_W)
    jobs = []
    for j in range(2):
        sl = slice(j * LANES, (j + 1) * LANES)
        kw = k_ref[pl.ds(ws, NA_KWIN), sl]
        vw = _feature_major(v_ref[pl.ds(ws, NA_KWIN), sl])
        kc = kc_ref[:, sl].astype(BF16)
        vc = _feature_major(vc_ref[:, sl])

        def add_bias(a, s, lo, hi, j=j):
            return s + bias_ref[0, 2 * j + a, lo:hi, :]

        jobs.append((q_ref[:, sl], [(kw, vw, add_bias), (kc, vc, None)], _NO_SINKS))
    for j, o in enumerate(_attention(jobs)):
        o_ref[:, j * LANES:(j + 1) * LANES] = o.astype(BF16)


def _latent_na(qn, kn, vn, knc, vnc, bias):
    nq = DEC_SEQ // TQ
    kind = lambda t: jnp.where(t == 0, 0, jnp.where(t == nq - 1, 2, 1))
    return pl.pallas_call(
        _lat_na_kernel,
        grid=(DEC_BATCH, nq),
        in_specs=[
            pl.BlockSpec((TQ, 256), lambda b, t: (b * nq + t, 0)),
            pl.BlockSpec((DEC_SEQ, 256), lambda b, t: (b, 0)),
            pl.BlockSpec((DEC_SEQ, 256), lambda b, t: (b, 0)),
            pl.BlockSpec((PAST_LEN, 256), lambda b, t: (b, 0)),
            pl.BlockSpec((PAST_LEN, 256), lambda b, t: (b, 0)),
            pl.BlockSpec((1, NA_HEADS, NA_KWIN, TQ), lambda b, t: (kind(t), 0, 0, 0)),
        ],
        out_specs=pl.BlockSpec((TQ, 256), lambda b, t: (b * nq + t, 0)),
        out_shape=jax.ShapeDtypeStruct((N_LAT_TOK, 256), BF16),
        compiler_params=_cparams(("arbitrary", "arbitrary")),
        name="attn_lat_na",
    )(qn, kn, vn, knc, vnc, bias)


_GRP_LANE0 = 0
_EXP_LANE0 = N_GROUPS


def _lane_first_max(x, valid, lane):
    xm = jnp.where(valid, x, -jnp.inf)
    mx = jnp.max(xm, axis=-1, keepdims=True)
    idx = jnp.min(jnp.where(valid & (xm == mx), lane, LANES), axis=-1, keepdims=True)
    return mx, idx


def _tail_kernel(x_ref, om_ref, os_ref, on_ref, mod_ref, n2_ref, wo_ref, wr_ref, x1_ref, h2_ref, gates_ref):
    m = mod_ref[0]
    wo = wo_ref
    attn = (_dot(om_ref[...], wo[0:512, :]) + _dot(os_ref[...], wo[512:768, :]) + _dot(on_ref[...], wo[768:1024, :]))
    x1 = x_ref[...] + m[2:3] * attn
    x1_ref[...] = x1
    h2 = _rms(x1, n2_ref[...]) * (1.0 + m[4:5]) + m[3:4]
    h2_ref[...] = h2.astype(BF16)
    hi = h2.astype(BF16)
    lo = (h2 - hi.astype(F32)).astype(BF16)
    a = _dot(hi, wr_ref[...])
    logits = a[:, :LANES] + a[:, LANES:] + _dot(lo, wr_ref[:, :LANES])
    lane = lax.broadcasted_iota(jnp.int32, logits.shape, 1)
    is_grp = lane < N_GROUPS
    gmax, gidx = _lane_first_max(logits, is_grp, lane)
    gden = jnp.sum(jnp.where(is_grp, jnp.exp(logits - gmax), 0.0), axis=-1, keepdims=True)
    grp_gate = 1.0 / gden
    in_grp = (lane >= _EXP_LANE0) & (lane < _EXP_LANE0 + N_EXPERTS) & ((lane // EXPERTS_PER_GROUP - 1) == gidx)
    v1, i1 = _lane_first_max(logits, in_grp, lane)
    v2, i2 = _lane_first_max(logits, in_grp & (lane != i1), lane)
    e2 = jnp.exp(v2 - v1)
    w1 = grp_gate / (1.0 + e2)
    w2 = grp_gate * e2 / (1.0 + e2)
    gates_ref[...] = jnp.where(lane == i1, w1, 0.0) + jnp.where(lane == i2, w2, 0.0)


def _tail(x, om, osw, ona, mod_l, lw, latent):
    n_tok = x.shape[0]
    tm = TM_TAIL
    tiles_per_seq = DEC_SEQ // tm
    mod_idx = (lambda i: (1 + i // tiles_per_seq, 0, 0)) if latent else (lambda i: (0, 0, 0))
    row = lambda w: pl.BlockSpec((tm, w), lambda i: (i, 0))
    whole = lambda a: pl.BlockSpec(a.shape, lambda i: (0,) * a.ndim)
    return pl.pallas_call(
        _tail_kernel,
        grid=(n_tok // tm,),
        in_specs=[row(D_MODEL), row(512), row(256), row(256), pl.BlockSpec((1, N_MOD, D_MODEL), mod_idx),
                  whole(lw["norm2"]), whole(lw["wout"]), whole(lw["wr"])],
        out_specs=[row(D_MODEL), row(D_MODEL), row(LANES)],
        out_shape=[jax.ShapeDtypeStruct((n_tok, D_MODEL), F32), jax.ShapeDtypeStruct((n_tok, D_MODEL), BF16),
                   jax.ShapeDtypeStruct((n_tok, LANES), F32)],
        compiler_params=_cparams(("arbitrary",)),
        name="tail_lat" if latent else "tail_ctx",
    )(x, om, osw, ona, mod_l, lw["norm2"], lw["wout"], lw["wr"])


def _moe_kernel(final, h2_ref, gates_ref, x1_ref, mod_ref, nf_ref, wg_ref, wu_ref, wd_ref, o_ref, acc_ref):
    g = pl.program_id(1)
    h2 = h2_ref[...]
    gates = gates_ref[...]
    lane = lax.broadcasted_iota(jnp.int32, gates.shape, 1)
    contrib = None
    for e in range(EXPERTS_PER_GROUP):
        ge = jnp.sum(jnp.where(lane == _EXP_LANE0 + g * EXPERTS_PER_GROUP + e, gates, 0.0), axis=-1, keepdims=True)
        hg = _dot(h2, wg_ref[0, e])
        hu = _dot(h2, wu_ref[0, e])
        act = hg * (1.0 / (1.0 + jnp.exp(-hg))) * hu * ge
        c = _dot(act.astype(BF16), wd_ref[0, e])
        contrib = c if contrib is None else contrib + c

    @pl.when(g == 0)
    def _():
        acc_ref[...] = contrib

    @pl.when(g > 0)
    def _():
        acc_ref[...] += contrib

    @pl.when(g == N_GROUPS - 1)
    def _():
        y = x1_ref[...] + mod_ref[0][5:6] * acc_ref[...]
        if final:
            y = _rms(y, nf_ref[...])
        o_ref[...] = y


def _moe(h2, gates, x1, mod_l, l, moe_w, norm_final, latent, final):
    n_tok = h2.shape[0]
    tm = TM_MOE
    tiles_per_seq = DEC_SEQ // tm
    mod_idx = (lambda i, g: (1 + i // tiles_per_seq, 0, 0)) if latent else (lambda i, g: (0, 0, 0))
    epg = EXPERTS_PER_GROUP
    return pl.pallas_call(
        functools.partial(_moe_kernel, final),
        grid=(n_tok // tm, N_GROUPS),
        in_specs=[
            pl.BlockSpec((tm, D_MODEL), lambda i, g: (i, 0)),
            pl.BlockSpec((tm, LANES), lambda i, g: (i, 0)),
            pl.BlockSpec((tm, D_MODEL), lambda i, g: (i, 0)),
            pl.BlockSpec((1, N_MOD, D_MODEL), mod_idx),
            pl.BlockSpec((1, D_MODEL), lambda i, g: (0, 0)),
            pl.BlockSpec((1, epg, D_MODEL, EXPERT_FF), lambda i, g: (l, g, 0, 0)),
            pl.BlockSpec((1, epg, D_MODEL, EXPERT_FF), lambda i, g: (l, g, 0, 0)),
            pl.BlockSpec((1, epg, EXPERT_FF, D_MODEL), lambda i, g: (l, g, 0, 0)),
        ],
        out_specs=pl.BlockSpec((tm, D_MODEL), lambda i, g: (i, 0)),
        out_shape=jax.ShapeDtypeStruct((n_tok, D_MODEL), F32),
        scratch_shapes=[pltpu.VMEM((tm, D_MODEL), F32)],
        compiler_params=_cparams(("arbitrary", "arbitrary")),
        name=("moe_lat" if latent else "moe_ctx") + ("_final" if final else ""),
    )(h2, gates, x1, mod_l, norm_final, *moe_w)


def _rot_cols(w, d):
    k, n = w.shape
    hh = d // 4
    w4 = w.reshape(k, n // (2 * hh), 2, hh)
    return jnp.stack([-w4[:, :, 1], w4[:, :, 0]], axis=2).reshape(k, n)


def _heads(w, d, order, axis):
    parts = [lax.slice_in_dim(w, h * d, (h + 1) * d, axis=axis) for h in order]
    return jnp.concatenate(parts, axis=axis)


def _layer_weights(l, norm1, norm2, w_in, g_qa, w_uq, g_kva, w_ukv, w_out, w_router_grp, w_router_exp):
    wi = w_in[l]
    z64 = jnp.zeros((D_MODEL, 64), F32)
    cq, ckv, kpe = wi[:, 0:256], wi[:, 256:384], wi[:, 384:416]
    qs = _heads(wi[:, 416:672], HEAD_DIM, (0, 2, 1, 3), 1) * (HEAD_SCALE * LOG2E)
    ks, vs = wi[:, 672:800], wi[:, 800:928]
    qn, kn, vn = wi[:, 928:1184] * (HEAD_SCALE * LOG2E), wi[:, 1184:1440], wi[:, 1440:1696]
    kped = jnp.concatenate([kpe, kpe, z64], axis=1)
    kper = _rot_cols(kpe, MLA_ROPE)
    ctx_cols = [cq, ckv, qs, ks, vs, qn, kn, vn, kped]
    lat_cols = ctx_cols + [_rot_cols(qs, HEAD_DIM), _rot_cols(ks, HEAD_DIM), jnp.concatenate([kper, kper, z64], axis=1)]
    wq = w_uq[l].reshape(MLA_Q_LORA, MLA_HEADS, MLA_NOPE + MLA_ROPE) * (MLA_SCALE * LOG2E)
    nope, ropew = wq[:, :, :MLA_NOPE], wq[:, :, MLA_NOPE:]
    z64q = jnp.zeros((MLA_Q_LORA, 64), F32)
    blocks, rots = [], []
    for i in range(N_PAIRS):
        blocks += [nope[:, 2 * i], nope[:, 2 * i + 1], ropew[:, 2 * i], ropew[:, 2 * i + 1], z64q]
        rots += [_rot_cols(ropew[:, 2 * i], MLA_ROPE), _rot_cols(ropew[:, 2 * i + 1], MLA_ROPE), z64q]
    wkv = w_ukv[l].reshape(MLA_KV_LORA, MLA_HEADS, MLA_NOPE + MLA_V)
    wuk = wkv[:, :, :MLA_NOPE].reshape(MLA_KV_LORA, -1)
    wuvt = wkv[:, :, MLA_NOPE:].reshape(MLA_KV_LORA, -1).T
    wo = w_out[l]
    wout = jnp.concatenate([wo[:512], _heads(wo[512:768], HEAD_DIM, (0, 2, 1, 3), 0), wo[768:]], axis=0)
    wr = jnp.concatenate([w_router_grp[l], w_router_exp[l], jnp.zeros((D_MODEL, LANES - N_GROUPS - N_EXPERTS), F32)],
                         axis=1)
    wr_hi = wr.astype(BF16)
    wr_lo = (wr - wr_hi.astype(F32)).astype(BF16)
    return {
        "norm1": norm1[l][None], "norm2": norm2[l][None], "g_qa": g_qa[l][None], "g_kva": g_kva[l][None],
        "win": jnp.concatenate(lat_cols, axis=1).astype(BF16),
        "wuq": jnp.concatenate(blocks + rots, axis=1).astype(BF16),
        "wuk": wuk.astype(BF16),
        "wuvt": wuvt.astype(BF16),
        "wout": wout.astype(BF16),
        "wr": jnp.concatenate([wr_hi, wr_lo], axis=1),
    }


def _rope_tables():
    n_rows = DEC_SEQ // GRID_W
    lane = np.arange(LANES)

    def tab(d, used_lanes):
        hh = d // 4
        i = lane % d
        freq = ROPE_THETA ** (-jnp.asarray(i % hh, F32) / hh)
        by_row = jnp.asarray((i // (2 * hh)) == 0)[None, None, :]
        valid = jnp.asarray(lane < used_lanes)[None, None, :]
        ang_r = jnp.arange(n_rows, dtype=F32)[:, None] * freq[None, :]
        ang_c = jnp.arange(GRID_W, dtype=F32)[:, None] * freq[None, :]

        def expand(fn):
            t = jnp.where(by_row, fn(ang_r)[:, None, :], fn(ang_c)[None, :, :])
            return jnp.where(valid, t, 0.0).reshape(DEC_SEQ, LANES)

        return expand(jnp.cos), expand(jnp.sin)

    c64, s64 = tab(HEAD_DIM, LANES)
    c32, s32 = tab(MLA_ROPE, 2 * MLA_ROPE)
    return c64, s64, c32, s32


def _na_bias_tables(rpb_l):
    cidx = np.arange(GRID_W)
    col_start = np.clip(cidx - NA_KW // 2, 0, GRID_W - NA_KW)
    col_ok = (cidx[None, :] >= col_start[:, None]) & (cidx[None, :] < col_start[:, None] + NA_KW)
    off_c = np.clip(cidx[None, :] - cidx[:, None], -(NA_KW - 1), NA_KW - 1) + NA_KW - 1
    pick_c = np.zeros((2 * NA_KW - 1, GRID_W, GRID_W), np.float32)
    pick_c[off_c.T, cidx[:, None], cidx[None, :]] = 1.0
    toep = jnp.einsum("hdc,ckq->hdkq", rpb_l * LOG2E, pick_c, precision=lax.Precision.HIGHEST)
    tt = jnp.where(col_ok.T[None, None], toep, MASK_VALUE)
    n_dr = 2 * NA_KH - 1
    tt = jnp.concatenate([tt, jnp.full((NA_HEADS, 1, GRID_W, GRID_W), MASK_VALUE, F32)], axis=1)
    n_rows = DEC_SEQ // GRID_W
    pick_r = np.zeros((3, NA_ROWS_PER_TILE, NA_KEY_ROWS, n_dr + 1), np.float32)
    for kind, r0 in enumerate((0, NA_ROWS_PER_TILE, n_rows - NA_ROWS_PER_TILE)):
        ws = int(np.clip(r0 - NA_KH // 2, 0, n_rows - NA_KEY_ROWS))
        for a in range(NA_ROWS_PER_TILE):
            r = r0 + a
            rs = int(np.clip(r - NA_KH // 2, 0, n_rows - NA_KH))
            for jj in range(NA_KEY_ROWS):
                kr = ws + jj
                pick_r[kind, a, jj, kr - r + NA_KH - 1 if rs <= kr < rs + NA_KH else n_dr] = 1.0
    b = jnp.einsum("sajd,hdkq->shjkaq", pick_r, tt, precision=lax.Precision.HIGHEST)
    return b.reshape(3, NA_HEADS, NA_KWIN, TQ)


def kernel(x_prompt, x_sample, cache_mla_ckv, cache_mla_kpe, cache_swa_k, cache_swa_v, cache_na_k, cache_na_v, c, c_ctx, w_mod, b_mod, norm1, norm2, w_in, g_qa, w_uq, g_kva, w_ukv, swa_sink, na_rpb, w_out, w_router_grp, w_router_exp, w_gate, w_up, w_down, norm_final):
    cpad = jnp.concatenate([c_ctx[None], c, jnp.zeros((8 - 1 - DEC_BATCH, D_MODEL), F32)], axis=0)
    mod = _modulation(cpad, w_mod, b_mod).reshape(DEPTH, 8, N_MOD, D_MODEL)
    tabs = _rope_tables()
    nf = norm_final[None]
    xp = x_prompt.reshape(N_CTX_TOK, D_MODEL)
    xs = x_sample.reshape(N_LAT_TOK, D_MODEL)
    caches = [[] for _ in range(6)]
    moe_w = (w_gate.astype(BF16), w_up.astype(BF16), w_down.astype(BF16))
    for l in range(DEPTH):
        lw = _layer_weights(l, norm1, norm2, w_in, g_qa, w_uq, g_kva, w_ukv, w_out, w_router_grp, w_router_exp)
        final = l == DEPTH - 1
        outs = _projections(xp, mod[l], lw, None, rope=False)
        om, osw, ona = _context_attention(swa_sink[l], *outs[:9])
        for dst, a in zip(caches, outs[9:]):
            dst.append(a)
        x1, h2, gates = _tail(xp, om, osw, ona, mod[l], lw, latent=False)
        xp = _moe(h2, gates, x1, mod[l], l, moe_w, nf, latent=False, final=final)
        qm, km, vm, qs, ks, vs, qn, kn, vn = _projections(xs, mod[l], lw, tabs, rope=True)
        kpe_c = cache_mla_kpe[:, l].reshape(DEC_BATCH * PAST_LEN, MLA_ROPE)
        kpe_dup = jnp.concatenate([kpe_c, kpe_c, jnp.zeros((DEC_BATCH * PAST_LEN, 64), F32)], axis=1)
        kmc, vmc = _expand_cached_mla(cache_mla_ckv[:, l].reshape(DEC_BATCH * PAST_LEN, MLA_KV_LORA), kpe_dup,
                                      lw["wuk"], lw["wuvt"])
        om = _latent_mla(qm, km, vm, kmc, vmc)
        flat = lambda a: a[:, l].reshape(DEC_BATCH * PAST_LEN, -1)
        osw = _latent_swa(swa_sink[l], qs, ks, vs, flat(cache_swa_k), flat(cache_swa_v))
        ona = _latent_na(qn, kn, vn, flat(cache_na_k), flat(cache_na_v), _na_bias_tables(na_rpb[l]))
        x1, h2, gates = _tail(xs, om, osw, ona, mod[l], lw, latent=True)
        xs = _moe(h2, gates, x1, mod[l], l, moe_w, nf, latent=True, final=final)
    stack = lambda parts, tail: jnp.stack([p.reshape((BATCH, SEQ) + tail) for p in parts], axis=1)
    return (xp.reshape(BATCH, SEQ, D_MODEL), xs.reshape(DEC_BATCH, DEC_SEQ, D_MODEL),
            stack(caches[0], (MLA_KV_LORA,)), stack(caches[1], (MLA_ROPE,)),
            stack(caches[2], (SWA_KV_HEADS, HEAD_DIM)), stack(caches[3], (SWA_KV_HEADS, HEAD_DIM)),
            stack(caches[4], (NA_HEADS, HEAD_DIM)), stack(caches[5], (NA_HEADS, HEAD_DIM)))
```

```python
import functools

import jax
import jax.numpy as jnp
import numpy as np
from jax import lax
from jax.experimental import pallas as pl
from jax.experimental.pallas import tpu as pltpu

D_MODEL = 1024
BATCH = 32
SEQ = 256
DEPTH = 2
DEC_BATCH = 2
DEC_SEQ = 4096
PAST_LEN = 512
GRID_W = 64
HEAD_DIM = 64
MLA_HEADS = 8
MLA_Q_LORA = 256
MLA_KV_LORA = 128
MLA_NOPE = 64
MLA_ROPE = 32
MLA_V = 64
SWA_HEADS = 4
SWA_KV_HEADS = 2
SWA_WINDOW = 128
NA_HEADS = 4
NA_KH = 8
NA_KW = 16
N_GROUPS = 4
EXPERTS_PER_GROUP = 4
N_EXPERTS = 16
EXPERT_FF = 256
N_MOD = 6
ROPE_THETA = 10000.0
EPS = 1e-6
MASK_VALUE = -1e30
MLA_SCALE = (MLA_NOPE + MLA_ROPE) ** -0.5
HEAD_SCALE = HEAD_DIM ** -0.5
LOG2E = 1.4426950408889634

LANES = 128
N_PAIRS = MLA_HEADS // 2
MLA_QK_BLK = 2 * LANES
N_CTX_TOK = BATCH * SEQ
N_LAT_TOK = DEC_BATCH * DEC_SEQ

_C_CQ, _C_CKV, _C_QS, _C_KS, _C_VS, _C_QN, _C_KN, _C_VN, _C_KPE, _C_QSR, _C_KSR, _C_KPER, _C_END = (
    0, 256, 384, 640, 768, 896, 1152, 1408, 1664, 1792, 2048, 2176, 2304)

TM_PROJ = 512
TM_TAIL = 512
TM_MOE = 1024
TQ = 256
KV_CHUNK = 2048
MLA_KV_CHUNK = 1024
SCORES_AHEAD = 2
ONES_ROWS = 16
SWA_KWIN = TQ + 2 * SWA_WINDOW
NA_ROWS_PER_TILE = TQ // GRID_W
NA_KEY_ROWS = 12
NA_KWIN = NA_KEY_ROWS * GRID_W
VMEM_LIMIT = 56 * 1024 * 1024

F32 = jnp.float32
BF16 = jnp.bfloat16


def _dot(a, b):
    return jnp.dot(a, b, preferred_element_type=F32)


def _dot_nt(a, b):
    return lax.dot_general(a, b, (((1,), (1,)), ((), ())), preferred_element_type=F32)


def _rms(x, g):
    return x * lax.rsqrt(jnp.mean(x * x, axis=-1, keepdims=True) + EPS) * g


def _cparams(sem):
    return pltpu.CompilerParams(dimension_semantics=sem, vmem_limit_bytes=VMEM_LIMIT)


def _mod_kernel(c_ref, w_ref, b_ref, o_ref):
    c = c_ref[...]
    s = c * (1.0 / (1.0 + jnp.exp(-c)))
    o_ref[0] = jnp.dot(s, w_ref[0], preferred_element_type=F32, precision=lax.Precision.HIGHEST) + b_ref[0]


def _modulation(cpad, w_mod, b_mod):
    nt = 1024
    return pl.pallas_call(
        _mod_kernel,
        grid=(DEPTH, N_MOD * D_MODEL // nt),
        in_specs=[
            pl.BlockSpec((8, D_MODEL), lambda l, n: (0, 0)),
            pl.BlockSpec((1, D_MODEL, nt), lambda l, n: (l, 0, n)),
            pl.BlockSpec((1, 1, nt), lambda l, n: (l, 0, n)),
        ],
        out_specs=pl.BlockSpec((1, 8, nt), lambda l, n: (l, 0, n)),
        out_shape=jax.ShapeDtypeStruct((DEPTH, 8, N_MOD * D_MODEL), F32),
        compiler_params=_cparams(("arbitrary", "arbitrary")),
        name="modulation",
    )(cpad, w_mod, b_mod.reshape(DEPTH, 1, N_MOD * D_MODEL))


def _proj_kernel(rope, *refs):
    if rope:
        (x_ref, mod_ref, n1_ref, win_ref, gqa_ref, wuq_ref, gkva_ref, wuk_ref, wuvt_ref,
         c64_ref, s64_ref, c32_ref, s32_ref,
         qm_ref, km_ref, vmt_ref, qs_ref, ks_ref, vs_ref, qn_ref, kn_ref, vn_ref) = refs
    else:
        (x_ref, mod_ref, n1_ref, win_ref, gqa_ref, wuq_ref, gkva_ref, wuk_ref, wuvt_ref,
         qm_ref, km_ref, vmt_ref, qs_ref, ks_ref, vs_ref, qn_ref, kn_ref, vn_ref,
         ckv_o, kpe_o, ks_o, vs_o, kn_o, vn_o) = refs
    m = mod_ref[0]
    h = _rms(x_ref[...], n1_ref[...]) * (1.0 + m[1:2]) + m[0:1]
    p = _dot(h.astype(BF16), win_ref[...])
    qm = _dot(_rms(p[:, _C_CQ:_C_CKV], gqa_ref[...]).astype(BF16), wuq_ref[...])
    ckv = _rms(p[:, _C_CKV:_C_QS], gkva_ref[...])
    ckv_b = ckv.astype(BF16)
    kn_mla = _dot(ckv_b, wuk_ref[...])
    vmt_ref[...] = _dot_nt(wuvt_ref[...], ckv_b).astype(BF16)
    qs = p[:, _C_QS:_C_KS]
    ks = p[:, _C_KS:_C_VS]
    kpe = p[:, _C_KPE:_C_QSR]
    if rope:
        c64, s64, c32, s32 = c64_ref[...], s64_ref[...], c32_ref[...], s32_ref[...]
        qsr = p[:, _C_QSR:_C_KSR]
        qs = jnp.concatenate(
            [qs[:, j * LANES:(j + 1) * LANES] * c64 + qsr[:, j * LANES:(j + 1) * LANES] * s64 for j in range(2)], axis=1)
        ks = ks * c64 + p[:, _C_KSR:_C_KPER] * s64
        kpe = kpe * c32 + p[:, _C_KPER:_C_END] * s32
    for i in range(N_PAIRS):
        lo = i * MLA_QK_BLK
        qrope = qm[:, lo + LANES:lo + MLA_QK_BLK]
        if rope:
            r0 = N_PAIRS * MLA_QK_BLK + i * LANES
            qrope = qrope * c32 + qm[:, r0:r0 + LANES] * s32
        qm_ref[:, lo:lo + LANES] = qm[:, lo:lo + LANES].astype(BF16)
        qm_ref[:, lo + LANES:lo + MLA_QK_BLK] = qrope.astype(BF16)
        km_ref[:, lo:lo + LANES] = kn_mla[:, i * LANES:(i + 1) * LANES].astype(BF16)
        km_ref[:, lo + LANES:lo + MLA_QK_BLK] = kpe.astype(BF16)
    qs_ref[...] = qs.astype(BF16)
    ks_ref[...] = ks.astype(BF16)
    vs_ref[...] = p[:, _C_VS:_C_QN].astype(BF16)
    qn_ref[...] = p[:, _C_QN:_C_KN].astype(BF16)
    kn_ref[...] = p[:, _C_KN:_C_VN].astype(BF16)
    vn_ref[...] = p[:, _C_VN:_C_KPE].astype(BF16)
    if not rope:
        ckv_o[...] = ckv
        kpe_o[...] = kpe[:, :MLA_ROPE]
        ks_o[...] = ks
        vs_o[...] = p[:, _C_VS:_C_QN]
        kn_o[...] = p[:, _C_KN:_C_VN]
        vn_o[...] = p[:, _C_VN:_C_KPE]


def _projections(x, mod_l, lw, tabs, rope):
    n_tok = x.shape[0]
    tm = TM_PROJ
    tiles_per_seq = DEC_SEQ // tm
    win, wuq = lw["win"], lw["wuq"]
    if rope:
        mod_idx = lambda i: (1 + i // tiles_per_seq, 0, 0)
        win_cols, wuq_cols = _C_END, wuq.shape[1]
    else:
        mod_idx = lambda i: (0, 0, 0)
        win_cols, wuq_cols = _C_QSR, N_PAIRS * MLA_QK_BLK
    row = lambda w: pl.BlockSpec((tm, w), lambda i: (i, 0))
    whole = lambda a: pl.BlockSpec(a.shape, lambda i: (0,) * a.ndim)
    lead = lambda a, n: pl.BlockSpec((a.shape[0], n), lambda i: (0, 0))
    in_specs = [row(D_MODEL), pl.BlockSpec((1, N_MOD, D_MODEL), mod_idx), whole(lw["norm1"]), lead(win, win_cols),
                whole(lw["g_qa"]), lead(wuq, wuq_cols), whole(lw["g_kva"]), whole(lw["wuk"]), whole(lw["wuvt"])]
    args = [x, mod_l, lw["norm1"], win, lw["g_qa"], wuq, lw["g_kva"], lw["wuk"], lw["wuvt"]]
    widths = [N_PAIRS * MLA_QK_BLK, N_PAIRS * MLA_QK_BLK, None, 256, 128, 128, 256, 256, 256]
    vmt_spec = pl.BlockSpec((N_PAIRS * LANES, tm), lambda i: (0, i))
    out_specs = [vmt_spec if w is None else row(w) for w in widths]
    out_shape = [jax.ShapeDtypeStruct((N_PAIRS * LANES, n_tok) if w is None else (n_tok, w), BF16) for w in widths]
    if rope:
        tab_spec = pl.BlockSpec((tm, LANES), lambda i: (i % tiles_per_seq, 0))
        in_specs += [tab_spec] * 4
        args += list(tabs)
    else:
        cache_w = [MLA_KV_LORA, MLA_ROPE, 128, 128, 256, 256]
        out_specs += [row(w) for w in cache_w]
        out_shape += [jax.ShapeDtypeStruct((n_tok, w), F32) for w in cache_w]
    return pl.pallas_call(
        functools.partial(_proj_kernel, rope),
        grid=(n_tok // tm,),
        in_specs=in_specs,
        out_specs=out_specs,
        out_shape=out_shape,
        compiler_params=_cparams(("arbitrary",)),
        name="proj_lat" if rope else "proj_ctx",
    )(*args)


def _ctxkv_kernel(ckv_ref, kpe_ref, wuk_ref, wuvt_ref, km_ref, vmt_ref):
    ckv_b = ckv_ref[...].astype(BF16)
    kn_mla = _dot(ckv_b, wuk_ref[...])
    kpe = kpe_ref[...].astype(BF16)
    for i in range(N_PAIRS):
        lo = i * MLA_QK_BLK
        km_ref[:, lo:lo + LANES] = kn_mla[:, i * LANES:(i + 1) * LANES].astype(BF16)
        km_ref[:, lo + LANES:lo + MLA_QK_BLK] = kpe
    vmt_ref[...] = _dot_nt(wuvt_ref[...], ckv_b).astype(BF16)


def _expand_cached_mla(ckv_c, kpe_dup, wuk, wuvt):
    n = ckv_c.shape[0]
    whole = lambda a: pl.BlockSpec(a.shape, lambda i: (0,) * a.ndim)
    return pl.pallas_call(
        _ctxkv_kernel,
        grid=(1,),
        in_specs=[whole(ckv_c), whole(kpe_dup), whole(wuk), whole(wuvt)],
        out_specs=[pl.BlockSpec((n, N_PAIRS * MLA_QK_BLK), lambda i: (0, 0)),
                   pl.BlockSpec((N_PAIRS * LANES, n), lambda i: (0, 0))],
        out_shape=[jax.ShapeDtypeStruct((n, N_PAIRS * MLA_QK_BLK), BF16),
                   jax.ShapeDtypeStruct((N_PAIRS * LANES, n), BF16)],
        compiler_params=_cparams(("arbitrary",)),
        name="expand_cached_mla",
    )(ckv_c, kpe_dup, wuk, wuvt)


def _feature_major(v):
    return v.astype(F32).T.astype(BF16)


def _pair_masks(width):
    lane = lax.broadcasted_iota(jnp.int32, (1, width), 1)
    if width == LANES:
        return [lane < HEAD_DIM, lane >= HEAD_DIM]
    m0 = (lane < MLA_NOPE) | ((lane >= LANES) & (lane < LANES + MLA_ROPE))
    m1 = ((lane >= MLA_NOPE) & (lane < LANES)) | ((lane >= LANES + MLA_ROPE) & (lane < LANES + 2 * MLA_ROPE))
    return [m0, m1]


def _attention(jobs):
    steps, qa = [], []
    for j, (q, blocks, _) in enumerate(jobs):
        masks = _pair_masks(q.shape[1])
        qa.append([jnp.where(masks[a], q, jnp.zeros_like(q)) for a in range(2)])
        chunks = []
        for blk in blocks:
            k, vt, post = blk[:3]
            cuts = blk[3] if len(blk) > 3 else list(range(0, k.shape[0], KV_CHUNK)) + [k.shape[0]]
            chunks += [(k, vt, post, lo, hi) for lo, hi in zip(cuts[:-1], cuts[1:])]
        steps += [(j, c, ci == len(chunks) - 1) for ci, c in enumerate(chunks)]

    def scores(step):
        j, (k, _, post, lo, hi), _ = step
        kc = k[lo:hi, :]
        s = [_dot_nt(kc, qa[j][a]) for a in range(2)]
        return s if post is None else [post(a, s[a], lo, hi) for a in range(2)]

    outs = [None] * len(jobs)
    m, acc = [None, None], [None, None]
    pending = [scores(st) for st in steps[:SCORES_AHEAD]]
    for n, (j, (_, vt, _, lo, hi), last) in enumerate(steps):
        if n + SCORES_AHEAD < len(steps):
            pending.append(scores(steps[n + SCORES_AHEAD]))
        s_cur = pending.pop(0)
        ones = jnp.ones((ONES_ROWS, hi - lo), BF16)
        for a in range(2):
            vta = jnp.concatenate([vt[a * HEAD_DIM:(a + 1) * HEAD_DIM, lo:hi], ones], axis=0)
            s = s_cur[a]
            mc = jnp.max(s, axis=0, keepdims=True)
            mn = mc if m[a] is None else jnp.maximum(m[a], mc)
            pv = _dot(vta, jnp.exp2(s - mn).astype(BF16))
            acc[a] = pv if m[a] is None else jnp.exp2(m[a] - mn) * acc[a] + pv
            m[a] = mn
        if last:
            sinks, heads = jobs[j][2], []
            for a in range(2):
                l = acc[a][HEAD_DIM:HEAD_DIM + 1, :]
                if sinks[a] is not None:
                    mf = jnp.maximum(m[a], sinks[a])
                    scale = jnp.exp2(m[a] - mf)
                    l = scale * l + jnp.exp2(sinks[a] - mf)
                    heads.append(acc[a][:HEAD_DIM, :] * (scale / l))
                else:
                    heads.append(acc[a][:HEAD_DIM, :] / l)
            outs[j] = jnp.concatenate(heads, axis=0).T
            m, acc = [None, None], [None, None]
    return outs


_NO_SINKS = (None, None)


def _ctx_attn_kernel(sink_ref, qm_ref, km_ref, vmt_ref, qs_ref, ks_ref, vs_ref, qn_ref, kn_ref, vn_ref,
                     om_ref, os_ref, on_ref):
    jobs = []
    for i in range(N_PAIRS):
        q = qm_ref[:, i * MLA_QK_BLK:(i + 1) * MLA_QK_BLK]
        k = km_ref[:, i * MLA_QK_BLK:(i + 1) * MLA_QK_BLK]
        vt = vmt_ref[i * LANES:(i + 1) * LANES, :]
        jobs.append((q, [(k, vt, None)], _NO_SINKS))
    ks, vst = ks_ref[...], _feature_major(vs_ref[...])
    for j in range(2):
        sinks = (sink_ref[j] * LOG2E, sink_ref[j + 2] * LOG2E)
        jobs.append((qs_ref[:, j * LANES:(j + 1) * LANES], [(ks, vst, None)], sinks))
    for j in range(2):
        sl = slice(j * LANES, (j + 1) * LANES)
        jobs.append((qn_ref[:, sl], [(kn_ref[:, sl], _feature_major(vn_ref[:, sl]), None)], _NO_SINKS))
    outs = _attention(jobs)
    for i in range(N_PAIRS):
        om_ref[:, i * LANES:(i + 1) * LANES] = outs[i].astype(BF16)
    for j in range(2):
        os_ref[:, j * LANES:(j + 1) * LANES] = outs[N_PAIRS + j].astype(BF16)
        on_ref[:, j * LANES:(j + 1) * LANES] = outs[N_PAIRS + 2 + j].astype(BF16)


def _context_attention(sink_l, qm, km, vmt, qs, ks, vs, qn, kn, vn):
    row = lambda a: (pl.BlockSpec((a.shape[0], SEQ), lambda b: (0, b)) if a is vmt
                     else pl.BlockSpec((SEQ, a.shape[1]), lambda b: (b, 0)))
    ins = [qm, km, vmt, qs, ks, vs, qn, kn, vn]
    widths = [N_PAIRS * LANES, 256, 256]
    return pl.pallas_call(
        _ctx_attn_kernel,
        grid=(BATCH,),
        in_specs=[pl.BlockSpec(memory_space=pltpu.SMEM)] + [row(a) for a in ins],
        out_specs=[pl.BlockSpec((SEQ, w), lambda b: (b, 0)) for w in widths],
        out_shape=[jax.ShapeDtypeStruct((N_CTX_TOK, w), BF16) for w in widths],
        compiler_params=_cparams(("arbitrary",)),
        name="attn_ctx",
    )(sink_l, *ins)


def _lat_mla_kernel(q_ref, kl_ref, vlt_ref, kc_ref, vct_ref, o_ref):
    blocks = [(kc_ref, vct_ref, None), (kl_ref, vlt_ref, None, list(range(0, DEC_SEQ + 1, MLA_KV_CHUNK)))]
    o_ref[...] = _attention([(q_ref[...], blocks, _NO_SINKS)])[0].astype(BF16)


def _latent_mla(qm, km, vmt, kmc, vmct):
    nq = DEC_SEQ // TQ
    return pl.pallas_call(
        _lat_mla_kernel,
        grid=(DEC_BATCH, N_PAIRS, nq),
        in_specs=[
            pl.BlockSpec((TQ, MLA_QK_BLK), lambda b, i, t: (b * nq + t, i)),
            pl.BlockSpec((DEC_SEQ, MLA_QK_BLK), lambda b, i, t: (b, i)),
            pl.BlockSpec((LANES, DEC_SEQ), lambda b, i, t: (i, b)),
            pl.BlockSpec((PAST_LEN, MLA_QK_BLK), lambda b, i, t: (b, i)),
            pl.BlockSpec((LANES, PAST_LEN), lambda b, i, t: (i, b)),
        ],
        out_specs=pl.BlockSpec((TQ, LANES), lambda b, i, t: (b * nq + t, i)),
        out_shape=jax.ShapeDtypeStruct((N_LAT_TOK, N_PAIRS * LANES), BF16),
        compiler_params=_cparams(("arbitrary", "arbitrary", "arbitrary")),
        name="attn_lat_mla",
    )(qm, km, vmt, kmc, vmct)


def _lat_swa_kernel(sink_ref, q_ref, k_ref, v_ref, kc_ref, vc_ref, o_ref):
    t = pl.program_id(1)
    q0 = t * TQ
    ws = pl.multiple_of(jnp.clip(q0 - SWA_WINDOW, 0, DEC_SEQ - SWA_KWIN), SWA_WINDOW)
    kw = k_ref[pl.ds(ws, SWA_KWIN), :]
    vw = _feature_major(v_ref[pl.ds(ws, SWA_KWIN), :])
    kc = kc_ref[...].astype(BF16)
    vc = _feature_major(vc_ref[...])
    kpos = ws + lax.broadcasted_iota(jnp.int32, (SWA_KWIN, TQ), 0)
    qpos = q0 + lax.broadcasted_iota(jnp.int32, (SWA_KWIN, TQ), 1)
    in_band = jnp.abs(kpos - qpos) <= SWA_WINDOW

    def band(a, s, lo, hi):
        return jnp.where(in_band[lo:hi, :], s, MASK_VALUE)

    jobs = []
    for j in range(2):
        sinks = (sink_ref[j] * LOG2E, sink_ref[j + 2] * LOG2E)
        jobs.append((q_ref[:, j * LANES:(j + 1) * LANES], [(kw, vw, band), (kc, vc, None)], sinks))
    for j, o in enumerate(_attention(jobs)):
        o_ref[:, j * LANES:(j + 1) * LANES] = o.astype(BF16)


def _latent_swa(sink_l, qs, ks, vs, ksc, vsc):
    nq = DEC_SEQ // TQ
    return pl.pallas_call(
        _lat_swa_kernel,
        grid=(DEC_BATCH, nq),
        in_specs=[
            pl.BlockSpec(memory_space=pltpu.SMEM),
            pl.BlockSpec((TQ, 256), lambda b, t: (b * nq + t, 0)),
            pl.BlockSpec((DEC_SEQ, LANES), lambda b, t: (b, 0)),
            pl.BlockSpec((DEC_SEQ, LANES), lambda b, t: (b, 0)),
            pl.BlockSpec((PAST_LEN, LANES), lambda b, t: (b, 0)),
            pl.BlockSpec((PAST_LEN, LANES), lambda b, t: (b, 0)),
        ],
        out_specs=pl.BlockSpec((TQ, 256), lambda b, t: (b * nq + t, 0)),
        out_shape=jax.ShapeDtypeStruct((N_LAT_TOK, 256), BF16),
        compiler_params=_cparams(("arbitrary", "arbitrary")),
        name="attn_lat_swa",
    )(sink_l, qs, ks, vs, ksc, vsc)


def _lat_na_kernel(q_ref, k_ref, v_ref, kc_ref, vc_ref, bias_ref, o_ref):
    t = pl.program_id(1)
    r0 = t * NA_ROWS_PER_TILE
    ws = pl.multiple_of(jnp.clip(r0 - NA_KH // 2, 0, DEC_SEQ // GRID_W - NA_KEY_ROWS) * GRID_W, GRID_W)
    jobs = []
    for j in range(2):
        sl = slice(j * LANES, (j + 1) * LANES)
        kw = k_ref[pl.ds(ws, NA_KWIN), sl]
        vw = _feature_major(v_ref[pl.ds(ws, NA_KWIN), sl])
        kc = kc_ref[:, sl].astype(BF16)
        vc = _feature_major(vc_ref[:, sl])

        def add_bias(a, s, lo, hi, j=j):
            return s + bias_ref[0, 2 * j + a, lo:hi, :]

        jobs.append((q_ref[:, sl], [(kw, vw, add_bias), (kc, vc, None)], _NO_SINKS))
    for j, o in enumerate(_attention(jobs)):
        o_ref[:, j * LANES:(j + 1) * LANES] = o.astype(BF16)


def _latent_na(qn, kn, vn, knc, vnc, bias):
    nq = DEC_SEQ // TQ
    kind = lambda t: jnp.where(t == 0, 0, jnp.where(t == nq - 1, 2, 1))
    return pl.pallas_call(
        _lat_na_kernel,
        grid=(DEC_BATCH, nq),
        in_specs=[
            pl.BlockSpec((TQ, 256), lambda b, t: (b * nq + t, 0)),
            pl.BlockSpec((DEC_SEQ, 256), lambda b, t: (b, 0)),
            pl.BlockSpec((DEC_SEQ, 256), lambda b, t: (b, 0)),
            pl.BlockSpec((PAST_LEN, 256), lambda b, t: (b, 0)),
            pl.BlockSpec((PAST_LEN, 256), lambda b, t: (b, 0)),
            pl.BlockSpec((1, NA_HEADS, NA_KWIN, TQ), lambda b, t: (kind(t), 0, 0, 0)),
        ],
        out_specs=pl.BlockSpec((TQ, 256), lambda b, t: (b * nq + t, 0)),
        out_shape=jax.ShapeDtypeStruct((N_LAT_TOK, 256), BF16),
        compiler_params=_cparams(("arbitrary", "arbitrary")),
        name="attn_lat_na",
    )(qn, kn, vn, knc, vnc, bias)


_GRP_LANE0 = 0
_EXP_LANE0 = N_GROUPS


def _lane_first_max(x, valid, lane):
    xm = jnp.where(valid, x, -jnp.inf)
    mx = jnp.max(xm, axis=-1, keepdims=True)
    idx = jnp.min(jnp.where(valid & (xm == mx), lane, LANES), axis=-1, keepdims=True)
    return mx, idx


def _tail_kernel(x_ref, om_ref, os_ref, on_ref, mod_ref, n2_ref, wo_ref, wr_ref, x1_ref, h2_ref, gates_ref):
    m = mod_ref[0]
    wo = wo_ref
    attn = (_dot(om_ref[...], wo[0:512, :]) + _dot(os_ref[...], wo[512:768, :]) + _dot(on_ref[...], wo[768:1024, :]))
    x1 = x_ref[...] + m[2:3] * attn
    x1_ref[...] = x1
    h2 = _rms(x1, n2_ref[...]) * (1.0 + m[4:5]) + m[3:4]
    h2_ref[...] = h2.astype(BF16)
    hi = h2.astype(BF16)
    lo = (h2 - hi.astype(F32)).astype(BF16)
    a = _dot(hi, wr_ref[...])
    logits = a[:, :LANES] + a[:, LANES:] + _dot(lo, wr_ref[:, :LANES])
    lane = lax.broadcasted_iota(jnp.int32, logits.shape, 1)
    is_grp = lane < N_GROUPS
    gmax, gidx = _lane_first_max(logits, is_grp, lane)
    gden = jnp.sum(jnp.where(is_grp, jnp.exp(logits - gmax), 0.0), axis=-1, keepdims=True)
    grp_gate = 1.0 / gden
    in_grp = (lane >= _EXP_LANE0) & (lane < _EXP_LANE0 + N_EXPERTS) & ((lane // EXPERTS_PER_GROUP - 1) == gidx)
    v1, i1 = _lane_first_max(logits, in_grp, lane)
    v2, i2 = _lane_first_max(logits, in_grp & (lane != i1), lane)
    e2 = jnp.exp(v2 - v1)
    w1 = grp_gate / (1.0 + e2)
    w2 = grp_gate * e2 / (1.0 + e2)
    gates_ref[...] = jnp.where(lane == i1, w1, 0.0) + jnp.where(lane == i2, w2, 0.0)


def _tail(x, om, osw, ona, mod_l, lw, latent):
    n_tok = x.shape[0]
    tm = TM_TAIL
    tiles_per_seq = DEC_SEQ // tm
    mod_idx = (lambda i: (1 + i // tiles_per_seq, 0, 0)) if latent else (lambda i: (0, 0, 0))
    row = lambda w: pl.BlockSpec((tm, w), lambda i: (i, 0))
    whole = lambda a: pl.BlockSpec(a.shape, lambda i: (0,) * a.ndim)
    return pl.pallas_call(
        _tail_kernel,
        grid=(n_tok // tm,),
        in_specs=[row(D_MODEL), row(512), row(256), row(256), pl.BlockSpec((1, N_MOD, D_MODEL), mod_idx),
                  whole(lw["norm2"]), whole(lw["wout"]), whole(lw["wr"])],
        out_specs=[row(D_MODEL), row(D_MODEL), row(LANES)],
        out_shape=[jax.ShapeDtypeStruct((n_tok, D_MODEL), F32), jax.ShapeDtypeStruct((n_tok, D_MODEL), BF16),
                   jax.ShapeDtypeStruct((n_tok, LANES), F32)],
        compiler_params=_cparams(("arbitrary",)),
        name="tail_lat" if latent else "tail_ctx",
    )(x, om, osw, ona, mod_l, lw["norm2"], lw["wout"], lw["wr"])


def _moe_kernel(final, h2_ref, gates_ref, x1_ref, mod_ref, nf_ref, wg_ref, wu_ref, wd_ref, o_ref, acc_ref):
    g = pl.program_id(1)
    h2 = h2_ref[...]
    gates = gates_ref[...]
    lane = lax.broadcasted_iota(jnp.int32, gates.shape, 1)
    contrib = None
    for e in range(EXPERTS_PER_GROUP):
        ge = jnp.sum(jnp.where(lane == _EXP_LANE0 + g * EXPERTS_PER_GROUP + e, gates, 0.0), axis=-1, keepdims=True)
        hg = _dot(h2, wg_ref[0, e])
        hu = _dot(h2, wu_ref[0, e])
        act = hg * (1.0 / (1.0 + jnp.exp(-hg))) * hu * ge
        c = _dot(act.astype(BF16), wd_ref[0, e])
        contrib = c if contrib is None else contrib + c

    @pl.when(g == 0)
    def _():
        acc_ref[...] = contrib

    @pl.when(g > 0)
    def _():
        acc_ref[...] += contrib

    @pl.when(g == N_GROUPS - 1)
    def _():
        y = x1_ref[...] + mod_ref[0][5:6] * acc_ref[...]
        if final:
            y = _rms(y, nf_ref[...])
        o_ref[...] = y


def _moe(h2, gates, x1, mod_l, l, moe_w, norm_final, latent, final):
    n_tok = h2.shape[0]
    tm = TM_MOE
    tiles_per_seq = DEC_SEQ // tm
    mod_idx = (lambda i, g: (1 + i // tiles_per_seq, 0, 0)) if latent else (lambda i, g: (0, 0, 0))
    epg = EXPERTS_PER_GROUP
    return pl.pallas_call(
        functools.partial(_moe_kernel, final),
        grid=(n_tok // tm, N_GROUPS),
        in_specs=[
            pl.BlockSpec((tm, D_MODEL), lambda i, g: (i, 0)),
            pl.BlockSpec((tm, LANES), lambda i, g: (i, 0)),
            pl.BlockSpec((tm, D_MODEL), lambda i, g: (i, 0)),
            pl.BlockSpec((1, N_MOD, D_MODEL), mod_idx),
            pl.BlockSpec((1, D_MODEL), lambda i, g: (0, 0)),
            pl.BlockSpec((1, epg, D_MODEL, EXPERT_FF), lambda i, g: (l, g, 0, 0)),
            pl.BlockSpec((1, epg, D_MODEL, EXPERT_FF), lambda i, g: (l, g, 0, 0)),
            pl.BlockSpec((1, epg, EXPERT_FF, D_MODEL), lambda i, g: (l, g, 0, 0)),
        ],
        out_specs=pl.BlockSpec((tm, D_MODEL), lambda i, g: (i, 0)),
        out_shape=jax.ShapeDtypeStruct((n_tok, D_MODEL), F32),
        scratch_shapes=[pltpu.VMEM((tm, D_MODEL), F32)],
        compiler_params=_cparams(("arbitrary", "arbitrary")),
        name=("moe_lat" if latent else "moe_ctx") + ("_final" if final else ""),
    )(h2, gates, x1, mod_l, norm_final, *moe_w)


def _rot_cols(w, d):
    k, n = w.shape
    hh = d // 4
    w4 = w.reshape(k, n // (2 * hh), 2, hh)
    return jnp.stack([-w4[:, :, 1], w4[:, :, 0]], axis=2).reshape(k, n)


def _heads(w, d, order, axis):
    parts = [lax.slice_in_dim(w, h * d, (h + 1) * d, axis=axis) for h in order]
    return jnp.concatenate(parts, axis=axis)


def _layer_weights(l, norm1, norm2, w_in, g_qa, w_uq, g_kva, w_ukv, w_out, w_router_grp, w_router_exp):
    wi = w_in[l]
    z64 = jnp.zeros((D_MODEL, 64), F32)
    cq, ckv, kpe = wi[:, 0:256], wi[:, 256:384], wi[:, 384:416]
    qs = _heads(wi[:, 416:672], HEAD_DIM, (0, 2, 1, 3), 1) * (HEAD_SCALE * LOG2E)
    ks, vs = wi[:, 672:800], wi[:, 800:928]
    qn, kn, vn = wi[:, 928:1184] * (HEAD_SCALE * LOG2E), wi[:, 1184:1440], wi[:, 1440:1696]
    kped = jnp.concatenate([kpe, kpe, z64], axis=1)
    kper = _rot_cols(kpe, MLA_ROPE)
    ctx_cols = [cq, ckv, qs, ks, vs, qn, kn, vn, kped]
    lat_cols = ctx_cols + [_rot_cols(qs, HEAD_DIM), _rot_cols(ks, HEAD_DIM), jnp.concatenate([kper, kper, z64], axis=1)]
    wq = w_uq[l].reshape(MLA_Q_LORA, MLA_HEADS, MLA_NOPE + MLA_ROPE) * (MLA_SCALE * LOG2E)
    nope, ropew = wq[:, :, :MLA_NOPE], wq[:, :, MLA_NOPE:]
    z64q = jnp.zeros((MLA_Q_LORA, 64), F32)
    blocks, rots = [], []
    for i in range(N_PAIRS):
        blocks += [nope[:, 2 * i], nope[:, 2 * i + 1], ropew[:, 2 * i], ropew[:, 2 * i + 1], z64q]
        rots += [_rot_cols(ropew[:, 2 * i], MLA_ROPE), _rot_cols(ropew[:, 2 * i + 1], MLA_ROPE), z64q]
    wkv = w_ukv[l].reshape(MLA_KV_LORA, MLA_HEADS, MLA_NOPE + MLA_V)
    wuk = wkv[:, :, :MLA_NOPE].reshape(MLA_KV_LORA, -1)
    wuvt = wkv[:, :, MLA_NOPE:].reshape(MLA_KV_LORA, -1).T
    wo = w_out[l]
    wout = jnp.concatenate([wo[:512], _heads(wo[512:768], HEAD_DIM, (0, 2, 1, 3), 0), wo[768:]], axis=0)
    wr = jnp.concatenate([w_router_grp[l], w_router_exp[l], jnp.zeros((D_MODEL, LANES - N_GROUPS - N_EXPERTS), F32)],
                         axis=1)
    wr_hi = wr.astype(BF16)
    wr_lo = (wr - wr_hi.astype(F32)).astype(BF16)
    return {
        "norm1": norm1[l][None], "norm2": norm2[l][None], "g_qa": g_qa[l][None], "g_kva": g_kva[l][None],
        "win": jnp.concatenate(lat_cols, axis=1).astype(BF16),
        "wuq": jnp.concatenate(blocks + rots, axis=1).astype(BF16),
        "wuk": wuk.astype(BF16),
        "wuvt": wuvt.astype(BF16),
        "wout": wout.astype(BF16),
        "wr": jnp.concatenate([wr_hi, wr_lo], axis=1),
    }


def _rope_tables():
    n_rows = DEC_SEQ // GRID_W
    lane = np.arange(LANES)

    def tab(d, used_lanes):
        hh = d // 4
        i = lane % d
        freq = ROPE_THETA ** (-jnp.asarray(i % hh, F32) / hh)
        by_row = jnp.asarray((i // (2 * hh)) == 0)[None, None, :]
        valid = jnp.asarray(lane < used_lanes)[None, None, :]
        ang_r = jnp.arange(n_rows, dtype=F32)[:, None] * freq[None, :]
        ang_c = jnp.arange(GRID_W, dtype=F32)[:, None] * freq[None, :]

        def expand(fn):
            t = jnp.where(by_row, fn(ang_r)[:, None, :], fn(ang_c)[None, :, :])
            return jnp.where(valid, t, 0.0).reshape(DEC_SEQ, LANES)

        return expand(jnp.cos), expand(jnp.sin)

    c64, s64 = tab(HEAD_DIM, LANES)
    c32, s32 = tab(MLA_ROPE, 2 * MLA_ROPE)
    return c64, s64, c32, s32


def _na_bias_tables(rpb_l):
    cidx = np.arange(GRID_W)
    col_start = np.clip(cidx - NA_KW // 2, 0, GRID_W - NA_KW)
    col_ok = (cidx[None, :] >= col_start[:, None]) & (cidx[None, :] < col_start[:, None] + NA_KW)
    off_c = np.clip(cidx[None, :] - cidx[:, None], -(NA_KW - 1), NA_KW - 1) + NA_KW - 1
    pick_c = np.zeros((2 * NA_KW - 1, GRID_W, GRID_W), np.float32)
    pick_c[off_c.T, cidx[:, None], cidx[None, :]] = 1.0
    toep = jnp.einsum("hdc,ckq->hdkq", rpb_l * LOG2E, pick_c, precision=lax.Precision.HIGHEST)
    tt = jnp.where(col_ok.T[None, None], toep, MASK_VALUE)
    n_dr = 2 * NA_KH - 1
    tt = jnp.concatenate([tt, jnp.full((NA_HEADS, 1, GRID_W, GRID_W), MASK_VALUE, F32)], axis=1)
    n_rows = DEC_SEQ // GRID_W
    pick_r = np.zeros((3, NA_ROWS_PER_TILE, NA_KEY_ROWS, n_dr + 1), np.float32)
    for kind, r0 in enumerate((0, NA_ROWS_PER_TILE, n_rows - NA_ROWS_PER_TILE)):
        ws = int(np.clip(r0 - NA_KH // 2, 0, n_rows - NA_KEY_ROWS))
        for a in range(NA_ROWS_PER_TILE):
            r = r0 + a
            rs = int(np.clip(r - NA_KH // 2, 0, n_rows - NA_KH))
            for jj in range(NA_KEY_ROWS):
                kr = ws + jj
                pick_r[kind, a, jj, kr - r + NA_KH - 1 if rs <= kr < rs + NA_KH else n_dr] = 1.0
    b = jnp.einsum("sajd,hdkq->shjkaq", pick_r, tt, precision=lax.Precision.HIGHEST)
    return b.reshape(3, NA_HEADS, NA_KWIN, TQ)


def kernel(x_prompt, x_sample, cache_mla_ckv, cache_mla_kpe, cache_swa_k, cache_swa_v, cache_na_k, cache_na_v, c, c_ctx, w_mod, b_mod, norm1, norm2, w_in, g_qa, w_uq, g_kva, w_ukv, swa_sink, na_rpb, w_out, w_router_grp, w_router_exp, w_gate, w_up, w_down, norm_final):
    cpad = jnp.concatenate([c_ctx[None], c, jnp.zeros((8 - 1 - DEC_BATCH, D_MODEL), F32)], axis=0)
    mod = _modulation(cpad, w_mod, b_mod).reshape(DEPTH, 8, N_MOD, D_MODEL)
    tabs = _rope_tables()
    nf = norm_final[None]
    xp = x_prompt.reshape(N_CTX_TOK, D_MODEL)
    xs = x_sample.reshape(N_LAT_TOK, D_MODEL)
    caches = [[] for _ in range(6)]
    moe_w = (w_gate.astype(BF16), w_up.astype(BF16), w_down.astype(BF16))
    for l in range(DEPTH):
        lw = _layer_weights(l, norm1, norm2, w_in, g_qa, w_uq, g_kva, w_ukv, w_out, w_router_grp, w_router_exp)
        final = l == DEPTH - 1
        outs = _projections(xp, mod[l], lw, None, rope=False)
        om, osw, ona = _context_attention(swa_sink[l], *outs[:9])
        for dst, a in zip(caches, outs[9:]):
            dst.append(a)
        x1, h2, gates = _tail(xp, om, osw, ona, mod[l], lw, latent=False)
        xp = _moe(h2, gates, x1, mod[l], l, moe_w, nf, latent=False, final=final)
        qm, km, vm, qs, ks, vs, qn, kn, vn = _projections(xs, mod[l], lw, tabs, rope=True)
        kpe_c = cache_mla_kpe[:, l].reshape(DEC_BATCH * PAST_LEN, MLA_ROPE)
        kpe_dup = jnp.concatenate([kpe_c, kpe_c, jnp.zeros((DEC_BATCH * PAST_LEN, 64), F32)], axis=1)
        kmc, vmc = _expand_cached_mla(cache_mla_ckv[:, l].reshape(DEC_BATCH * PAST_LEN, MLA_KV_LORA), kpe_dup,
                                      lw["wuk"], lw["wuvt"])
        om = _latent_mla(qm, km, vm, kmc, vmc)
        flat = lambda a: a[:, l].reshape(DEC_BATCH * PAST_LEN, -1)
        osw = _latent_swa(swa_sink[l], qs, ks, vs, flat(cache_swa_k), flat(cache_swa_v))
        ona = _latent_na(qn, kn, vn, flat(cache_na_k), flat(cache_na_v), _na_bias_tables(na_rpb[l]))
        x1, h2, gates = _tail(xs, om, osw, ona, mod[l], lw, latent=True)
        xs = _moe(h2, gates, x1, mod[l], l, moe_w, nf, latent=True, final=final)
    stack = lambda parts, tail: jnp.stack([p.reshape((BATCH, SEQ) + tail) for p in parts], axis=1)
    return (xp.reshape(BATCH, SEQ, D_MODEL), xs.reshape(DEC_BATCH, DEC_SEQ, D_MODEL),
            stack(caches[0], (MLA_KV_LORA,)), stack(caches[1], (MLA_ROPE,)),
            stack(caches[2], (SWA_KV_HEADS, HEAD_DIM)), stack(caches[3], (SWA_KV_HEADS, HEAD_DIM)),
            stack(caches[4], (NA_HEADS, HEAD_DIM)), stack(caches[5], (NA_HEADS, HEAD_DIM)))
```

```python
import functools

import jax
import jax.numpy as jnp
import numpy as np
from jax import lax
from jax.experimental import pallas as pl
from jax.experimental.pallas import tpu as pltpu

D_MODEL = 1024
BATCH = 32
SEQ = 256
DEPTH = 2
DEC_BATCH = 2
DEC_SEQ = 4096
PAST_LEN = 512
GRID_W = 64
HEAD_DIM = 64
MLA_HEADS = 8
MLA_Q_LORA = 256
MLA_KV_LORA = 128
MLA_NOPE = 64
MLA_ROPE = 32
MLA_V = 64
SWA_HEADS = 4
SWA_KV_HEADS = 2
SWA_WINDOW = 128
NA_HEADS = 4
NA_KH = 8
NA_KW = 16
N_GROUPS = 4
EXPERTS_PER_GROUP = 4
N_EXPERTS = 16
EXPERT_FF = 256
N_MOD = 6
ROPE_THETA = 10000.0
EPS = 1e-6
MASK_VALUE = -1e30
MLA_SCALE = (MLA_NOPE + MLA_ROPE) ** -0.5
HEAD_SCALE = HEAD_DIM ** -0.5
LOG2E = 1.4426950408889634

LANES = 128
N_PAIRS = MLA_HEADS // 2
MLA_QK_BLK = 2 * LANES
N_CTX_TOK = BATCH * SEQ
N_LAT_TOK = DEC_BATCH * DEC_SEQ

_C_CQ, _C_CKV, _C_QS, _C_KS, _C_VS, _C_QN, _C_KN, _C_VN, _C_KPE, _C_QSR, _C_KSR, _C_KPER, _C_END = (
    0, 256, 384, 640, 768, 896, 1152, 1408, 1664, 1792, 2048, 2176, 2304)

TM_PROJ = 512
TM_TAIL = 512
TAIL_SUB_ROWS = 128
TM_MOE = 1024
TQ = 256
KV_CHUNK = 2048
MLA_KV_CHUNK = 1024
SCORES_AHEAD = 2
ONES_ROWS = 16
SWA_KWIN = TQ + 2 * SWA_WINDOW
NA_ROWS_PER_TILE = TQ // GRID_W
NA_KEY_ROWS = 12
NA_KWIN = NA_KEY_ROWS * GRID_W
VMEM_LIMIT = 56 * 1024 * 1024

F32 = jnp.float32
BF16 = jnp.bfloat16


def _dot(a, b):
    return jnp.dot(a, b, preferred_element_type=F32)


def _dot_nt(a, b):
    return lax.dot_general(a, b, (((1,), (1,)), ((), ())), preferred_element_type=F32)


def _rms(x, g):
    return x * lax.rsqrt(jnp.mean(x * x, axis=-1, keepdims=True) + EPS) * g


def _cparams(sem):
    return pltpu.CompilerParams(dimension_semantics=sem, vmem_limit_bytes=VMEM_LIMIT)


def _mod_kernel(c_ref, w_ref, b_ref, o_ref):
    c = c_ref[...]
    s = c * (1.0 / (1.0 + jnp.exp(-c)))
    o_ref[0] = jnp.dot(s, w_ref[0], preferred_element_type=F32, precision=lax.Precision.HIGHEST) + b_ref[0]


def _modulation(cpad, w_mod, b_mod):
    nt = 1024
    return pl.pallas_call(
        _mod_kernel,
        grid=(DEPTH, N_MOD * D_MODEL // nt),
        in_specs=[
            pl.BlockSpec((8, D_MODEL), lambda l, n: (0, 0)),
            pl.BlockSpec((1, D_MODEL, nt), lambda l, n: (l, 0, n)),
            pl.BlockSpec((1, 1, nt), lambda l, n: (l, 0, n)),
        ],
        out_specs=pl.BlockSpec((1, 8, nt), lambda l, n: (l, 0, n)),
        out_shape=jax.ShapeDtypeStruct((DEPTH, 8, N_MOD * D_MODEL), F32),
        compiler_params=_cparams(("arbitrary", "arbitrary")),
        name="modulation",
    )(cpad, w_mod, b_mod.reshape(DEPTH, 1, N_MOD * D_MODEL))


def _proj_kernel(rope, *refs):
    if rope:
        (x_ref, mod_ref, n1_ref, win_ref, gqa_ref, wuq_ref, gkva_ref, wuk_ref, wuvt_ref,
         c64_ref, s64_ref, c32_ref, s32_ref,
         qm_ref, km_ref, vmt_ref, qs_ref, ks_ref, vs_ref, qn_ref, kn_ref, vn_ref) = refs
    else:
        (x_ref, mod_ref, n1_ref, win_ref, gqa_ref, wuq_ref, gkva_ref, wuk_ref, wuvt_ref,
         qm_ref, km_ref, vmt_ref, qs_ref, ks_ref, vs_ref, qn_ref, kn_ref, vn_ref,
         ckv_o, kpe_o, ks_o, vs_o, kn_o, vn_o) = refs
    m = mod_ref[0]
    h = _rms(x_ref[...], n1_ref[...]) * (1.0 + m[1:2]) + m[0:1]
    p = _dot(h.astype(BF16), win_ref[...])
    qm = _dot(_rms(p[:, _C_CQ:_C_CKV], gqa_ref[...]).astype(BF16), wuq_ref[...])
    ckv = _rms(p[:, _C_CKV:_C_QS], gkva_ref[...])
    ckv_b = ckv.astype(BF16)
    kn_mla = _dot(ckv_b, wuk_ref[...])
    vmt_ref[...] = _dot_nt(wuvt_ref[...], ckv_b).astype(BF16)
    qs = p[:, _C_QS:_C_KS]
    ks = p[:, _C_KS:_C_VS]
    kpe = p[:, _C_KPE:_C_QSR]
    if rope:
        c64, s64, c32, s32 = c64_ref[...], s64_ref[...], c32_ref[...], s32_ref[...]
        qsr = p[:, _C_QSR:_C_KSR]
        qs = jnp.concatenate(
            [qs[:, j * LANES:(j + 1) * LANES] * c64 + qsr[:, j * LANES:(j + 1) * LANES] * s64 for j in range(2)], axis=1)
        ks = ks * c64 + p[:, _C_KSR:_C_KPER] * s64
        kpe = kpe * c32 + p[:, _C_KPER:_C_END] * s32
    for i in range(N_PAIRS):
        lo = i * MLA_QK_BLK
        qrope = qm[:, lo + LANES:lo + MLA_QK_BLK]
        if rope:
            r0 = N_PAIRS * MLA_QK_BLK + i * LANES
            qrope = qrope * c32 + qm[:, r0:r0 + LANES] * s32
        qm_ref[:, lo:lo + LANES] = qm[:, lo:lo + LANES].astype(BF16)
        qm_ref[:, lo + LANES:lo + MLA_QK_BLK] = qrope.astype(BF16)
        km_ref[:, lo:lo + LANES] = kn_mla[:, i * LANES:(i + 1) * LANES].astype(BF16)
        km_ref[:, lo + LANES:lo + MLA_QK_BLK] = kpe.astype(BF16)
    qs_ref[...] = qs.astype(BF16)
    ks_ref[...] = ks.astype(BF16)
    vs_ref[...] = p[:, _C_VS:_C_QN].astype(BF16)
    qn_ref[...] = p[:, _C_QN:_C_KN].astype(BF16)
    kn_ref[...] = p[:, _C_KN:_C_VN].astype(BF16)
    vn_ref[...] = p[:, _C_VN:_C_KPE].astype(BF16)
    if not rope:
        ckv_o[...] = ckv
        kpe_o[...] = kpe[:, :MLA_ROPE]
        ks_o[...] = ks
        vs_o[...] = p[:, _C_VS:_C_QN]
        kn_o[...] = p[:, _C_KN:_C_VN]
        vn_o[...] = p[:, _C_VN:_C_KPE]


def _projections(x, mod_l, lw, tabs, rope):
    n_tok = x.shape[0]
    tm = TM_PROJ
    tiles_per_seq = DEC_SEQ // tm
    win, wuq = lw["win"], lw["wuq"]
    if rope:
        mod_idx = lambda i: (1 + i // tiles_per_seq, 0, 0)
        win_cols, wuq_cols = _C_END, wuq.shape[1]
    else:
        mod_idx = lambda i: (0, 0, 0)
        win_cols, wuq_cols = _C_QSR, N_PAIRS * MLA_QK_BLK
    row = lambda w: pl.BlockSpec((tm, w), lambda i: (i, 0))
    whole = lambda a: pl.BlockSpec(a.shape, lambda i: (0,) * a.ndim)
    lead = lambda a, n: pl.BlockSpec((a.shape[0], n), lambda i: (0, 0))
    in_specs = [row(D_MODEL), pl.BlockSpec((1, N_MOD, D_MODEL), mod_idx), whole(lw["norm1"]), lead(win, win_cols),
                whole(lw["g_qa"]), lead(wuq, wuq_cols), whole(lw["g_kva"]), whole(lw["wuk"]), whole(lw["wuvt"])]
    args = [x, mod_l, lw["norm1"], win, lw["g_qa"], wuq, lw["g_kva"], lw["wuk"], lw["wuvt"]]
    widths = [N_PAIRS * MLA_QK_BLK, N_PAIRS * MLA_QK_BLK, None, 256, 128, 128, 256, 256, 256]
    vmt_spec = pl.BlockSpec((N_PAIRS * LANES, tm), lambda i: (0, i))
    out_specs = [vmt_spec if w is None else row(w) for w in widths]
    out_shape = [jax.ShapeDtypeStruct((N_PAIRS * LANES, n_tok) if w is None else (n_tok, w), BF16) for w in widths]
    if rope:
        tab_spec = pl.BlockSpec((tm, LANES), lambda i: (i % tiles_per_seq, 0))
        in_specs += [tab_spec] * 4
        args += list(tabs)
    else:
        cache_w = [MLA_KV_LORA, MLA_ROPE, 128, 128, 256, 256]
        out_specs += [row(w) for w in cache_w]
        out_shape += [jax.ShapeDtypeStruct((n_tok, w), F32) for w in cache_w]
    return pl.pallas_call(
        functools.partial(_proj_kernel, rope),
        grid=(n_tok // tm,),
        in_specs=in_specs,
        out_specs=out_specs,
        out_shape=out_shape,
        compiler_params=_cparams(("arbitrary",)),
        name="proj_lat" if rope else "proj_ctx",
    )(*args)


def _ctxkv_kernel(ckv_ref, kpe_ref, wuk_ref, wuvt_ref, km_ref, vmt_ref):
    ckv_b = ckv_ref[...].astype(BF16)
    kn_mla = _dot(ckv_b, wuk_ref[...])
    kpe = kpe_ref[...].astype(BF16)
    for i in range(N_PAIRS):
        lo = i * MLA_QK_BLK
        km_ref[:, lo:lo + LANES] = kn_mla[:, i * LANES:(i + 1) * LANES].astype(BF16)
        km_ref[:, lo + LANES:lo + MLA_QK_BLK] = kpe
    vmt_ref[...] = _dot_nt(wuvt_ref[...], ckv_b).astype(BF16)


def _expand_cached_mla(ckv_c, kpe_dup, wuk, wuvt):
    n = ckv_c.shape[0]
    whole = lambda a: pl.BlockSpec(a.shape, lambda i: (0,) * a.ndim)
    return pl.pallas_call(
        _ctxkv_kernel,
        grid=(1,),
        in_specs=[whole(ckv_c), whole(kpe_dup), whole(wuk), whole(wuvt)],
        out_specs=[pl.BlockSpec((n, N_PAIRS * MLA_QK_BLK), lambda i: (0, 0)),
                   pl.BlockSpec((N_PAIRS * LANES, n), lambda i: (0, 0))],
        out_shape=[jax.ShapeDtypeStruct((n, N_PAIRS * MLA_QK_BLK), BF16),
                   jax.ShapeDtypeStruct((N_PAIRS * LANES, n), BF16)],
        compiler_params=_cparams(("arbitrary",)),
        name="expand_cached_mla",
    )(ckv_c, kpe_dup, wuk, wuvt)


def _feature_major(v):
    return v.astype(F32).T.astype(BF16)


def _pair_masks(width):
    lane = lax.broadcasted_iota(jnp.int32, (1, width), 1)
    if width == LANES:
        return [lane < HEAD_DIM, lane >= HEAD_DIM]
    m0 = (lane < MLA_NOPE) | ((lane >= LANES) & (lane < LANES + MLA_ROPE))
    m1 = ((lane >= MLA_NOPE) & (lane < LANES)) | ((lane >= LANES + MLA_ROPE) & (lane < LANES + 2 * MLA_ROPE))
    return [m0, m1]


def _attention(jobs):
    steps, qa = [], []
    for j, (q, blocks, _) in enumerate(jobs):
        masks = _pair_masks(q.shape[1])
        qa.append([jnp.where(masks[a], q, jnp.zeros_like(q)) for a in range(2)])
        chunks = []
        for blk in blocks:
            k, vt, post = blk[:3]
            cuts = blk[3] if len(blk) > 3 else list(range(0, k.shape[0], KV_CHUNK)) + [k.shape[0]]
            chunks += [(k, vt, post, lo, hi) for lo, hi in zip(cuts[:-1], cuts[1:])]
        steps += [(j, c, ci == len(chunks) - 1) for ci, c in enumerate(chunks)]

    def scores(step):
        j, (k, _, post, lo, hi), _ = step
        kc = k[lo:hi, :]
        s = [_dot_nt(kc, qa[j][a]) for a in range(2)]
        return s if post is None else [post(a, s[a], lo, hi) for a in range(2)]

    outs = [None] * len(jobs)
    m, acc = [None, None], [None, None]
    pending = [scores(st) for st in steps[:SCORES_AHEAD]]
    for n, (j, (_, vt, _, lo, hi), last) in enumerate(steps):
        if n + SCORES_AHEAD < len(steps):
            pending.append(scores(steps[n + SCORES_AHEAD]))
        s_cur = pending.pop(0)
        ones = jnp.ones((ONES_ROWS, hi - lo), BF16)
        for a in range(2):
            vta = jnp.concatenate([vt[a * HEAD_DIM:(a + 1) * HEAD_DIM, lo:hi], ones], axis=0)
            s = s_cur[a]
            mc = jnp.max(s, axis=0, keepdims=True)
            mn = mc if m[a] is None else jnp.maximum(m[a], mc)
            pv = _dot(vta, jnp.exp2(s - mn).astype(BF16))
            acc[a] = pv if m[a] is None else jnp.exp2(m[a] - mn) * acc[a] + pv
            m[a] = mn
        if last:
            sinks, heads = jobs[j][2], []
            for a in range(2):
                l = acc[a][HEAD_DIM:HEAD_DIM + 1, :]
                if sinks[a] is not None:
                    mf = jnp.maximum(m[a], sinks[a])
                    scale = jnp.exp2(m[a] - mf)
                    l = scale * l + jnp.exp2(sinks[a] - mf)
                    heads.append(acc[a][:HEAD_DIM, :] * (scale / l))
                else:
                    heads.append(acc[a][:HEAD_DIM, :] / l)
            outs[j] = jnp.concatenate(heads, axis=0).T
            m, acc = [None, None], [None, None]
    return outs


_NO_SINKS = (None, None)


def _ctx_attn_kernel(sink_ref, qm_ref, km_ref, vmt_ref, qs_ref, ks_ref, vs_ref, qn_ref, kn_ref, vn_ref,
                     om_ref, os_ref, on_ref):
    jobs = []
    for i in range(N_PAIRS):
        q = qm_ref[:, i * MLA_QK_BLK:(i + 1) * MLA_QK_BLK]
        k = km_ref[:, i * MLA_QK_BLK:(i + 1) * MLA_QK_BLK]
        vt = vmt_ref[i * LANES:(i + 1) * LANES, :]
        jobs.append((q, [(k, vt, None)], _NO_SINKS))
    ks, vst = ks_ref[...], _feature_major(vs_ref[...])
    for j in range(2):
        sinks = (sink_ref[j] * LOG2E, sink_ref[j + 2] * LOG2E)
        jobs.append((qs_ref[:, j * LANES:(j + 1) * LANES], [(ks, vst, None)], sinks))
    for j in range(2):
        sl = slice(j * LANES, (j + 1) * LANES)
        jobs.append((qn_ref[:, sl], [(kn_ref[:, sl], _feature_major(vn_ref[:, sl]), None)], _NO_SINKS))
    outs = _attention(jobs)
    for i in range(N_PAIRS):
        om_ref[:, i * LANES:(i + 1) * LANES] = outs[i].astype(BF16)
    for j in range(2):
        os_ref[:, j * LANES:(j + 1) * LANES] = outs[N_PAIRS + j].astype(BF16)
        on_ref[:, j * LANES:(j + 1) * LANES] = outs[N_PAIRS + 2 + j].astype(BF16)


def _context_attention(sink_l, qm, km, vmt, qs, ks, vs, qn, kn, vn):
    row = lambda a: (pl.BlockSpec((a.shape[0], SEQ), lambda b: (0, b)) if a is vmt
                     else pl.BlockSpec((SEQ, a.shape[1]), lambda b: (b, 0)))
    ins = [qm, km, vmt, qs, ks, vs, qn, kn, vn]
    widths = [N_PAIRS * LANES, 256, 256]
    return pl.pallas_call(
        _ctx_attn_kernel,
        grid=(BATCH,),
        in_specs=[pl.BlockSpec(memory_space=pltpu.SMEM)] + [row(a) for a in ins],
        out_specs=[pl.BlockSpec((SEQ, w), lambda b: (b, 0)) for w in widths],
        out_shape=[jax.ShapeDtypeStruct((N_CTX_TOK, w), BF16) for w in widths],
        compiler_params=_cparams(("arbitrary",)),
        name="attn_ctx",
    )(sink_l, *ins)


def _lat_mla_kernel(q_ref, kl_ref, vlt_ref, kc_ref, vct_ref, o_ref):
    blocks = [(kc_ref, vct_ref, None), (kl_ref, vlt_ref, None, list(range(0, DEC_SEQ + 1, MLA_KV_CHUNK)))]
    o_ref[...] = _attention([(q_ref[...], blocks, _NO_SINKS)])[0].astype(BF16)


def _latent_mla(qm, km, vmt, kmc, vmct):
    nq = DEC_SEQ // TQ
    return pl.pallas_call(
        _lat_mla_kernel,
        grid=(DEC_BATCH, N_PAIRS, nq),
        in_specs=[
            pl.BlockSpec((TQ, MLA_QK_BLK), lambda b, i, t: (b * nq + t, i)),
            pl.BlockSpec((DEC_SEQ, MLA_QK_BLK), lambda b, i, t: (b, i)),
            pl.BlockSpec((LANES, DEC_SEQ), lambda b, i, t: (i, b)),
            pl.BlockSpec((PAST_LEN, MLA_QK_BLK), lambda b, i, t: (b, i)),
            pl.BlockSpec((LANES, PAST_LEN), lambda b, i, t: (i, b)),
        ],
        out_specs=pl.BlockSpec((TQ, LANES), lambda b, i, t: (b * nq + t, i)),
        out_shape=jax.ShapeDtypeStruct((N_LAT_TOK, N_PAIRS * LANES), BF16),
        compiler_params=_cparams(("arbitrary", "arbitrary", "arbitrary")),
        name="attn_lat_mla",
    )(qm, km, vmt, kmc, vmct)


def _lat_swa_kernel(sink_ref, q_ref, k_ref, v_ref, kc_ref, vc_ref, o_ref):
    t = pl.program_id(1)
    q0 = t * TQ
    ws = pl.multiple_of(jnp.clip(q0 - SWA_WINDOW, 0, DEC_SEQ - SWA_KWIN), SWA_WINDOW)
    kw = k_ref[pl.ds(ws, SWA_KWIN), :]
    vw = _feature_major(v_ref[pl.ds(ws, SWA_KWIN), :])
    kc = kc_ref[...].astype(BF16)
    vc = _feature_major(vc_ref[...])
    kpos = ws + lax.broadcasted_iota(jnp.int32, (SWA_KWIN, TQ), 0)
    qpos = q0 + lax.broadcasted_iota(jnp.int32, (SWA_KWIN, TQ), 1)
    in_band = jnp.abs(kpos - qpos) <= SWA_WINDOW

    def band(a, s, lo, hi):
        return jnp.where(in_band[lo:hi, :], s, MASK_VALUE)

    jobs = []
    for j in range(2):
        sinks = (sink_ref[j] * LOG2E, sink_ref[j + 2] * LOG2E)
        jobs.append((q_ref[:, j * LANES:(j + 1) * LANES], [(kw, vw, band), (kc, vc, None)], sinks))
    for j, o in enumerate(_attention(jobs)):
        o_ref[:, j * LANES:(j + 1) * LANES] = o.astype(BF16)


def _latent_swa(sink_l, qs, ks, vs, ksc, vsc):
    nq = DEC_SEQ // TQ
    return pl.pallas_call(
        _lat_swa_kernel,
        grid=(DEC_BATCH, nq),
        in_specs=[
            pl.BlockSpec(memory_space=pltpu.SMEM),
            pl.BlockSpec((TQ, 256), lambda b, t: (b * nq + t, 0)),
            pl.BlockSpec((DEC_SEQ, LANES), lambda b, t: (b, 0)),
            pl.BlockSpec((DEC_SEQ, LANES), lambda b, t: (b, 0)),
            pl.BlockSpec((PAST_LEN, LANES), lambda b, t: (b, 0)),
            pl.BlockSpec((PAST_LEN, LANES), lambda b, t: (b, 0)),
        ],
        out_specs=pl.BlockSpec((TQ, 256), lambda b, t: (b * nq + t, 0)),
        out_shape=jax.ShapeDtypeStruct((N_LAT_TOK, 256), BF16),
        compiler_params=_cparams(("arbitrary", "arbitrary")),
        name="attn_lat_swa",
    )(sink_l, qs, ks, vs, ksc, vsc)


_NA_DR = 2 * NA_KH - 1


def _build_na_bias_tiles(rpb_ref, tile_ref):
    kc = lax.broadcasted_iota(jnp.int32, (GRID_W, LANES), 0)
    qc = lax.broadcasted_iota(jnp.int32, (GRID_W, LANES), 1) % GRID_W
    rel = jnp.clip(kc - qc, -(NA_KW - 1), NA_KW - 1) + NA_KW - 1
    col_start = jnp.clip(qc - NA_KW // 2, 0, GRID_W - NA_KW)
    col_ok = (kc >= col_start) & (kc < col_start + NA_KW)
    masked = jnp.full((GRID_W, LANES), MASK_VALUE, F32)
    n_rel = 2 * NA_KW - 1

    def one_tile(hd, carry):
        tile = masked
        for c in range(n_rel):
            tile = jnp.where(col_ok & (rel == c), rpb_ref[hd * n_rel + c] * LOG2E, tile)
        tile_ref[(hd // _NA_DR) * (_NA_DR + 1) + hd % _NA_DR] = tile
        return carry

    lax.fori_loop(0, NA_HEADS * _NA_DR, one_tile, 0)
    for h in range(NA_HEADS):
        tile_ref[h * (_NA_DR + 1) + _NA_DR] = masked


def _lat_na_kernel(rpb_ref, q_ref, k_ref, v_ref, kc_ref, vc_ref, o_ref, tile_ref):
    t = pl.program_id(1)

    @pl.when((pl.program_id(0) == 0) & (t == 0))
    def _():
        _build_na_bias_tiles(rpb_ref, tile_ref)

    n_rows = DEC_SEQ // GRID_W
    r0 = t * NA_ROWS_PER_TILE
    ws_row = jnp.clip(r0 - NA_KH // 2, 0, n_rows - NA_KEY_ROWS)
    ws = pl.multiple_of(ws_row * GRID_W, GRID_W)
    lane_lo = lax.broadcasted_iota(jnp.int32, (GRID_W, LANES), 1) < GRID_W

    def tile_index(h, a_q, jj):
        r, kr = r0 + a_q, ws_row + jj
        rs = jnp.clip(r - NA_KH // 2, 0, n_rows - NA_KH)
        d = jnp.where((kr >= rs) & (kr < rs + NA_KH), kr - r + NA_KH - 1, _NA_DR)
        return h * (_NA_DR + 1) + d

    jobs = []
    for j in range(2):
        sl = slice(j * LANES, (j + 1) * LANES)
        kw = k_ref[pl.ds(ws, NA_KWIN), sl]
        vw = _feature_major(v_ref[pl.ds(ws, NA_KWIN), sl])
        kc = kc_ref[:, sl].astype(BF16)
        vc = _feature_major(vc_ref[:, sl])

        def add_bias(a, s, lo, hi, j=j):
            assert (lo, hi) == (0, NA_KWIN)
            h = 2 * j + a
            rows = []
            for jj in range(NA_KEY_ROWS):
                halves = [jnp.where(lane_lo, tile_ref[tile_index(h, 2 * u, jj)], tile_ref[tile_index(h, 2 * u + 1, jj)])
                          for u in range(NA_ROWS_PER_TILE // 2)]
                rows.append(jnp.concatenate(halves, axis=1))
            return s + jnp.concatenate(rows, axis=0)

        jobs.append((q_ref[:, sl], [(kw, vw, add_bias), (kc, vc, None)], _NO_SINKS))
    for j, o in enumerate(_attention(jobs)):
        o_ref[:, j * LANES:(j + 1) * LANES] = o.astype(BF16)


def _latent_na(rpb_l, qn, kn, vn, knc, vnc):
    nq = DEC_SEQ // TQ
    return pl.pallas_call(
        _lat_na_kernel,
        grid=(DEC_BATCH, nq),
        in_specs=[
            pl.BlockSpec(memory_space=pltpu.SMEM),
            pl.BlockSpec((TQ, 256), lambda b, t: (b * nq + t, 0)),
            pl.BlockSpec((DEC_SEQ, 256), lambda b, t: (b, 0)),
            pl.BlockSpec((DEC_SEQ, 256), lambda b, t: (b, 0)),
            pl.BlockSpec((PAST_LEN, 256), lambda b, t: (b, 0)),
            pl.BlockSpec((PAST_LEN, 256), lambda b, t: (b, 0)),
        ],
        out_specs=pl.BlockSpec((TQ, 256), lambda b, t: (b * nq + t, 0)),
        out_shape=jax.ShapeDtypeStruct((N_LAT_TOK, 256), BF16),
        scratch_shapes=[pltpu.VMEM((NA_HEADS * (_NA_DR + 1), GRID_W, LANES), F32)],
        compiler_params=_cparams(("arbitrary", "arbitrary")),
        name="attn_lat_na",
    )(rpb_l.reshape(-1), qn, kn, vn, knc, vnc)


_GRP_LANE0 = 0
_EXP_LANE0 = N_GROUPS


def _lane_first_max(x, valid, lane):
    xm = jnp.where(valid, x, -jnp.inf)
    mx = jnp.max(xm, axis=-1, keepdims=True)
    idx = jnp.min(jnp.where(valid & (xm == mx), lane, LANES), axis=-1, keepdims=True)
    return mx, idx


def _tail_kernel(x_ref, om_ref, os_ref, on_ref, mod_ref, n2_ref, wo_ref, wr_ref, x1_ref, h2_ref, gates_ref):
    n_sub = x_ref.shape[0] // TAIL_SUB_ROWS
    subs = [slice(i * TAIL_SUB_ROWS, (i + 1) * TAIL_SUB_ROWS) for i in range(n_sub)]
    wo = wo_ref
    attn = [(_dot(om_ref[r, :], wo[0:512, :]) + _dot(os_ref[r, :], wo[512:768, :]) + _dot(on_ref[r, :], wo[768:1024, :]))
            for r in subs]
    for r, attn_r in zip(subs, attn):
        _tail_rows(r, attn_r, x_ref, mod_ref, n2_ref, wr_ref, x1_ref, h2_ref, gates_ref)


def _tail_rows(r, attn, x_ref, mod_ref, n2_ref, wr_ref, x1_ref, h2_ref, gates_ref):
    m = mod_ref[0]
    x1 = x_ref[r, :] + m[2:3] * attn
    x1_ref[r, :] = x1
    h2 = _rms(x1, n2_ref[...]) * (1.0 + m[4:5]) + m[3:4]
    h2_ref[r, :] = h2.astype(BF16)
    hi = h2.astype(BF16)
    lo = (h2 - hi.astype(F32)).astype(BF16)
    a = _dot(hi, wr_ref[...])
    logits = a[:, :LANES] + a[:, LANES:] + _dot(lo, wr_ref[:, :LANES])
    lane = lax.broadcasted_iota(jnp.int32, logits.shape, 1)
    is_grp = lane < N_GROUPS
    gmax, gidx = _lane_first_max(logits, is_grp, lane)
    gden = jnp.sum(jnp.where(is_grp, jnp.exp(logits - gmax), 0.0), axis=-1, keepdims=True)
    grp_gate = 1.0 / gden
    in_grp = (lane >= _EXP_LANE0) & (lane < _EXP_LANE0 + N_EXPERTS) & ((lane // EXPERTS_PER_GROUP - 1) == gidx)
    v1, i1 = _lane_first_max(logits, in_grp, lane)
    v2, i2 = _lane_first_max(logits, in_grp & (lane != i1), lane)
    e2 = jnp.exp(v2 - v1)
    w1 = grp_gate / (1.0 + e2)
    w2 = grp_gate * e2 / (1.0 + e2)
    gates_ref[r, :] = jnp.where(lane == i1, w1, 0.0) + jnp.where(lane == i2, w2, 0.0)


def _tail(x, om, osw, ona, mod_l, lw, latent):
    n_tok = x.shape[0]
    tm = TM_TAIL
    tiles_per_seq = DEC_SEQ // tm
    mod_idx = (lambda i: (1 + i // tiles_per_seq, 0, 0)) if latent else (lambda i: (0, 0, 0))
    row = lambda w: pl.BlockSpec((tm, w), lambda i: (i, 0))
    whole = lambda a: pl.BlockSpec(a.shape, lambda i: (0,) * a.ndim)
    return pl.pallas_call(
        _tail_kernel,
        grid=(n_tok // tm,),
        in_specs=[row(D_MODEL), row(512), row(256), row(256), pl.BlockSpec((1, N_MOD, D_MODEL), mod_idx),
                  whole(lw["norm2"]), whole(lw["wout"]), whole(lw["wr"])],
        out_specs=[row(D_MODEL), row(D_MODEL), row(LANES)],
        out_shape=[jax.ShapeDtypeStruct((n_tok, D_MODEL), F32), jax.ShapeDtypeStruct((n_tok, D_MODEL), BF16),
                   jax.ShapeDtypeStruct((n_tok, LANES), F32)],
        compiler_params=_cparams(("arbitrary",)),
        name="tail_lat" if latent else "tail_ctx",
    )(x, om, osw, ona, mod_l, lw["norm2"], lw["wout"], lw["wr"])


def _moe_kernel(final, h2_ref, gates_ref, x1_ref, mod_ref, nf_ref, wg_ref, wu_ref, wd_ref, o_ref, acc_ref):
    g = pl.program_id(1)
    h2 = h2_ref[...]
    gates = gates_ref[...]
    lane = lax.broadcasted_iota(jnp.int32, gates.shape, 1)
    contrib = None
    for e in range(EXPERTS_PER_GROUP):
        ge = jnp.sum(jnp.where(lane == _EXP_LANE0 + g * EXPERTS_PER_GROUP + e, gates, 0.0), axis=-1, keepdims=True)
        gu = _dot(h2, jnp.concatenate([wg_ref[0, e], wu_ref[0, e]], axis=1))
        hg, hu = gu[:, :EXPERT_FF], gu[:, EXPERT_FF:]
        act = hg * (1.0 / (1.0 + jnp.exp(-hg))) * hu * ge
        c = _dot(act.astype(BF16), wd_ref[0, e])
        contrib = c if contrib is None else contrib + c

    @pl.when(g == 0)
    def _():
        acc_ref[...] = contrib

    @pl.when((g > 0) & (g < N_GROUPS - 1))
    def _():
        acc_ref[...] += contrib

    @pl.when(g == N_GROUPS - 1)
    def _():
        y = x1_ref[...] + mod_ref[0][5:6] * (acc_ref[...] + contrib)
        if final:
            y = _rms(y, nf_ref[...])
        o_ref[...] = y


def _moe(h2, gates, x1, mod_l, l, moe_w, norm_final, latent, final):
    n_tok = h2.shape[0]
    tm = TM_MOE
    tiles_per_seq = DEC_SEQ // tm
    mod_idx = (lambda i, g: (1 + i // tiles_per_seq, 0, 0)) if latent else (lambda i, g: (0, 0, 0))
    epg = EXPERTS_PER_GROUP
    return pl.pallas_call(
        functools.partial(_moe_kernel, final),
        grid=(n_tok // tm, N_GROUPS),
        in_specs=[
            pl.BlockSpec((tm, D_MODEL), lambda i, g: (i, 0)),
            pl.BlockSpec((tm, LANES), lambda i, g: (i, 0)),
            pl.BlockSpec((tm, D_MODEL), lambda i, g: (i, 0)),
            pl.BlockSpec((1, N_MOD, D_MODEL), mod_idx),
            pl.BlockSpec((1, D_MODEL), lambda i, g: (0, 0)),
            pl.BlockSpec((1, epg, D_MODEL, EXPERT_FF), lambda i, g: (l, g, 0, 0)),
            pl.BlockSpec((1, epg, D_MODEL, EXPERT_FF), lambda i, g: (l, g, 0, 0)),
            pl.BlockSpec((1, epg, EXPERT_FF, D_MODEL), lambda i, g: (l, g, 0, 0)),
        ],
        out_specs=pl.BlockSpec((tm, D_MODEL), lambda i, g: (i, 0)),
        out_shape=jax.ShapeDtypeStruct((n_tok, D_MODEL), F32),
        scratch_shapes=[pltpu.VMEM((tm, D_MODEL), F32)],
        compiler_params=_cparams(("arbitrary", "arbitrary")),
        name=("moe_lat" if latent else "moe_ctx") + ("_final" if final else ""),
    )(h2, gates, x1, mod_l, norm_final, *moe_w)


def _rot_cols(w, d):
    k, n = w.shape
    hh = d // 4
    w4 = w.reshape(k, n // (2 * hh), 2, hh)
    return jnp.stack([-w4[:, :, 1], w4[:, :, 0]], axis=2).reshape(k, n)


def _heads(w, d, order, axis):
    parts = [lax.slice_in_dim(w, h * d, (h + 1) * d, axis=axis) for h in order]
    return jnp.concatenate(parts, axis=axis)


def _layer_weights(l, norm1, norm2, w_in, g_qa, w_uq, g_kva, w_ukv, w_out, w_router_grp, w_router_exp):
    wi = w_in[l]
    z64 = jnp.zeros((D_MODEL, 64), F32)
    cq, ckv, kpe = wi[:, 0:256], wi[:, 256:384], wi[:, 384:416]
    qs = _heads(wi[:, 416:672], HEAD_DIM, (0, 2, 1, 3), 1) * (HEAD_SCALE * LOG2E)
    ks, vs = wi[:, 672:800], wi[:, 800:928]
    qn, kn, vn = wi[:, 928:1184] * (HEAD_SCALE * LOG2E), wi[:, 1184:1440], wi[:, 1440:1696]
    kped = jnp.concatenate([kpe, kpe, z64], axis=1)
    kper = _rot_cols(kpe, MLA_ROPE)
    ctx_cols = [cq, ckv, qs, ks, vs, qn, kn, vn, kped]
    lat_cols = ctx_cols + [_rot_cols(qs, HEAD_DIM), _rot_cols(ks, HEAD_DIM), jnp.concatenate([kper, kper, z64], axis=1)]
    wq = w_uq[l].reshape(MLA_Q_LORA, MLA_HEADS, MLA_NOPE + MLA_ROPE) * (MLA_SCALE * LOG2E)
    nope, ropew = wq[:, :, :MLA_NOPE], wq[:, :, MLA_NOPE:]
    z64q = jnp.zeros((MLA_Q_LORA, 64), F32)
    blocks, rots = [], []
    for i in range(N_PAIRS):
        blocks += [nope[:, 2 * i], nope[:, 2 * i + 1], ropew[:, 2 * i], ropew[:, 2 * i + 1], z64q]
        rots += [_rot_cols(ropew[:, 2 * i], MLA_ROPE), _rot_cols(ropew[:, 2 * i + 1], MLA_ROPE), z64q]
    wkv = w_ukv[l].reshape(MLA_KV_LORA, MLA_HEADS, MLA_NOPE + MLA_V)
    wuk = wkv[:, :, :MLA_NOPE].reshape(MLA_KV_LORA, -1)
    wuvt = wkv[:, :, MLA_NOPE:].reshape(MLA_KV_LORA, -1).T
    wo = w_out[l]
    wout = jnp.concatenate([wo[:512], _heads(wo[512:768], HEAD_DIM, (0, 2, 1, 3), 0), wo[768:]], axis=0)
    wr = jnp.concatenate([w_router_grp[l], w_router_exp[l], jnp.zeros((D_MODEL, LANES - N_GROUPS - N_EXPERTS), F32)],
                         axis=1)
    wr_hi = wr.astype(BF16)
    wr_lo = (wr - wr_hi.astype(F32)).astype(BF16)
    return {
        "norm1": norm1[l][None], "norm2": norm2[l][None], "g_qa": g_qa[l][None], "g_kva": g_kva[l][None],
        "win": jnp.concatenate(lat_cols, axis=1).astype(BF16),
        "wuq": jnp.concatenate(blocks + rots, axis=1).astype(BF16),
        "wuk": wuk.astype(BF16),
        "wuvt": wuvt.astype(BF16),
        "wout": wout.astype(BF16),
        "wr": jnp.concatenate([wr_hi, wr_lo], axis=1),
    }


def _rope_tables():
    n_rows = DEC_SEQ // GRID_W
    lane = np.arange(LANES)

    def tab(d, used_lanes):
        hh = d // 4
        i = lane % d
        freq = ROPE_THETA ** (-jnp.asarray(i % hh, F32) / hh)
        by_row = jnp.asarray((i // (2 * hh)) == 0)[None, None, :]
        valid = jnp.asarray(lane < used_lanes)[None, None, :]
        ang_r = jnp.arange(n_rows, dtype=F32)[:, None] * freq[None, :]
        ang_c = jnp.arange(GRID_W, dtype=F32)[:, None] * freq[None, :]

        def expand(fn):
            t = jnp.where(by_row, fn(ang_r)[:, None, :], fn(ang_c)[None, :, :])
            return jnp.where(valid, t, 0.0).reshape(DEC_SEQ, LANES)

        return expand(jnp.cos), expand(jnp.sin)

    c64, s64 = tab(HEAD_DIM, LANES)
    c32, s32 = tab(MLA_ROPE, 2 * MLA_ROPE)
    return c64, s64, c32, s32


def kernel(x_prompt, x_sample, cache_mla_ckv, cache_mla_kpe, cache_swa_k, cache_swa_v, cache_na_k, cache_na_v, c, c_ctx, w_mod, b_mod, norm1, norm2, w_in, g_qa, w_uq, g_kva, w_ukv, swa_sink, na_rpb, w_out, w_router_grp, w_router_exp, w_gate, w_up, w_down, norm_final):
    cpad = jnp.concatenate([c_ctx[None], c, jnp.zeros((8 - 1 - DEC_BATCH, D_MODEL), F32)], axis=0)
    mod = _modulation(cpad, w_mod, b_mod).reshape(DEPTH, 8, N_MOD, D_MODEL)
    tabs = _rope_tables()
    nf = norm_final[None]
    xp = x_prompt.reshape(N_CTX_TOK, D_MODEL)
    xs = x_sample.reshape(N_LAT_TOK, D_MODEL)
    caches = [[] for _ in range(6)]
    moe_w = (w_gate.astype(BF16), w_up.astype(BF16), w_down.astype(BF16))
    for l in range(DEPTH):
        lw = _layer_weights(l, norm1, norm2, w_in, g_qa, w_uq, g_kva, w_ukv, w_out, w_router_grp, w_router_exp)
        final = l == DEPTH - 1
        outs = _projections(xp, mod[l], lw, None, rope=False)
        om, osw, ona = _context_attention(swa_sink[l], *outs[:9])
        for dst, a in zip(caches, outs[9:]):
            dst.append(a)
        x1, h2, gates = _tail(xp, om, osw, ona, mod[l], lw, latent=False)
        xp = _moe(h2, gates, x1, mod[l], l, moe_w, nf, latent=False, final=final)
        qm, km, vm, qs, ks, vs, qn, kn, vn = _projections(xs, mod[l], lw, tabs, rope=True)
        kpe_c = cache_mla_kpe[:, l].reshape(DEC_BATCH * PAST_LEN, MLA_ROPE)
        kpe_dup = jnp.concatenate([kpe_c, kpe_c, jnp.zeros((DEC_BATCH * PAST_LEN, 64), F32)], axis=1)
        kmc, vmc = _expand_cached_mla(cache_mla_ckv[:, l].reshape(DEC_BATCH * PAST_LEN, MLA_KV_LORA), kpe_dup,
                                      lw["wuk"], lw["wuvt"])
        om = _latent_mla(qm, km, vm, kmc, vmc)
        flat = lambda a: a[:, l].reshape(DEC_BATCH * PAST_LEN, -1)
        osw = _latent_swa(swa_sink[l], qs, ks, vs, flat(cache_swa_k), flat(cache_swa_v))
        ona = _latent_na(na_rpb[l], qn, kn, vn, flat(cache_na_k), flat(cache_na_v))
        x1, h2, gates = _tail(xs, om, osw, ona, mod[l], lw, latent=True)
        xs = _moe(h2, gates, x1, mod[l], l, moe_w, nf, latent=True, final=final)
    stack = lambda parts, tail: jnp.stack([p.reshape((BATCH, SEQ) + tail) for p in parts], axis=1)
    return (xp.reshape(BATCH, SEQ, D_MODEL), xs.reshape(DEC_BATCH, DEC_SEQ, D_MODEL),
            stack(caches[0], (MLA_KV_LORA,)), stack(caches[1], (MLA_ROPE,)),
            stack(caches[2], (SWA_KV_HEADS, HEAD_DIM)), stack(caches[3], (SWA_KV_HEADS, HEAD_DIM)),
            stack(caches[4], (NA_HEADS, HEAD_DIM)), stack(caches[5], (NA_HEADS, HEAD_DIM)))
```

```python
import functools

import jax
import jax.numpy as jnp
import numpy as np
from jax import lax
from jax.experimental import pallas as pl
from jax.experimental.pallas import tpu as pltpu

D_MODEL = 1024
BATCH = 32
SEQ = 256
DEPTH = 2
DEC_BATCH = 2
DEC_SEQ = 4096
PAST_LEN = 512
GRID_W = 64
HEAD_DIM = 64
MLA_HEADS = 8
MLA_Q_LORA = 256
MLA_KV_LORA = 128
MLA_NOPE = 64
MLA_ROPE = 32
MLA_V = 64
SWA_HEADS = 4
SWA_KV_HEADS = 2
SWA_WINDOW = 128
NA_HEADS = 4
NA_KH = 8
NA_KW = 16
N_GROUPS = 4
EXPERTS_PER_GROUP = 4
N_EXPERTS = 16
EXPERT_FF = 256
N_MOD = 6
ROPE_THETA = 10000.0
EPS = 1e-6
MASK_VALUE = -1e30
MLA_SCALE = (MLA_NOPE + MLA_ROPE) ** -0.5
HEAD_SCALE = HEAD_DIM ** -0.5
LOG2E = 1.4426950408889634

LANES = 128
N_PAIRS = MLA_HEADS // 2
MLA_QK_BLK = 2 * LANES
N_CTX_TOK = BATCH * SEQ
N_LAT_TOK = DEC_BATCH * DEC_SEQ

_C_CQ, _C_CKV, _C_QS, _C_KS, _C_VS, _C_QN, _C_KN, _C_VN, _C_KPE, _C_QSR, _C_KSR, _C_KPER, _C_END = (
    0, 256, 384, 640, 768, 896, 1152, 1408, 1664, 1792, 2048, 2176, 2304)

TM_PROJ = 512
TM_TAIL = 512
TAIL_SUB_ROWS = 128
TM_MOE = 1024
TQ = 256
KV_CHUNK = 2048
MLA_KV_CHUNK = 1024
SCORES_AHEAD = 2
ONES_ROWS = 16
SWA_KWIN = TQ + 2 * SWA_WINDOW
NA_ROWS_PER_TILE = TQ // GRID_W
NA_KEY_ROWS = 12
NA_KWIN = NA_KEY_ROWS * GRID_W
VMEM_LIMIT = 56 * 1024 * 1024

F32 = jnp.float32
BF16 = jnp.bfloat16


def _dot(a, b):
    return jnp.dot(a, b, preferred_element_type=F32)


def _dot_nt(a, b):
    return lax.dot_general(a, b, (((1,), (1,)), ((), ())), preferred_element_type=F32)


def _rms(x, g):
    return x * lax.rsqrt(jnp.mean(x * x, axis=-1, keepdims=True) + EPS) * g


def _cparams(sem):
    return pltpu.CompilerParams(dimension_semantics=sem, vmem_limit_bytes=VMEM_LIMIT)


def _mod_kernel(c_ref, w_ref, b_ref, o_ref):
    c = c_ref[...]
    s = c * (1.0 / (1.0 + jnp.exp(-c)))
    o_ref[0] = jnp.dot(s, w_ref[0], preferred_element_type=F32, precision=lax.Precision.HIGHEST) + b_ref[0]


def _modulation(cpad, w_mod, b_mod):
    nt = 1024
    return pl.pallas_call(
        _mod_kernel,
        grid=(DEPTH, N_MOD * D_MODEL // nt),
        in_specs=[
            pl.BlockSpec((8, D_MODEL), lambda l, n: (0, 0)),
            pl.BlockSpec((1, D_MODEL, nt), lambda l, n: (l, 0, n)),
            pl.BlockSpec((1, 1, nt), lambda l, n: (l, 0, n)),
        ],
        out_specs=pl.BlockSpec((1, 8, nt), lambda l, n: (l, 0, n)),
        out_shape=jax.ShapeDtypeStruct((DEPTH, 8, N_MOD * D_MODEL), F32),
        compiler_params=_cparams(("arbitrary", "arbitrary")),
        name="modulation",
    )(cpad, w_mod, b_mod.reshape(DEPTH, 1, N_MOD * D_MODEL))


def _proj_kernel(rope, *refs):
    if rope:
        (x_ref, mod_ref, n1_ref, win_ref, gqa_ref, wuq_ref, gkva_ref, wuk_ref, wuvt_ref,
         c64_ref, s64_ref, c32_ref, s32_ref,
         qm_ref, km_ref, vmt_ref, qs_ref, ks_ref, vs_ref, qn_ref, kn_ref, vn_ref) = refs
    else:
        (x_ref, mod_ref, n1_ref, win_ref, gqa_ref, wuq_ref, gkva_ref, wuk_ref, wuvt_ref,
         qm_ref, km_ref, vmt_ref, qs_ref, ks_ref, vs_ref, qn_ref, kn_ref, vn_ref,
         ckv_o, kpe_o, ks_o, vs_o, kn_o, vn_o) = refs
    m = mod_ref[0]
    h = _rms(x_ref[...], n1_ref[...]) * (1.0 + m[1:2]) + m[0:1]
    p = _dot_nt(h.astype(BF16), win_ref[...])
    qm = _dot(_rms(p[:, _C_CQ:_C_CKV], gqa_ref[...]).astype(BF16), wuq_ref[...])
    ckv = _rms(p[:, _C_CKV:_C_QS], gkva_ref[...])
    ckv_b = ckv.astype(BF16)
    kn_mla = _dot(ckv_b, wuk_ref[...])
    vmt_ref[...] = _dot_nt(wuvt_ref[...], ckv_b).astype(BF16)
    qs = p[:, _C_QS:_C_KS]
    ks = p[:, _C_KS:_C_VS]
    kpe = p[:, _C_KPE:_C_QSR]
    if rope:
        c64, s64, c32, s32 = c64_ref[...], s64_ref[...], c32_ref[...], s32_ref[...]
        qsr = p[:, _C_QSR:_C_KSR]
        qs = jnp.concatenate(
            [qs[:, j * LANES:(j + 1) * LANES] * c64 + qsr[:, j * LANES:(j + 1) * LANES] * s64 for j in range(2)], axis=1)
        ks = ks * c64 + p[:, _C_KSR:_C_KPER] * s64
        kpe = kpe * c32 + p[:, _C_KPER:_C_END] * s32
    for i in range(N_PAIRS):
        lo = i * MLA_QK_BLK
        qrope = qm[:, lo + LANES:lo + MLA_QK_BLK]
        if rope:
            r0 = N_PAIRS * MLA_QK_BLK + i * LANES
            qrope = qrope * c32 + qm[:, r0:r0 + LANES] * s32
        qm_ref[:, lo:lo + LANES] = qm[:, lo:lo + LANES].astype(BF16)
        qm_ref[:, lo + LANES:lo + MLA_QK_BLK] = qrope.astype(BF16)
        km_ref[:, lo:lo + LANES] = kn_mla[:, i * LANES:(i + 1) * LANES].astype(BF16)
        km_ref[:, lo + LANES:lo + MLA_QK_BLK] = kpe.astype(BF16)
    qs_ref[...] = qs.astype(BF16)
    ks_ref[...] = ks.astype(BF16)
    vs_ref[...] = p[:, _C_VS:_C_QN].astype(BF16)
    qn_ref[...] = p[:, _C_QN:_C_KN].astype(BF16)
    kn_ref[...] = p[:, _C_KN:_C_VN].astype(BF16)
    vn_ref[...] = p[:, _C_VN:_C_KPE].astype(BF16)
    if not rope:
        ckv_o[...] = ckv

        def put_feature_major(o_ref, val, n_feat):
            vt = val.T
            for bb in range(val.shape[0] // SEQ):
                o_ref[bb] = vt[:n_feat, bb * SEQ:(bb + 1) * SEQ]

        put_feature_major(kpe_o, kpe, MLA_ROPE)
        put_feature_major(ks_o, ks, 128)
        put_feature_major(vs_o, p[:, _C_VS:_C_QN], 128)
        put_feature_major(kn_o, p[:, _C_KN:_C_VN], 256)
        put_feature_major(vn_o, p[:, _C_VN:_C_KPE], 256)


def _projections(x, mod_l, lw, tabs, rope):
    n_tok = x.shape[0]
    tm = TM_PROJ
    tiles_per_seq = DEC_SEQ // tm
    win, wuq = lw["win"], lw["wuq"]
    if rope:
        mod_idx = lambda i: (1 + i // tiles_per_seq, 0, 0)
        win_cols, wuq_cols = _C_END, wuq.shape[1]
    else:
        mod_idx = lambda i: (0, 0, 0)
        win_cols, wuq_cols = _C_QSR, N_PAIRS * MLA_QK_BLK
    row = lambda w: pl.BlockSpec((tm, w), lambda i: (i, 0))
    whole = lambda a: pl.BlockSpec(a.shape, lambda i: (0,) * a.ndim)
    lead = lambda a, n: pl.BlockSpec((a.shape[0], n), lambda i: (0, 0))
    lead_rows = lambda a, n: pl.BlockSpec((n, a.shape[1]), lambda i: (0, 0))
    in_specs = [row(D_MODEL), pl.BlockSpec((1, N_MOD, D_MODEL), mod_idx), whole(lw["norm1"]), lead_rows(win, win_cols),
                whole(lw["g_qa"]), lead(wuq, wuq_cols), whole(lw["g_kva"]), whole(lw["wuk"]), whole(lw["wuvt"])]
    args = [x, mod_l, lw["norm1"], win, lw["g_qa"], wuq, lw["g_kva"], lw["wuk"], lw["wuvt"]]
    widths = [N_PAIRS * MLA_QK_BLK, N_PAIRS * MLA_QK_BLK, None, 256, 128, 128, 256, 256, 256]
    vmt_spec = pl.BlockSpec((N_PAIRS * LANES, tm), lambda i: (0, i))
    out_specs = [vmt_spec if w is None else row(w) for w in widths]
    out_shape = [jax.ShapeDtypeStruct((N_PAIRS * LANES, n_tok) if w is None else (n_tok, w), BF16) for w in widths]
    if rope:
        tab_spec = pl.BlockSpec((tm, LANES), lambda i: (i % tiles_per_seq, 0))
        in_specs += [tab_spec] * 4
        args += list(tabs)
    else:
        out_specs.append(row(MLA_KV_LORA))
        out_shape.append(jax.ShapeDtypeStruct((n_tok, MLA_KV_LORA), F32))
        for w in [MLA_ROPE, 128, 128, 256, 256]:
            out_specs.append(pl.BlockSpec((tm // SEQ, w, SEQ), lambda i: (i, 0, 0)))
            out_shape.append(jax.ShapeDtypeStruct((n_tok // SEQ, w, SEQ), F32))
    return pl.pallas_call(
        functools.partial(_proj_kernel, rope),
        grid=(n_tok // tm,),
        in_specs=in_specs,
        out_specs=out_specs,
        out_shape=out_shape,
        compiler_params=_cparams(("arbitrary",)),
        name="proj_lat" if rope else "proj_ctx",
    )(*args)


def _ctxkv_kernel(ckv_ref, kpe_ref, wuk_ref, wuvt_ref, km_ref, vmt_ref):
    ckv_b = ckv_ref[...].astype(BF16)
    kn_mla = _dot(ckv_b, wuk_ref[...])
    kpe = kpe_ref[...].astype(BF16)
    for i in range(N_PAIRS):
        lo = i * MLA_QK_BLK
        km_ref[:, lo:lo + LANES] = kn_mla[:, i * LANES:(i + 1) * LANES].astype(BF16)
        km_ref[:, lo + LANES:lo + MLA_QK_BLK] = kpe
    vmt_ref[...] = _dot_nt(wuvt_ref[...], ckv_b).astype(BF16)


def _expand_cached_mla(ckv_c, kpe_dup, wuk, wuvt):
    n = ckv_c.shape[0]
    whole = lambda a: pl.BlockSpec(a.shape, lambda i: (0,) * a.ndim)
    return pl.pallas_call(
        _ctxkv_kernel,
        grid=(1,),
        in_specs=[whole(ckv_c), whole(kpe_dup), whole(wuk), whole(wuvt)],
        out_specs=[pl.BlockSpec((n, N_PAIRS * MLA_QK_BLK), lambda i: (0, 0)),
                   pl.BlockSpec((N_PAIRS * LANES, n), lambda i: (0, 0))],
        out_shape=[jax.ShapeDtypeStruct((n, N_PAIRS * MLA_QK_BLK), BF16),
                   jax.ShapeDtypeStruct((N_PAIRS * LANES, n), BF16)],
        compiler_params=_cparams(("arbitrary",)),
        name="expand_cached_mla",
    )(ckv_c, kpe_dup, wuk, wuvt)


def _feature_major(v):
    return v.astype(F32).T.astype(BF16)


def _pair_masks(width):
    lane = lax.broadcasted_iota(jnp.int32, (1, width), 1)
    if width == LANES:
        return [lane < HEAD_DIM, lane >= HEAD_DIM]
    m0 = (lane < MLA_NOPE) | ((lane >= LANES) & (lane < LANES + MLA_ROPE))
    m1 = ((lane >= MLA_NOPE) & (lane < LANES)) | ((lane >= LANES + MLA_ROPE) & (lane < LANES + 2 * MLA_ROPE))
    return [m0, m1]


def _attention(jobs):
    steps, qa = [], []
    for j, (q, blocks, _) in enumerate(jobs):
        masks = _pair_masks(q.shape[1])
        qa.append([jnp.where(masks[a], q, jnp.zeros_like(q)) for a in range(2)])
        chunks = []
        for blk in blocks:
            k, vt, post = blk[:3]
            cuts = blk[3] if len(blk) > 3 else list(range(0, k.shape[0], KV_CHUNK)) + [k.shape[0]]
            chunks += [(k, vt, post, lo, hi) for lo, hi in zip(cuts[:-1], cuts[1:])]
        steps += [(j, c, ci == len(chunks) - 1) for ci, c in enumerate(chunks)]

    def scores(step):
        j, (k, _, post, lo, hi), _ = step
        kc = k[lo:hi, :]
        s = [_dot_nt(kc, qa[j][a]) for a in range(2)]
        return s if post is None else [post(a, s[a], lo, hi) for a in range(2)]

    outs = [None] * len(jobs)
    m, acc = [None, None], [None, None]
    pending = [scores(st) for st in steps[:SCORES_AHEAD]]
    for n, (j, (_, vt, _, lo, hi), last) in enumerate(steps):
        if n + SCORES_AHEAD < len(steps):
            pending.append(scores(steps[n + SCORES_AHEAD]))
        s_cur = pending.pop(0)
        ones = jnp.ones((ONES_ROWS, hi - lo), BF16)
        for a in range(2):
            vta = jnp.concatenate([vt[a * HEAD_DIM:(a + 1) * HEAD_DIM, lo:hi], ones], axis=0)
            s = s_cur[a]
            mc = jnp.max(s, axis=0, keepdims=True)
            mn = mc if m[a] is None else jnp.maximum(m[a], mc)
            pv = _dot(vta, jnp.exp2(s - mn).astype(BF16))
            acc[a] = pv if m[a] is None else jnp.exp2(m[a] - mn) * acc[a] + pv
            m[a] = mn
        if last:
            sinks, heads = jobs[j][2], []
            for a in range(2):
                l = acc[a][HEAD_DIM:HEAD_DIM + 1, :]
                if sinks[a] is not None:
                    mf = jnp.maximum(m[a], sinks[a])
                    scale = jnp.exp2(m[a] - mf)
                    l = scale * l + jnp.exp2(sinks[a] - mf)
                    heads.append(acc[a][:HEAD_DIM, :] * (scale / l))
                else:
                    heads.append(acc[a][:HEAD_DIM, :] / l)
            outs[j] = jnp.concatenate(heads, axis=0).T
            m, acc = [None, None], [None, None]
    return outs


_NO_SINKS = (None, None)


def _ctx_attn_kernel(sink_ref, qm_ref, km_ref, vmt_ref, qs_ref, ks_ref, vs_ref, qn_ref, kn_ref, vn_ref,
                     om_ref, os_ref, on_ref):
    jobs = []
    for i in range(N_PAIRS):
        q = qm_ref[:, i * MLA_QK_BLK:(i + 1) * MLA_QK_BLK]
        k = km_ref[:, i * MLA_QK_BLK:(i + 1) * MLA_QK_BLK]
        vt = vmt_ref[i * LANES:(i + 1) * LANES, :]
        jobs.append((q, [(k, vt, None)], _NO_SINKS))
    ks, vst = ks_ref[...], _feature_major(vs_ref[...])
    for j in range(2):
        sinks = (sink_ref[j] * LOG2E, sink_ref[j + 2] * LOG2E)
        jobs.append((qs_ref[:, j * LANES:(j + 1) * LANES], [(ks, vst, None)], sinks))
    for j in range(2):
        sl = slice(j * LANES, (j + 1) * LANES)
        jobs.append((qn_ref[:, sl], [(kn_ref[:, sl], _feature_major(vn_ref[:, sl]), None)], _NO_SINKS))
    outs = _attention(jobs)
    for i in range(N_PAIRS):
        om_ref[:, i * LANES:(i + 1) * LANES] = outs[i].astype(BF16)
    for j in range(2):
        os_ref[:, j * LANES:(j + 1) * LANES] = outs[N_PAIRS + j].astype(BF16)
        on_ref[:, j * LANES:(j + 1) * LANES] = outs[N_PAIRS + 2 + j].astype(BF16)


def _context_attention(sink_l, qm, km, vmt, qs, ks, vs, qn, kn, vn):
    row = lambda a: (pl.BlockSpec((a.shape[0], SEQ), lambda b: (0, b)) if a is vmt
                     else pl.BlockSpec((SEQ, a.shape[1]), lambda b: (b, 0)))
    ins = [qm, km, vmt, qs, ks, vs, qn, kn, vn]
    widths = [N_PAIRS * LANES, 256, 256]
    return pl.pallas_call(
        _ctx_attn_kernel,
        grid=(BATCH,),
        in_specs=[pl.BlockSpec(memory_space=pltpu.SMEM)] + [row(a) for a in ins],
        out_specs=[pl.BlockSpec((SEQ, w), lambda b: (b, 0)) for w in widths],
        out_shape=[jax.ShapeDtypeStruct((N_CTX_TOK, w), BF16) for w in widths],
        compiler_params=_cparams(("arbitrary",)),
        name="attn_ctx",
    )(sink_l, *ins)


def _lat_mla_kernel(q_ref, kl_ref, vlt_ref, kc_ref, vct_ref, o_ref):
    blocks = [(kc_ref, vct_ref, None), (kl_ref, vlt_ref, None, list(range(0, DEC_SEQ + 1, MLA_KV_CHUNK)))]
    o_ref[...] = _attention([(q_ref[...], blocks, _NO_SINKS)])[0].astype(BF16)


def _latent_mla(qm, km, vmt, kmc, vmct):
    nq = DEC_SEQ // TQ
    return pl.pallas_call(
        _lat_mla_kernel,
        grid=(DEC_BATCH, N_PAIRS, nq),
        in_specs=[
            pl.BlockSpec((TQ, MLA_QK_BLK), lambda b, i, t: (b * nq + t, i)),
            pl.BlockSpec((DEC_SEQ, MLA_QK_BLK), lambda b, i, t: (b, i)),
            pl.BlockSpec((LANES, DEC_SEQ), lambda b, i, t: (i, b)),
            pl.BlockSpec((PAST_LEN, MLA_QK_BLK), lambda b, i, t: (b, i)),
            pl.BlockSpec((LANES, PAST_LEN), lambda b, i, t: (i, b)),
        ],
        out_specs=pl.BlockSpec((TQ, LANES), lambda b, i, t: (b * nq + t, i)),
        out_shape=jax.ShapeDtypeStruct((N_LAT_TOK, N_PAIRS * LANES), BF16),
        compiler_params=_cparams(("arbitrary", "arbitrary", "arbitrary")),
        name="attn_lat_mla",
    )(qm, km, vmt, kmc, vmct)


def _lat_swa_kernel(sink_ref, q_ref, k_ref, v_ref, kc_ref, vc_ref, o_ref):
    t = pl.program_id(1)
    q0 = t * TQ
    ws = pl.multiple_of(jnp.clip(q0 - SWA_WINDOW, 0, DEC_SEQ - SWA_KWIN), SWA_WINDOW)
    kw = k_ref[pl.ds(ws, SWA_KWIN), :]
    vw = _feature_major(v_ref[pl.ds(ws, SWA_KWIN), :])
    kc = kc_ref[...].astype(BF16)
    vc = _feature_major(vc_ref[...])
    kpos = ws + lax.broadcasted_iota(jnp.int32, (SWA_KWIN, TQ), 0)
    qpos = q0 + lax.broadcasted_iota(jnp.int32, (SWA_KWIN, TQ), 1)
    in_band = jnp.abs(kpos - qpos) <= SWA_WINDOW

    def band(a, s, lo, hi):
        return jnp.where(in_band[lo:hi, :], s, MASK_VALUE)

    jobs = []
    for j in range(2):
        sinks = (sink_ref[j] * LOG2E, sink_ref[j + 2] * LOG2E)
        jobs.append((q_ref[:, j * LANES:(j + 1) * LANES], [(kw, vw, band), (kc, vc, None)], sinks))
    for j, o in enumerate(_attention(jobs)):
        o_ref[:, j * LANES:(j + 1) * LANES] = o.astype(BF16)


def _latent_swa(sink_l, qs, ks, vs, ksc, vsc):
    nq = DEC_SEQ // TQ
    return pl.pallas_call(
        _lat_swa_kernel,
        grid=(DEC_BATCH, nq),
        in_specs=[
            pl.BlockSpec(memory_space=pltpu.SMEM),
            pl.BlockSpec((TQ, 256), lambda b, t: (b * nq + t, 0)),
            pl.BlockSpec((DEC_SEQ, LANES), lambda b, t: (b, 0)),
            pl.BlockSpec((DEC_SEQ, LANES), lambda b, t: (b, 0)),
            pl.BlockSpec((PAST_LEN, LANES), lambda b, t: (b, 0)),
            pl.BlockSpec((PAST_LEN, LANES), lambda b, t: (b, 0)),
        ],
        out_specs=pl.BlockSpec((TQ, 256), lambda b, t: (b * nq + t, 0)),
        out_shape=jax.ShapeDtypeStruct((N_LAT_TOK, 256), BF16),
        compiler_params=_cparams(("arbitrary", "arbitrary")),
        name="attn_lat_swa",
    )(sink_l, qs, ks, vs, ksc, vsc)


_NA_DR = 2 * NA_KH - 1


def _build_na_bias_tiles(rpb_ref, tile_ref):
    kc = lax.broadcasted_iota(jnp.int32, (GRID_W, LANES), 0)
    qc = lax.broadcasted_iota(jnp.int32, (GRID_W, LANES), 1) % GRID_W
    rel = jnp.clip(kc - qc, -(NA_KW - 1), NA_KW - 1) + NA_KW - 1
    col_start = jnp.clip(qc - NA_KW // 2, 0, GRID_W - NA_KW)
    col_ok = (kc >= col_start) & (kc < col_start + NA_KW)
    masked = jnp.full((GRID_W, LANES), MASK_VALUE, F32)
    n_rel = 2 * NA_KW - 1

    def one_tile(hd, carry):
        tile = masked
        for c in range(n_rel):
            tile = jnp.where(col_ok & (rel == c), rpb_ref[hd * n_rel + c] * LOG2E, tile)
        tile_ref[(hd // _NA_DR) * (_NA_DR + 1) + hd % _NA_DR] = tile
        return carry

    lax.fori_loop(0, NA_HEADS * _NA_DR, one_tile, 0)
    for h in range(NA_HEADS):
        tile_ref[h * (_NA_DR + 1) + _NA_DR] = masked


def _na_tile_kinds():
    n_rows = DEC_SEQ // GRID_W
    return (0, NA_ROWS_PER_TILE, n_rows - NA_ROWS_PER_TILE)


def _assemble_na_bias(tile_ref, bias_ref):
    n_rows = DEC_SEQ // GRID_W
    lane_lo = lax.broadcasted_iota(jnp.int32, (GRID_W, LANES), 1) < GRID_W
    for kind, r0 in enumerate(_na_tile_kinds()):
        ws_row = int(np.clip(r0 - NA_KH // 2, 0, n_rows - NA_KEY_ROWS))

        def tile(h, a_q, jj):
            r, kr = r0 + a_q, ws_row + jj
            rs = int(np.clip(r - NA_KH // 2, 0, n_rows - NA_KH))
            return tile_ref[h * (_NA_DR + 1) + (kr - r + NA_KH - 1 if rs <= kr < rs + NA_KH else _NA_DR)]

        for h in range(NA_HEADS):
            for jj in range(NA_KEY_ROWS):
                for u in range(NA_ROWS_PER_TILE // 2):
                    bias_ref[kind * NA_HEADS + h, jj * GRID_W:(jj + 1) * GRID_W, u * LANES:(u + 1) * LANES] = (
                        jnp.where(lane_lo, tile(h, 2 * u, jj), tile(h, 2 * u + 1, jj)))


def _lat_na_kernel(rpb_ref, q_ref, k_ref, v_ref, kc_ref, vc_ref, o_ref, tile_ref, bias_ref):
    t = pl.program_id(1)

    @pl.when((pl.program_id(0) == 0) & (t == 0))
    def _():
        _build_na_bias_tiles(rpb_ref, tile_ref)
        _assemble_na_bias(tile_ref, bias_ref)

    n_rows = DEC_SEQ // GRID_W
    r0 = t * NA_ROWS_PER_TILE
    ws_row = jnp.clip(r0 - NA_KH // 2, 0, n_rows - NA_KEY_ROWS)
    ws = pl.multiple_of(ws_row * GRID_W, GRID_W)
    last = pl.num_programs(1) - 1
    kind = jnp.where(t == 0, 0, jnp.where(t == last, 2, 1))
    jobs = []
    for j in range(2):
        sl = slice(j * LANES, (j + 1) * LANES)
        kw = k_ref[pl.ds(ws, NA_KWIN), sl]
        vw = _feature_major(v_ref[pl.ds(ws, NA_KWIN), sl])
        kc = kc_ref[:, sl].astype(BF16)
        vc = _feature_major(vc_ref[:, sl])

        def add_bias(a, s, lo, hi, j=j):
            return s + bias_ref[kind * NA_HEADS + 2 * j + a, lo:hi, :]

        jobs.append((q_ref[:, sl], [(kw, vw, add_bias), (kc, vc, None)], _NO_SINKS))
    for j, o in enumerate(_attention(jobs)):
        o_ref[:, j * LANES:(j + 1) * LANES] = o.astype(BF16)


def _latent_na(rpb_l, qn, kn, vn, knc, vnc):
    nq = DEC_SEQ // TQ
    return pl.pallas_call(
        _lat_na_kernel,
        grid=(DEC_BATCH, nq),
        in_specs=[
            pl.BlockSpec(memory_space=pltpu.SMEM),
            pl.BlockSpec((TQ, 256), lambda b, t: (b * nq + t, 0)),
            pl.BlockSpec((DEC_SEQ, 256), lambda b, t: (b, 0)),
            pl.BlockSpec((DEC_SEQ, 256), lambda b, t: (b, 0)),
            pl.BlockSpec((PAST_LEN, 256), lambda b, t: (b, 0)),
            pl.BlockSpec((PAST_LEN, 256), lambda b, t: (b, 0)),
        ],
        out_specs=pl.BlockSpec((TQ, 256), lambda b, t: (b * nq + t, 0)),
        out_shape=jax.ShapeDtypeStruct((N_LAT_TOK, 256), BF16),
        scratch_shapes=[pltpu.VMEM((NA_HEADS * (_NA_DR + 1), GRID_W, LANES), F32),
                        pltpu.VMEM((3 * NA_HEADS, NA_KWIN, TQ), F32)],
        compiler_params=_cparams(("arbitrary", "arbitrary")),
        name="attn_lat_na",
    )(rpb_l.reshape(-1), qn, kn, vn, knc, vnc)


_GRP_LANE0 = 0
_EXP_LANE0 = N_GROUPS


def _lane_first_max(x, valid, lane):
    xm = jnp.where(valid, x, -jnp.inf)
    mx = jnp.max(xm, axis=-1, keepdims=True)
    idx = jnp.min(jnp.where(valid & (xm == mx), lane, LANES), axis=-1, keepdims=True)
    return mx, idx


def _tail_kernel(x_ref, om_ref, os_ref, on_ref, mod_ref, n2_ref, wo_ref, wr_ref, x1_ref, h2_ref, gates_ref):
    n_sub = x_ref.shape[0] // TAIL_SUB_ROWS
    subs = [slice(i * TAIL_SUB_ROWS, (i + 1) * TAIL_SUB_ROWS) for i in range(n_sub)]
    wo = wo_ref
    attn = [(_dot(om_ref[r, :], wo[0:512, :]) + _dot(os_ref[r, :], wo[512:768, :]) + _dot(on_ref[r, :], wo[768:1024, :]))
            for r in subs]
    for r, attn_r in zip(subs, attn):
        _tail_rows(r, attn_r, x_ref, mod_ref, n2_ref, wr_ref, x1_ref, h2_ref, gates_ref)


def _tail_rows(r, attn, x_ref, mod_ref, n2_ref, wr_ref, x1_ref, h2_ref, gates_ref):
    m = mod_ref[0]
    x1 = x_ref[r, :] + m[2:3] * attn
    x1_ref[r, :] = x1
    h2 = _rms(x1, n2_ref[...]) * (1.0 + m[4:5]) + m[3:4]
    h2_ref[r, :] = h2.astype(BF16)
    hi = h2.astype(BF16)
    lo = (h2 - hi.astype(F32)).astype(BF16)
    a = _dot(hi, wr_ref[...])
    logits = a[:, :LANES] + a[:, LANES:] + _dot(lo, wr_ref[:, :LANES])
    lane = lax.broadcasted_iota(jnp.int32, logits.shape, 1)
    is_grp = lane < N_GROUPS
    gmax, gidx = _lane_first_max(logits, is_grp, lane)
    gden = jnp.sum(jnp.where(is_grp, jnp.exp(logits - gmax), 0.0), axis=-1, keepdims=True)
    grp_gate = 1.0 / gden
    in_grp = (lane >= _EXP_LANE0) & (lane < _EXP_LANE0 + N_EXPERTS) & ((lane // EXPERTS_PER_GROUP - 1) == gidx)
    v1, i1 = _lane_first_max(logits, in_grp, lane)
    v2, i2 = _lane_first_max(logits, in_grp & (lane != i1), lane)
    e2 = jnp.exp(v2 - v1)
    w1 = grp_gate / (1.0 + e2)
    w2 = grp_gate * e2 / (1.0 + e2)
    gates_ref[r, :] = jnp.where(lane == i1, w1, 0.0) + jnp.where(lane == i2, w2, 0.0)


def _tail(x, om, osw, ona, mod_l, lw, latent):
    n_tok = x.shape[0]
    tm = TM_TAIL
    tiles_per_seq = DEC_SEQ // tm
    mod_idx = (lambda i: (1 + i // tiles_per_seq, 0, 0)) if latent else (lambda i: (0, 0, 0))
    row = lambda w: pl.BlockSpec((tm, w), lambda i: (i, 0))
    whole = lambda a: pl.BlockSpec(a.shape, lambda i: (0,) * a.ndim)
    return pl.pallas_call(
        _tail_kernel,
        grid=(n_tok // tm,),
        in_specs=[row(D_MODEL), row(512), row(256), row(256), pl.BlockSpec((1, N_MOD, D_MODEL), mod_idx),
                  whole(lw["norm2"]), whole(lw["wout"]), whole(lw["wr"])],
        out_specs=[row(D_MODEL), row(D_MODEL), row(LANES)],
        out_shape=[jax.ShapeDtypeStruct((n_tok, D_MODEL), F32), jax.ShapeDtypeStruct((n_tok, D_MODEL), BF16),
                   jax.ShapeDtypeStruct((n_tok, LANES), F32)],
        compiler_params=_cparams(("arbitrary",)),
        name="tail_lat" if latent else "tail_ctx",
    )(x, om, osw, ona, mod_l, lw["norm2"], lw["wout"], lw["wr"])


def _moe_kernel(final, h2_ref, gates_ref, x1_ref, mod_ref, nf_ref, wg_ref, wu_ref, wd_ref, o_ref, acc_ref):
    g = pl.program_id(1)
    h2 = h2_ref[...]
    gates = gates_ref[...]
    lane = lax.broadcasted_iota(jnp.int32, gates.shape, 1)
    contrib = None
    for e in range(EXPERTS_PER_GROUP):
        ge = jnp.sum(jnp.where(lane == _EXP_LANE0 + g * EXPERTS_PER_GROUP + e, gates, 0.0), axis=-1, keepdims=True)
        gu = _dot(h2, jnp.concatenate([wg_ref[0, e], wu_ref[0, e]], axis=1))
        hg, hu = gu[:, :EXPERT_FF], gu[:, EXPERT_FF:]
        act = hg * (1.0 / (1.0 + jnp.exp(-hg))) * hu * ge
        c = _dot(act.astype(BF16), wd_ref[0, e])
        contrib = c if contrib is None else contrib + c

    @pl.when(g == 0)
    def _():
        acc_ref[...] = contrib

    @pl.when((g > 0) & (g < N_GROUPS - 1))
    def _():
        acc_ref[...] += contrib

    @pl.when(g == N_GROUPS - 1)
    def _():
        y = x1_ref[...] + mod_ref[0][5:6] * (acc_ref[...] + contrib)
        if final:
            y = _rms(y, nf_ref[...])
        o_ref[...] = y


def _moe(h2, gates, x1, mod_l, l, moe_w, norm_final, latent, final):
    n_tok = h2.shape[0]
    tm = TM_MOE
    tiles_per_seq = DEC_SEQ // tm
    mod_idx = (lambda i, g: (1 + i // tiles_per_seq, 0, 0)) if latent else (lambda i, g: (0, 0, 0))
    epg = EXPERTS_PER_GROUP
    return pl.pallas_call(
        functools.partial(_moe_kernel, final),
        grid=(n_tok // tm, N_GROUPS),
        in_specs=[
            pl.BlockSpec((tm, D_MODEL), lambda i, g: (i, 0)),
            pl.BlockSpec((tm, LANES), lambda i, g: (i, 0)),
            pl.BlockSpec((tm, D_MODEL), lambda i, g: (i, 0)),
            pl.BlockSpec((1, N_MOD, D_MODEL), mod_idx),
            pl.BlockSpec((1, D_MODEL), lambda i, g: (0, 0)),
            pl.BlockSpec((1, epg, D_MODEL, EXPERT_FF), lambda i, g: (l, g, 0, 0)),
            pl.BlockSpec((1, epg, D_MODEL, EXPERT_FF), lambda i, g: (l, g, 0, 0)),
            pl.BlockSpec((1, epg, EXPERT_FF, D_MODEL), lambda i, g: (l, g, 0, 0)),
        ],
        out_specs=pl.BlockSpec((tm, D_MODEL), lambda i, g: (i, 0)),
        out_shape=jax.ShapeDtypeStruct((n_tok, D_MODEL), F32),
        scratch_shapes=[pltpu.VMEM((tm, D_MODEL), F32)],
        compiler_params=_cparams(("arbitrary", "arbitrary")),
        name=("moe_lat" if latent else "moe_ctx") + ("_final" if final else ""),
    )(h2, gates, x1, mod_l, norm_final, *moe_w)


def _rot_cols(w, d):
    k, n = w.shape
    hh = d // 4
    w4 = w.reshape(k, n // (2 * hh), 2, hh)
    return jnp.stack([-w4[:, :, 1], w4[:, :, 0]], axis=2).reshape(k, n)


def _rot_rows(w, d):
    n, k = w.shape
    hh = d // 4
    w4 = w.reshape(n // (2 * hh), 2, hh, k)
    return jnp.stack([-w4[:, 1], w4[:, 0]], axis=1).reshape(n, k)


def _heads(w, d, order, axis):
    parts = [lax.slice_in_dim(w, h * d, (h + 1) * d, axis=axis) for h in order]
    return jnp.concatenate(parts, axis=axis)


def _layer_weights(l, norm1, norm2, w_in, g_qa, w_uq, g_kva, w_ukv, w_out, w_router_grp, w_router_exp):
    wi = w_in[l].T
    z64 = jnp.zeros((64, D_MODEL), F32)
    cq, ckv, kpe = wi[0:256], wi[256:384], wi[384:416]
    qs = _heads(wi[416:672], HEAD_DIM, (0, 2, 1, 3), 0) * (HEAD_SCALE * LOG2E)
    ks, vs = wi[672:800], wi[800:928]
    qn, kn, vn = wi[928:1184] * (HEAD_SCALE * LOG2E), wi[1184:1440], wi[1440:1696]
    kped = jnp.concatenate([kpe, kpe, z64], axis=0)
    kper = _rot_rows(kpe, MLA_ROPE)
    ctx_rows = [cq, ckv, qs, ks, vs, qn, kn, vn, kped]
    lat_rows = ctx_rows + [_rot_rows(qs, HEAD_DIM), _rot_rows(ks, HEAD_DIM), jnp.concatenate([kper, kper, z64], axis=0)]
    wq = w_uq[l].reshape(MLA_Q_LORA, MLA_HEADS, MLA_NOPE + MLA_ROPE) * (MLA_SCALE * LOG2E)
    nope, ropew = wq[:, :, :MLA_NOPE], wq[:, :, MLA_NOPE:]
    z64q = jnp.zeros((MLA_Q_LORA, 64), F32)
    blocks, rots = [], []
    for i in range(N_PAIRS):
        blocks += [nope[:, 2 * i], nope[:, 2 * i + 1], ropew[:, 2 * i], ropew[:, 2 * i + 1], z64q]
        rots += [_rot_cols(ropew[:, 2 * i], MLA_ROPE), _rot_cols(ropew[:, 2 * i + 1], MLA_ROPE), z64q]
    wkv = w_ukv[l].reshape(MLA_KV_LORA, MLA_HEADS, MLA_NOPE + MLA_V)
    wuk = wkv[:, :, :MLA_NOPE].reshape(MLA_KV_LORA, -1)
    wuvt = wkv[:, :, MLA_NOPE:].reshape(MLA_KV_LORA, -1).T
    wo = w_out[l]
    wout = jnp.concatenate([wo[:512], _heads(wo[512:768], HEAD_DIM, (0, 2, 1, 3), 0), wo[768:]], axis=0)
    wr = jnp.concatenate([w_router_grp[l], w_router_exp[l], jnp.zeros((D_MODEL, LANES - N_GROUPS - N_EXPERTS), F32)],
                         axis=1)
    wr_hi = wr.astype(BF16)
    wr_lo = (wr - wr_hi.astype(F32)).astype(BF16)
    return {
        "norm1": norm1[l][None], "norm2": norm2[l][None], "g_qa": g_qa[l][None], "g_kva": g_kva[l][None],
        "win": jnp.concatenate(lat_rows, axis=0).astype(BF16),
        "wuq": jnp.concatenate(blocks + rots, axis=1).astype(BF16),
        "wuk": wuk.astype(BF16),
        "wuvt": wuvt.astype(BF16),
        "wout": wout.astype(BF16),
        "wr": jnp.concatenate([wr_hi, wr_lo], axis=1),
    }


def _rope_tables():
    n_rows = DEC_SEQ // GRID_W
    lane = np.arange(LANES)

    def tab(d, used_lanes):
        hh = d // 4
        i = lane % d
        freq = ROPE_THETA ** (-jnp.asarray(i % hh, F32) / hh)
        by_row = jnp.asarray((i // (2 * hh)) == 0)[None, None, :]
        valid = jnp.asarray(lane < used_lanes)[None, None, :]
        ang_r = jnp.arange(n_rows, dtype=F32)[:, None] * freq[None, :]
        ang_c = jnp.arange(GRID_W, dtype=F32)[:, None] * freq[None, :]

        def expand(fn):
            t = jnp.where(by_row, fn(ang_r)[:, None, :], fn(ang_c)[None, :, :])
            return jnp.where(valid, t, 0.0).reshape(DEC_SEQ, LANES)

        return expand(jnp.cos), expand(jnp.sin)

    c64, s64 = tab(HEAD_DIM, LANES)
    c32, s32 = tab(MLA_ROPE, 2 * MLA_ROPE)
    return c64, s64, c32, s32


def kernel(x_prompt, x_sample, cache_mla_ckv, cache_mla_kpe, cache_swa_k, cache_swa_v, cache_na_k, cache_na_v, c, c_ctx, w_mod, b_mod, norm1, norm2, w_in, g_qa, w_uq, g_kva, w_ukv, swa_sink, na_rpb, w_out, w_router_grp, w_router_exp, w_gate, w_up, w_down, norm_final):
    cpad = jnp.concatenate([c_ctx[None], c, jnp.zeros((8 - 1 - DEC_BATCH, D_MODEL), F32)], axis=0)
    mod = _modulation(cpad, w_mod, b_mod).reshape(DEPTH, 8, N_MOD, D_MODEL)
    tabs = _rope_tables()
    nf = norm_final[None]
    xp = x_prompt.reshape(N_CTX_TOK, D_MODEL)
    xs = x_sample.reshape(N_LAT_TOK, D_MODEL)
    caches = [[] for _ in range(6)]
    moe_w = (w_gate.astype(BF16), w_up.astype(BF16), w_down.astype(BF16))
    for l in range(DEPTH):
        lw = _layer_weights(l, norm1, norm2, w_in, g_qa, w_uq, g_kva, w_ukv, w_out, w_router_grp, w_router_exp)
        final = l == DEPTH - 1
        outs = _projections(xp, mod[l], lw, None, rope=False)
        om, osw, ona = _context_attention(swa_sink[l], *outs[:9])
        for dst, a in zip(caches, outs[9:]):
            dst.append(a)
        x1, h2, gates = _tail(xp, om, osw, ona, mod[l], lw, latent=False)
        xp = _moe(h2, gates, x1, mod[l], l, moe_w, nf, latent=False, final=final)
        qm, km, vm, qs, ks, vs, qn, kn, vn = _projections(xs, mod[l], lw, tabs, rope=True)
        kpe_c = cache_mla_kpe[:, l].reshape(DEC_BATCH * PAST_LEN, MLA_ROPE)
        kpe_dup = jnp.concatenate([kpe_c, kpe_c, jnp.zeros((DEC_BATCH * PAST_LEN, 64), F32)], axis=1)
        kmc, vmc = _expand_cached_mla(cache_mla_ckv[:, l].reshape(DEC_BATCH * PAST_LEN, MLA_KV_LORA), kpe_dup,
                                      lw["wuk"], lw["wuvt"])
        om = _latent_mla(qm, km, vm, kmc, vmc)
        flat = lambda a: a[:, l].reshape(DEC_BATCH * PAST_LEN, -1)
        osw = _latent_swa(swa_sink[l], qs, ks, vs, flat(cache_swa_k), flat(cache_swa_v))
        ona = _latent_na(na_rpb[l], qn, kn, vn, flat(cache_na_k), flat(cache_na_v))
        x1, h2, gates = _tail(xs, om, osw, ona, mod[l], lw, latent=True)
        xs = _moe(h2, gates, x1, mod[l], l, moe_w, nf, latent=True, final=final)
    def heads_last(parts, n_heads):
        a = jnp.stack(parts, axis=1)
        if n_heads is None:
            return a.transpose(0, 1, 3, 2)
        return a.reshape(BATCH, DEPTH, n_heads, -1, SEQ).transpose(0, 1, 4, 2, 3)

    return (xp.reshape(BATCH, SEQ, D_MODEL), xs.reshape(DEC_BATCH, DEC_SEQ, D_MODEL),
            jnp.stack([p.reshape(BATCH, SEQ, MLA_KV_LORA) for p in caches[0]], axis=1), heads_last(caches[1], None),
            heads_last(caches[2], SWA_KV_HEADS), heads_last(caches[3], SWA_KV_HEADS),
            heads_last(caches[4], NA_HEADS), heads_last(caches[5], NA_HEADS))
```

```python
import functools

import jax
import jax.numpy as jnp
import numpy as np
from jax import lax
from jax.experimental import pallas as pl
from jax.experimental.pallas import tpu as pltpu

D_MODEL = 1024
BATCH = 32
SEQ = 256
DEPTH = 2
DEC_BATCH = 2
DEC_SEQ = 4096
PAST_LEN = 512
GRID_W = 64
HEAD_DIM = 64
MLA_HEADS = 8
MLA_Q_LORA = 256
MLA_KV_LORA = 128
MLA_NOPE = 64
MLA_ROPE = 32
MLA_V = 64
SWA_HEADS = 4
SWA_KV_HEADS = 2
SWA_WINDOW = 128
NA_HEADS = 4
NA_KH = 8
NA_KW = 16
N_GROUPS = 4
EXPERTS_PER_GROUP = 4
N_EXPERTS = 16
EXPERT_FF = 256
N_MOD = 6
ROPE_THETA = 10000.0
EPS = 1e-6
MASK_VALUE = -1e30
MLA_SCALE = (MLA_NOPE + MLA_ROPE) ** -0.5
HEAD_SCALE = HEAD_DIM ** -0.5
LOG2E = 1.4426950408889634

LANES = 128
N_PAIRS = MLA_HEADS // 2
MLA_QK_BLK = 2 * LANES
N_CTX_TOK = BATCH * SEQ
N_LAT_TOK = DEC_BATCH * DEC_SEQ

_C_CQ, _C_CKV, _C_QS, _C_KS, _C_VS, _C_QN, _C_KN, _C_VN, _C_KPE, _C_END = (
    0, 256, 384, 640, 768, 896, 1152, 1408, 1664, 1792)

TM_PROJ = 512
TM_TAIL = 512
TAIL_SUB_ROWS = 128
TM_MOE = 1024
TQ = 256
TQ_MLA = 512
KV_CHUNK = 2048
MLA_KV_CHUNK = 512
SCORES_AHEAD = 2
ONES_ROWS = 16
SWA_KWIN = TQ + 2 * SWA_WINDOW
NA_ROWS_PER_TILE = TQ // GRID_W
NA_KEY_ROWS = 12
NA_KWIN = NA_KEY_ROWS * GRID_W
VMEM_LIMIT = 56 * 1024 * 1024

F32 = jnp.float32
BF16 = jnp.bfloat16


def _dot(a, b):
    return jnp.dot(a, b, preferred_element_type=F32)


def _dot_nt(a, b):
    return lax.dot_general(a, b, (((1,), (1,)), ((), ())), preferred_element_type=F32)


def _rms(x, g):
    return x * lax.rsqrt(jnp.mean(x * x, axis=-1, keepdims=True) + EPS) * g


def _cparams(sem):
    return pltpu.CompilerParams(dimension_semantics=sem, vmem_limit_bytes=VMEM_LIMIT)


def _mod_kernel(c_ref, w_ref, b_ref, o_ref):
    c = c_ref[...]
    s = c * (1.0 / (1.0 + jnp.exp(-c)))
    o_ref[0] = jnp.dot(s, w_ref[0], preferred_element_type=F32, precision=lax.Precision.HIGHEST) + b_ref[0]


def _modulation(cpad, w_mod, b_mod):
    nt = 1024
    return pl.pallas_call(
        _mod_kernel,
        grid=(DEPTH, N_MOD * D_MODEL // nt),
        in_specs=[
            pl.BlockSpec((8, D_MODEL), lambda l, n: (0, 0)),
            pl.BlockSpec((1, D_MODEL, nt), lambda l, n: (l, 0, n)),
            pl.BlockSpec((1, 1, nt), lambda l, n: (l, 0, n)),
        ],
        out_specs=pl.BlockSpec((1, 8, nt), lambda l, n: (l, 0, n)),
        out_shape=jax.ShapeDtypeStruct((DEPTH, 8, N_MOD * D_MODEL), F32),
        compiler_params=_cparams(("arbitrary", "arbitrary")),
        name="modulation",
    )(cpad, w_mod, b_mod.reshape(DEPTH, 1, N_MOD * D_MODEL))


def _rope(x, d, cos, sin):
    hh = d // 4
    lane = lax.broadcasted_iota(jnp.int32, x.shape, 1)
    first_half = (lane % (2 * hh)) < hh
    rot = jnp.where(first_half, -pltpu.roll(x, LANES - hh, 1), pltpu.roll(x, hh, 1))
    return x * cos + rot * sin


def _proj_kernel(rope, *refs):
    if rope:
        (x_ref, mod_ref, n1_ref, win_ref, gqa_ref, wuq_ref, gkva_ref, wuk_ref, wuvt_ref,
         c64_ref, s64_ref, c32_ref, s32_ref,
         qm_ref, km_ref, vmt_ref, qs_ref, ks_ref, vs_ref, qn_ref, kn_ref, vn_ref) = refs
    else:
        (x_ref, mod_ref, n1_ref, win_ref, gqa_ref, wuq_ref, gkva_ref, wuk_ref, wuvt_ref,
         qm_ref, km_ref, vmt_ref, qs_ref, ks_ref, vs_ref, qn_ref, kn_ref, vn_ref,
         ckv_o, kpe_o, ks_o, vs_o, kn_o, vn_o) = refs
    m = mod_ref[0]
    x_ref, n1_ref, win_ref = refs[0], refs[2], refs[3]
    n_sub = x_ref.shape[0] // SEQ
    ps = []
    for sub in range(n_sub):
        h = _rms(x_ref[sub * SEQ:(sub + 1) * SEQ, :], n1_ref[...]) * (1.0 + m[1:2]) + m[0:1]
        ps.append(_dot_nt(h.astype(BF16), win_ref[...]))
    for sub in range(n_sub):
        _proj_rows(sub, rope, ps[sub], refs)


def _proj_rows(sub, rope, p, refs):
    if rope:
        (x_ref, mod_ref, n1_ref, win_ref, gqa_ref, wuq_ref, gkva_ref, wuk_ref, wuvt_ref,
         c64_ref, s64_ref, c32_ref, s32_ref,
         qm_ref, km_ref, vmt_ref, qs_ref, ks_ref, vs_ref, qn_ref, kn_ref, vn_ref) = refs
    else:
        (x_ref, mod_ref, n1_ref, win_ref, gqa_ref, wuq_ref, gkva_ref, wuk_ref, wuvt_ref,
         qm_ref, km_ref, vmt_ref, qs_ref, ks_ref, vs_ref, qn_ref, kn_ref, vn_ref,
         ckv_o, kpe_o, ks_o, vs_o, kn_o, vn_o) = refs
    r = slice(sub * SEQ, (sub + 1) * SEQ)
    qm = _dot(_rms(p[:, _C_CQ:_C_CKV], gqa_ref[...]).astype(BF16), wuq_ref[...])
    ckv = _rms(p[:, _C_CKV:_C_QS], gkva_ref[...])
    ckv_b = ckv.astype(BF16)
    kn_mla = _dot(ckv_b, wuk_ref[...])
    vmt_ref[:, r] = _dot_nt(wuvt_ref[...], ckv_b).astype(BF16)
    qs = p[:, _C_QS:_C_KS]
    ks = p[:, _C_KS:_C_VS]
    kpe = p[:, _C_KPE:_C_END]
    if rope:
        c64, s64, c32, s32 = c64_ref[r, :], s64_ref[r, :], c32_ref[r, :], s32_ref[r, :]
        qs = jnp.concatenate([_rope(qs[:, j * LANES:(j + 1) * LANES], HEAD_DIM, c64, s64) for j in range(2)], axis=1)
        ks = _rope(ks, HEAD_DIM, c64, s64)
        kpe = _rope(kpe, MLA_ROPE, c32, s32)
    for i in range(N_PAIRS):
        lo = i * MLA_QK_BLK
        qrope = qm[:, lo + LANES:lo + MLA_QK_BLK]
        if rope:
            qrope = _rope(qrope, MLA_ROPE, c32, s32)
        qm_ref[r, lo:lo + LANES] = qm[:, lo:lo + LANES].astype(BF16)
        qm_ref[r, lo + LANES:lo + MLA_QK_BLK] = qrope.astype(BF16)
        km_ref[r, lo:lo + LANES] = kn_mla[:, i * LANES:(i + 1) * LANES].astype(BF16)
        km_ref[r, lo + LANES:lo + MLA_QK_BLK] = kpe.astype(BF16)
    qs_ref[r, :] = qs.astype(BF16)
    ks_ref[r, :] = ks.astype(BF16)
    vs_ref[r, :] = p[:, _C_VS:_C_QN].astype(BF16)
    qn_ref[r, :] = p[:, _C_QN:_C_KN].astype(BF16)
    kn_ref[r, :] = p[:, _C_KN:_C_VN].astype(BF16)
    vn_ref[r, :] = p[:, _C_VN:_C_KPE].astype(BF16)
    if not rope:
        ckv_o[r, :] = ckv

        def put_feature_major(o_ref, val, n_feat):
            o_ref[sub] = val.T[:n_feat, :]

        put_feature_major(kpe_o, kpe, MLA_ROPE)
        put_feature_major(ks_o, ks, 128)
        put_feature_major(vs_o, p[:, _C_VS:_C_QN], 128)
        put_feature_major(kn_o, p[:, _C_KN:_C_VN], 256)
        put_feature_major(vn_o, p[:, _C_VN:_C_KPE], 256)


def _projections(x, mod_l, lw, tabs, rope):
    n_tok = x.shape[0]
    tm = TM_PROJ
    tiles_per_seq = DEC_SEQ // tm
    win, wuq = lw["win"], lw["wuq"]
    mod_idx = (lambda i: (1 + i // tiles_per_seq, 0, 0)) if rope else (lambda i: (0, 0, 0))
    row = lambda w: pl.BlockSpec((tm, w), lambda i: (i, 0))
    whole = lambda a: pl.BlockSpec(a.shape, lambda i: (0,) * a.ndim)
    in_specs = [row(D_MODEL), pl.BlockSpec((1, N_MOD, D_MODEL), mod_idx), whole(lw["norm1"]), whole(win),
                whole(lw["g_qa"]), whole(wuq), whole(lw["g_kva"]), whole(lw["wuk"]), whole(lw["wuvt"])]
    args = [x, mod_l, lw["norm1"], win, lw["g_qa"], wuq, lw["g_kva"], lw["wuk"], lw["wuvt"]]
    widths = [N_PAIRS * MLA_QK_BLK, N_PAIRS * MLA_QK_BLK, None, 256, 128, 128, 256, 256, 256]
    vmt_spec = pl.BlockSpec((N_PAIRS * LANES, tm), lambda i: (0, i))
    out_specs = [vmt_spec if w is None else row(w) for w in widths]
    out_shape = [jax.ShapeDtypeStruct((N_PAIRS * LANES, n_tok) if w is None else (n_tok, w), BF16) for w in widths]
    if rope:
        tab_spec = pl.BlockSpec((tm, LANES), lambda i: (i % tiles_per_seq, 0))
        in_specs += [tab_spec] * 4
        args += list(tabs)
    else:
        out_specs.append(row(MLA_KV_LORA))
        out_shape.append(jax.ShapeDtypeStruct((n_tok, MLA_KV_LORA), F32))
        for w in [MLA_ROPE, 128, 128, 256, 256]:
            out_specs.append(pl.BlockSpec((tm // SEQ, w, SEQ), lambda i: (i, 0, 0)))
            out_shape.append(jax.ShapeDtypeStruct((n_tok // SEQ, w, SEQ), F32))
    return pl.pallas_call(
        functools.partial(_proj_kernel, rope),
        grid=(n_tok // tm,),
        in_specs=in_specs,
        out_specs=out_specs,
        out_shape=out_shape,
        compiler_params=_cparams(("arbitrary",)),
        name="proj_lat" if rope else "proj_ctx",
    )(*args)


def _ctxkv_kernel(ckv_ref, kpe_ref, wuk_ref, wuvt_ref, km_ref, vmt_ref):
    ckv_b = ckv_ref[...].astype(BF16)
    kn_mla = _dot(ckv_b, wuk_ref[...])
    kpe = kpe_ref[...].astype(BF16)
    for i in range(N_PAIRS):
        lo = i * MLA_QK_BLK
        km_ref[:, lo:lo + LANES] = kn_mla[:, i * LANES:(i + 1) * LANES].astype(BF16)
        km_ref[:, lo + LANES:lo + MLA_QK_BLK] = kpe
    vmt_ref[...] = _dot_nt(wuvt_ref[...], ckv_b).astype(BF16)


def _expand_cached_mla(ckv_c, kpe_dup, wuk, wuvt):
    n = ckv_c.shape[0]
    whole = lambda a: pl.BlockSpec(a.shape, lambda i: (0,) * a.ndim)
    return pl.pallas_call(
        _ctxkv_kernel,
        grid=(1,),
        in_specs=[whole(ckv_c), whole(kpe_dup), whole(wuk), whole(wuvt)],
        out_specs=[pl.BlockSpec((n, N_PAIRS * MLA_QK_BLK), lambda i: (0, 0)),
                   pl.BlockSpec((N_PAIRS * LANES, n), lambda i: (0, 0))],
        out_shape=[jax.ShapeDtypeStruct((n, N_PAIRS * MLA_QK_BLK), BF16),
                   jax.ShapeDtypeStruct((N_PAIRS * LANES, n), BF16)],
        compiler_params=_cparams(("arbitrary",)),
        name="expand_cached_mla",
    )(ckv_c, kpe_dup, wuk, wuvt)


def _feature_major(v):
    return v.astype(F32).T.astype(BF16)


def _pair_masks(width):
    lane = lax.broadcasted_iota(jnp.int32, (1, width), 1)
    if width == LANES:
        return [lane < HEAD_DIM, lane >= HEAD_DIM]
    m0 = (lane < MLA_NOPE) | ((lane >= LANES) & (lane < LANES + MLA_ROPE))
    m1 = ((lane >= MLA_NOPE) & (lane < LANES)) | ((lane >= LANES + MLA_ROPE) & (lane < LANES + 2 * MLA_ROPE))
    return [m0, m1]


def _attention(jobs):
    steps, qa = [], []
    for j, (q, blocks, _) in enumerate(jobs):
        masks = _pair_masks(q.shape[1])
        qa.append([jnp.where(masks[a], q, jnp.zeros_like(q)) for a in range(2)])
        chunks = []
        for blk in blocks:
            k, vt, post = blk[:3]
            cuts = blk[3] if len(blk) > 3 else list(range(0, k.shape[0], KV_CHUNK)) + [k.shape[0]]
            chunks += [(k, vt, post, lo, hi) for lo, hi in zip(cuts[:-1], cuts[1:])]
        steps += [(j, c, ci == len(chunks) - 1) for ci, c in enumerate(chunks)]

    def scores(step):
        j, (k, _, post, lo, hi), _ = step
        kc = k[lo:hi, :]
        s = [_dot_nt(kc, qa[j][a]) for a in range(2)]
        return s if post is None else [post(a, s[a], lo, hi) for a in range(2)]

    outs = [None] * len(jobs)
    m, acc = [None, None], [None, None]
    pending = [scores(st) for st in steps[:SCORES_AHEAD]]
    for n, (j, (_, vt, _, lo, hi), last) in enumerate(steps):
        if n + SCORES_AHEAD < len(steps):
            pending.append(scores(steps[n + SCORES_AHEAD]))
        s_cur = pending.pop(0)
        ones = jnp.ones((ONES_ROWS, hi - lo), BF16)
        for a in range(2):
            vta = jnp.concatenate([vt[a * HEAD_DIM:(a + 1) * HEAD_DIM, lo:hi], ones], axis=0)
            s = s_cur[a]
            mc = jnp.max(s, axis=0, keepdims=True)
            mn = mc if m[a] is None else jnp.maximum(m[a], mc)
            pv = _dot(vta, jnp.exp2(s - mn).astype(BF16))
            acc[a] = pv if m[a] is None else jnp.exp2(m[a] - mn) * acc[a] + pv
            m[a] = mn
        if last:
            sinks, heads = jobs[j][2], []
            for a in range(2):
                l = acc[a][HEAD_DIM:HEAD_DIM + 1, :]
                if sinks[a] is not None:
                    mf = jnp.maximum(m[a], sinks[a])
                    scale = jnp.exp2(m[a] - mf)
                    l = scale * l + jnp.exp2(sinks[a] - mf)
                    heads.append(acc[a][:HEAD_DIM, :] * (scale / l))
                else:
                    heads.append(acc[a][:HEAD_DIM, :] / l)
            outs[j] = jnp.concatenate(heads, axis=0).T
            m, acc = [None, None], [None, None]
    return outs


_NO_SINKS = (None, None)


def _ctx_attn_kernel(sink_ref, qm_ref, km_ref, vmt_ref, qs_ref, ks_ref, vs_ref, qn_ref, kn_ref, vn_ref,
                     om_ref, os_ref, on_ref):
    jobs = []
    for i in range(N_PAIRS):
        q = qm_ref[:, i * MLA_QK_BLK:(i + 1) * MLA_QK_BLK]
        k = km_ref[:, i * MLA_QK_BLK:(i + 1) * MLA_QK_BLK]
        vt = vmt_ref[i * LANES:(i + 1) * LANES, :]
        jobs.append((q, [(k, vt, None)], _NO_SINKS))
    ks, vst = ks_ref[...], _feature_major(vs_ref[...])
    for j in range(2):
        sinks = (sink_ref[j] * LOG2E, sink_ref[j + 2] * LOG2E)
        jobs.append((qs_ref[:, j * LANES:(j + 1) * LANES], [(ks, vst, None)], sinks))
    for j in range(2):
        sl = slice(j * LANES, (j + 1) * LANES)
        jobs.append((qn_ref[:, sl], [(kn_ref[:, sl], _feature_major(vn_ref[:, sl]), None)], _NO_SINKS))
    outs = _attention(jobs)
    for i in range(N_PAIRS):
        om_ref[:, i * LANES:(i + 1) * LANES] = outs[i].astype(BF16)
    for j in range(2):
        os_ref[:, j * LANES:(j + 1) * LANES] = outs[N_PAIRS + j].astype(BF16)
        on_ref[:, j * LANES:(j + 1) * LANES] = outs[N_PAIRS + 2 + j].astype(BF16)


def _context_attention(sink_l, qm, km, vmt, qs, ks, vs, qn, kn, vn):
    row = lambda a: (pl.BlockSpec((a.shape[0], SEQ), lambda b: (0, b)) if a is vmt
                     else pl.BlockSpec((SEQ, a.shape[1]), lambda b: (b, 0)))
    ins = [qm, km, vmt, qs, ks, vs, qn, kn, vn]
    widths = [N_PAIRS * LANES, 256, 256]
    return pl.pallas_call(
        _ctx_attn_kernel,
        grid=(BATCH,),
        in_specs=[pl.BlockSpec(memory_space=pltpu.SMEM)] + [row(a) for a in ins],
        out_specs=[pl.BlockSpec((SEQ, w), lambda b: (b, 0)) for w in widths],
        out_shape=[jax.ShapeDtypeStruct((N_CTX_TOK, w), BF16) for w in widths],
        compiler_params=_cparams(("arbitrary",)),
        name="attn_ctx",
    )(sink_l, *ins)


def _lat_mla_kernel(q_ref, kl_ref, vlt_ref, kc_ref, vct_ref, o_ref):
    blocks = [(kc_ref, vct_ref, None), (kl_ref, vlt_ref, None, list(range(0, DEC_SEQ + 1, MLA_KV_CHUNK)))]
    o_ref[...] = _attention([(q_ref[...], blocks, _NO_SINKS)])[0].astype(BF16)


def _latent_mla(qm, km, vmt, kmc, vmct):
    nq = DEC_SEQ // TQ_MLA
    return pl.pallas_call(
        _lat_mla_kernel,
        grid=(DEC_BATCH, N_PAIRS, nq),
        in_specs=[
            pl.BlockSpec((TQ_MLA, MLA_QK_BLK), lambda b, i, t: (b * nq + t, i)),
            pl.BlockSpec((DEC_SEQ, MLA_QK_BLK), lambda b, i, t: (b, i)),
            pl.BlockSpec((LANES, DEC_SEQ), lambda b, i, t: (i, b)),
            pl.BlockSpec((PAST_LEN, MLA_QK_BLK), lambda b, i, t: (b, i)),
            pl.BlockSpec((LANES, PAST_LEN), lambda b, i, t: (i, b)),
        ],
        out_specs=pl.BlockSpec((TQ_MLA, LANES), lambda b, i, t: (b * nq + t, i)),
        out_shape=jax.ShapeDtypeStruct((N_LAT_TOK, N_PAIRS * LANES), BF16),
        compiler_params=_cparams(("arbitrary", "arbitrary", "arbitrary")),
        name="attn_lat_mla",
    )(qm, km, vmt, kmc, vmct)


def _lat_swa_kernel(sink_ref, q_ref, k_ref, v_ref, kc_ref, vc_ref, o_ref):
    t = pl.program_id(1)
    q0 = t * TQ
    ws = pl.multiple_of(jnp.clip(q0 - SWA_WINDOW, 0, DEC_SEQ - SWA_KWIN), SWA_WINDOW)
    kw = k_ref[pl.ds(ws, SWA_KWIN), :]
    vw = _feature_major(v_ref[pl.ds(ws, SWA_KWIN), :])
    kc = kc_ref[...].astype(BF16)
    vc = _feature_major(vc_ref[...])
    kpos = ws + lax.broadcasted_iota(jnp.int32, (SWA_KWIN, TQ), 0)
    qpos = q0 + lax.broadcasted_iota(jnp.int32, (SWA_KWIN, TQ), 1)
    in_band = jnp.abs(kpos - qpos) <= SWA_WINDOW

    def band(a, s, lo, hi):
        return jnp.where(in_band[lo:hi, :], s, MASK_VALUE)

    jobs = []
    for j in range(2):
        sinks = (sink_ref[j] * LOG2E, sink_ref[j + 2] * LOG2E)
        jobs.append((q_ref[:, j * LANES:(j + 1) * LANES], [(kw, vw, band), (kc, vc, None)], sinks))
    for j, o in enumerate(_attention(jobs)):
        o_ref[:, j * LANES:(j + 1) * LANES] = o.astype(BF16)


def _latent_swa(sink_l, qs, ks, vs, ksc, vsc):
    nq = DEC_SEQ // TQ
    return pl.pallas_call(
        _lat_swa_kernel,
        grid=(DEC_BATCH, nq),
        in_specs=[
            pl.BlockSpec(memory_space=pltpu.SMEM),
            pl.BlockSpec((TQ, 256), lambda b, t: (b * nq + t, 0)),
            pl.BlockSpec((DEC_SEQ, LANES), lambda b, t: (b, 0)),
            pl.BlockSpec((DEC_SEQ, LANES), lambda b, t: (b, 0)),
            pl.BlockSpec((PAST_LEN, LANES), lambda b, t: (b, 0)),
            pl.BlockSpec((PAST_LEN, LANES), lambda b, t: (b, 0)),
        ],
        out_specs=pl.BlockSpec((TQ, 256), lambda b, t: (b * nq + t, 0)),
        out_shape=jax.ShapeDtypeStruct((N_LAT_TOK, 256), BF16),
        compiler_params=_cparams(("arbitrary", "arbitrary")),
        name="attn_lat_swa",
    )(sink_l, qs, ks, vs, ksc, vsc)


_NA_DR = 2 * NA_KH - 1


def _build_na_bias_tiles(rpb_ref, tile_ref):
    kc = lax.broadcasted_iota(jnp.int32, (GRID_W, LANES), 0)
    qc = lax.broadcasted_iota(jnp.int32, (GRID_W, LANES), 1) % GRID_W
    rel = jnp.clip(kc - qc, -(NA_KW - 1), NA_KW - 1) + NA_KW - 1
    col_start = jnp.clip(qc - NA_KW // 2, 0, GRID_W - NA_KW)
    col_ok = (kc >= col_start) & (kc < col_start + NA_KW)
    masked = jnp.full((GRID_W, LANES), MASK_VALUE, F32)
    n_rel = 2 * NA_KW - 1

    def one_tile(hd, carry):
        tile = masked
        for c in range(n_rel):
            tile = jnp.where(col_ok & (rel == c), rpb_ref[hd * n_rel + c] * LOG2E, tile)
        tile_ref[(hd // _NA_DR) * (_NA_DR + 1) + hd % _NA_DR] = tile
        return carry

    lax.fori_loop(0, NA_HEADS * _NA_DR, one_tile, 0)
    for h in range(NA_HEADS):
        tile_ref[h * (_NA_DR + 1) + _NA_DR] = masked


def _na_tile_kinds():
    n_rows = DEC_SEQ // GRID_W
    return (0, NA_ROWS_PER_TILE, n_rows - NA_ROWS_PER_TILE)


def _assemble_na_bias(tile_ref, bias_ref):
    n_rows = DEC_SEQ // GRID_W
    lane_lo = lax.broadcasted_iota(jnp.int32, (GRID_W, LANES), 1) < GRID_W
    for kind, r0 in enumerate(_na_tile_kinds()):
        ws_row = int(np.clip(r0 - NA_KH // 2, 0, n_rows - NA_KEY_ROWS))

        def tile(h, a_q, jj):
            r, kr = r0 + a_q, ws_row + jj
            rs = int(np.clip(r - NA_KH // 2, 0, n_rows - NA_KH))
            return tile_ref[h * (_NA_DR + 1) + (kr - r + NA_KH - 1 if rs <= kr < rs + NA_KH else _NA_DR)]

        for h in range(NA_HEADS):
            for jj in range(NA_KEY_ROWS):
                for u in range(NA_ROWS_PER_TILE // 2):
                    bias_ref[kind * NA_HEADS + h, jj * GRID_W:(jj + 1) * GRID_W, u * LANES:(u + 1) * LANES] = (
                        jnp.where(lane_lo, tile(h, 2 * u, jj), tile(h, 2 * u + 1, jj)))


def _lat_na_kernel(rpb_ref, q_ref, k_ref, v_ref, kc_ref, vc_ref, o_ref, tile_ref, bias_ref):
    t = pl.program_id(1)

    @pl.when((pl.program_id(0) == 0) & (t == 0))
    def _():
        _build_na_bias_tiles(rpb_ref, tile_ref)
        _assemble_na_bias(tile_ref, bias_ref)

    n_rows = DEC_SEQ // GRID_W
    r0 = t * NA_ROWS_PER_TILE
    ws_row = jnp.clip(r0 - NA_KH // 2, 0, n_rows - NA_KEY_ROWS)
    ws = pl.multiple_of(ws_row * GRID_W, GRID_W)
    last = pl.num_programs(1) - 1
    kind = jnp.where(t == 0, 0, jnp.where(t == last, 2, 1))
    jobs = []
    for j in range(2):
        sl = slice(j * LANES, (j + 1) * LANES)
        kw = k_ref[pl.ds(ws, NA_KWIN), sl]
        vw = _feature_major(v_ref[pl.ds(ws, NA_KWIN), sl])
        kc = kc_ref[:, sl].astype(BF16)
        vc = _feature_major(vc_ref[:, sl])

        def add_bias(a, s, lo, hi, j=j):
            return s + bias_ref[kind * NA_HEADS + 2 * j + a, lo:hi, :]

        jobs.append((q_ref[:, sl], [(kw, vw, add_bias), (kc, vc, None)], _NO_SINKS))
    for j, o in enumerate(_attention(jobs)):
        o_ref[:, j * LANES:(j + 1) * LANES] = o.astype(BF16)


def _latent_na(rpb_l, qn, kn, vn, knc, vnc):
    nq = DEC_SEQ // TQ
    return pl.pallas_call(
        _lat_na_kernel,
        grid=(DEC_BATCH, nq),
        in_specs=[
            pl.BlockSpec(memory_space=pltpu.SMEM),
            pl.BlockSpec((TQ, 256), lambda b, t: (b * nq + t, 0)),
            pl.BlockSpec((DEC_SEQ, 256), lambda b, t: (b, 0)),
            pl.BlockSpec((DEC_SEQ, 256), lambda b, t: (b, 0)),
            pl.BlockSpec((PAST_LEN, 256), lambda b, t: (b, 0)),
            pl.BlockSpec((PAST_LEN, 256), lambda b, t: (b, 0)),
        ],
        out_specs=pl.BlockSpec((TQ, 256), lambda b, t: (b * nq + t, 0)),
        out_shape=jax.ShapeDtypeStruct((N_LAT_TOK, 256), BF16),
        scratch_shapes=[pltpu.VMEM((NA_HEADS * (_NA_DR + 1), GRID_W, LANES), F32),
                        pltpu.VMEM((3 * NA_HEADS, NA_KWIN, TQ), F32)],
        compiler_params=_cparams(("arbitrary", "arbitrary")),
        name="attn_lat_na",
    )(rpb_l.reshape(-1), qn, kn, vn, knc, vnc)


_GRP_LANE0 = 0
_EXP_LANE0 = N_GROUPS


def _lane_first_max(x, valid, lane):
    xm = jnp.where(valid, x, -jnp.inf)
    mx = jnp.max(xm, axis=-1, keepdims=True)
    idx = jnp.min(jnp.where(valid & (xm == mx), lane, LANES), axis=-1, keepdims=True)
    return mx, idx


def _tail_kernel(x_ref, om_ref, os_ref, on_ref, mod_ref, n2_ref, wo_ref, wr_ref, x1_ref, h2_ref, gates_ref):
    n_sub = x_ref.shape[0] // TAIL_SUB_ROWS
    subs = [slice(i * TAIL_SUB_ROWS, (i + 1) * TAIL_SUB_ROWS) for i in range(n_sub)]
    wo = wo_ref
    attn = [(_dot(om_ref[r, :], wo[0:512, :]) + _dot(os_ref[r, :], wo[512:768, :]) + _dot(on_ref[r, :], wo[768:1024, :]))
            for r in subs]
    for r, attn_r in zip(subs, attn):
        _tail_rows(r, attn_r, x_ref, mod_ref, n2_ref, wr_ref, x1_ref, h2_ref, gates_ref)


def _tail_rows(r, attn, x_ref, mod_ref, n2_ref, wr_ref, x1_ref, h2_ref, gates_ref):
    m = mod_ref[0]
    x1 = x_ref[r, :] + m[2:3] * attn
    x1_ref[r, :] = x1
    h2 = _rms(x1, n2_ref[...]) * (1.0 + m[4:5]) + m[3:4]
    h2_ref[r, :] = h2.astype(BF16)
    hi = h2.astype(BF16)
    lo = (h2 - hi.astype(F32)).astype(BF16)
    a = _dot(hi, wr_ref[...])
    logits = a[:, :LANES] + a[:, LANES:] + _dot(lo, wr_ref[:, :LANES])
    lane = lax.broadcasted_iota(jnp.int32, logits.shape, 1)
    is_grp = lane < N_GROUPS
    gmax, gidx = _lane_first_max(logits, is_grp, lane)
    gden = jnp.sum(jnp.where(is_grp, jnp.exp(logits - gmax), 0.0), axis=-1, keepdims=True)
    grp_gate = 1.0 / gden
    in_grp = (lane >= _EXP_LANE0) & (lane < _EXP_LANE0 + N_EXPERTS) & ((lane // EXPERTS_PER_GROUP - 1) == gidx)
    v1, i1 = _lane_first_max(logits, in_grp, lane)
    v2, i2 = _lane_first_max(logits, in_grp & (lane != i1), lane)
    e2 = jnp.exp(v2 - v1)
    w1 = grp_gate / (1.0 + e2)
    w2 = grp_gate * e2 / (1.0 + e2)
    gates_ref[r, :] = jnp.where(lane == i1, w1, 0.0) + jnp.where(lane == i2, w2, 0.0)


def _tail(x, om, osw, ona, mod_l, lw, latent):
    n_tok = x.shape[0]
    tm = TM_TAIL
    tiles_per_seq = DEC_SEQ // tm
    mod_idx = (lambda i: (1 + i // tiles_per_seq, 0, 0)) if latent else (lambda i: (0, 0, 0))
    row = lambda w: pl.BlockSpec((tm, w), lambda i: (i, 0))
    whole = lambda a: pl.BlockSpec(a.shape, lambda i: (0,) * a.ndim)
    return pl.pallas_call(
        _tail_kernel,
        grid=(n_tok // tm,),
        in_specs=[row(D_MODEL), row(512), row(256), row(256), pl.BlockSpec((1, N_MOD, D_MODEL), mod_idx),
                  whole(lw["norm2"]), whole(lw["wout"]), whole(lw["wr"])],
        out_specs=[row(D_MODEL), row(D_MODEL), row(LANES)],
        out_shape=[jax.ShapeDtypeStruct((n_tok, D_MODEL), F32), jax.ShapeDtypeStruct((n_tok, D_MODEL), BF16),
                   jax.ShapeDtypeStruct((n_tok, LANES), F32)],
        compiler_params=_cparams(("arbitrary",)),
        name="tail_lat" if latent else "tail_ctx",
    )(x, om, osw, ona, mod_l, lw["norm2"], lw["wout"], lw["wr"])


def _moe_kernel(final, h2_ref, gates_ref, x1_ref, mod_ref, nf_ref, wg_ref, wu_ref, wd_ref, o_ref, acc_ref):
    g = pl.program_id(1)
    h2 = h2_ref[...]
    gates = gates_ref[...]
    lane = lax.broadcasted_iota(jnp.int32, gates.shape, 1)
    contrib = None
    for e in range(EXPERTS_PER_GROUP):
        ge = jnp.sum(jnp.where(lane == _EXP_LANE0 + g * EXPERTS_PER_GROUP + e, gates, 0.0), axis=-1, keepdims=True)
        gu = _dot(h2, jnp.concatenate([wg_ref[0, e], wu_ref[0, e]], axis=1))
        hg, hu = gu[:, :EXPERT_FF], gu[:, EXPERT_FF:]
        act = hg * (1.0 / (1.0 + jnp.exp(-hg))) * hu * ge
        c = _dot(act.astype(BF16), wd_ref[0, e])
        contrib = c if contrib is None else contrib + c

    @pl.when(g == 0)
    def _():
        acc_ref[...] = contrib

    @pl.when((g > 0) & (g < N_GROUPS - 1))
    def _():
        acc_ref[...] += contrib

    @pl.when(g == N_GROUPS - 1)
    def _():
        y = x1_ref[...] + mod_ref[0][5:6] * (acc_ref[...] + contrib)
        if final:
            y = _rms(y, nf_ref[...])
        o_ref[...] = y


def _moe(h2, gates, x1, mod_l, l, moe_w, norm_final, latent, final):
    n_tok = h2.shape[0]
    tm = TM_MOE
    tiles_per_seq = DEC_SEQ // tm
    mod_idx = (lambda i, g: (1 + i // tiles_per_seq, 0, 0)) if latent else (lambda i, g: (0, 0, 0))
    epg = EXPERTS_PER_GROUP
    return pl.pallas_call(
        functools.partial(_moe_kernel, final),
        grid=(n_tok // tm, N_GROUPS),
        in_specs=[
            pl.BlockSpec((tm, D_MODEL), lambda i, g: (i, 0)),
            pl.BlockSpec((tm, LANES), lambda i, g: (i, 0)),
            pl.BlockSpec((tm, D_MODEL), lambda i, g: (i, 0)),
            pl.BlockSpec((1, N_MOD, D_MODEL), mod_idx),
            pl.BlockSpec((1, D_MODEL), lambda i, g: (0, 0)),
            pl.BlockSpec((1, epg, D_MODEL, EXPERT_FF), lambda i, g: (l, g, 0, 0)),
            pl.BlockSpec((1, epg, D_MODEL, EXPERT_FF), lambda i, g: (l, g, 0, 0)),
            pl.BlockSpec((1, epg, EXPERT_FF, D_MODEL), lambda i, g: (l, g, 0, 0)),
        ],
        out_specs=pl.BlockSpec((tm, D_MODEL), lambda i, g: (i, 0)),
        out_shape=jax.ShapeDtypeStruct((n_tok, D_MODEL), F32),
        scratch_shapes=[pltpu.VMEM((tm, D_MODEL), F32)],
        compiler_params=_cparams(("arbitrary", "arbitrary")),
        name=("moe_lat" if latent else "moe_ctx") + ("_final" if final else ""),
    )(h2, gates, x1, mod_l, norm_final, *moe_w)


def _heads(w, d, order, axis):
    parts = [lax.slice_in_dim(w, h * d, (h + 1) * d, axis=axis) for h in order]
    return jnp.concatenate(parts, axis=axis)


def _layer_weights(l, norm1, norm2, w_in, g_qa, w_uq, g_kva, w_ukv, w_out, w_router_grp, w_router_exp):
    wi = w_in[l].T
    z64 = jnp.zeros((64, D_MODEL), F32)
    cq, ckv, kpe = wi[0:256], wi[256:384], wi[384:416]
    qs = _heads(wi[416:672], HEAD_DIM, (0, 2, 1, 3), 0) * (HEAD_SCALE * LOG2E)
    ks, vs = wi[672:800], wi[800:928]
    qn, kn, vn = wi[928:1184] * (HEAD_SCALE * LOG2E), wi[1184:1440], wi[1440:1696]
    kped = jnp.concatenate([kpe, kpe, z64], axis=0)
    win_rows = [cq, ckv, qs, ks, vs, qn, kn, vn, kped]
    wq = w_uq[l].reshape(MLA_Q_LORA, MLA_HEADS, MLA_NOPE + MLA_ROPE) * (MLA_SCALE * LOG2E)
    nope, ropew = wq[:, :, :MLA_NOPE], wq[:, :, MLA_NOPE:]
    z64q = jnp.zeros((MLA_Q_LORA, 64), F32)
    blocks = []
    for i in range(N_PAIRS):
        blocks += [nope[:, 2 * i], nope[:, 2 * i + 1], ropew[:, 2 * i], ropew[:, 2 * i + 1], z64q]
    wkv = w_ukv[l].reshape(MLA_KV_LORA, MLA_HEADS, MLA_NOPE + MLA_V)
    wuk = wkv[:, :, :MLA_NOPE].reshape(MLA_KV_LORA, -1)
    wuvt = wkv[:, :, MLA_NOPE:].reshape(MLA_KV_LORA, -1).T
    wo = w_out[l]
    wout = jnp.concatenate([wo[:512], _heads(wo[512:768], HEAD_DIM, (0, 2, 1, 3), 0), wo[768:]], axis=0)
    wr = jnp.concatenate([w_router_grp[l], w_router_exp[l], jnp.zeros((D_MODEL, LANES - N_GROUPS - N_EXPERTS), F32)],
                         axis=1)
    wr_hi = wr.astype(BF16)
    wr_lo = (wr - wr_hi.astype(F32)).astype(BF16)
    return {
        "norm1": norm1[l][None], "norm2": norm2[l][None], "g_qa": g_qa[l][None], "g_kva": g_kva[l][None],
        "win": jnp.concatenate(win_rows, axis=0).astype(BF16),
        "wuq": jnp.concatenate(blocks, axis=1).astype(BF16),
        "wuk": wuk.astype(BF16),
        "wuvt": wuvt.astype(BF16),
        "wout": wout.astype(BF16),
        "wr": jnp.concatenate([wr_hi, wr_lo], axis=1),
    }


def _rope_tables():
    n_rows = DEC_SEQ // GRID_W
    lane = np.arange(LANES)

    def tab(d, used_lanes):
        hh = d // 4
        i = lane % d
        freq = ROPE_THETA ** (-jnp.asarray(i % hh, F32) / hh)
        by_row = jnp.asarray((i // (2 * hh)) == 0)[None, None, :]
        valid = jnp.asarray(lane < used_lanes)[None, None, :]
        ang_r = jnp.arange(n_rows, dtype=F32)[:, None] * freq[None, :]
        ang_c = jnp.arange(GRID_W, dtype=F32)[:, None] * freq[None, :]

        def expand(fn):
            t = jnp.where(by_row, fn(ang_r)[:, None, :], fn(ang_c)[None, :, :])
            return jnp.where(valid, t, 0.0).reshape(DEC_SEQ, LANES)

        return expand(jnp.cos), expand(jnp.sin)

    c64, s64 = tab(HEAD_DIM, LANES)
    c32, s32 = tab(MLA_ROPE, 2 * MLA_ROPE)
    return c64, s64, c32, s32


def kernel(x_prompt, x_sample, cache_mla_ckv, cache_mla_kpe, cache_swa_k, cache_swa_v, cache_na_k, cache_na_v, c, c_ctx, w_mod, b_mod, norm1, norm2, w_in, g_qa, w_uq, g_kva, w_ukv, swa_sink, na_rpb, w_out, w_router_grp, w_router_exp, w_gate, w_up, w_down, norm_final):
    cpad = jnp.concatenate([c_ctx[None], c, jnp.zeros((8 - 1 - DEC_BATCH, D_MODEL), F32)], axis=0)
    mod = _modulation(cpad, w_mod, b_mod).reshape(DEPTH, 8, N_MOD, D_MODEL)
    tabs = _rope_tables()
    nf = norm_final[None]
    xp = x_prompt.reshape(N_CTX_TOK, D_MODEL)
    xs = x_sample.reshape(N_LAT_TOK, D_MODEL)
    caches = [[] for _ in range(6)]
    moe_w = (w_gate.astype(BF16), w_up.astype(BF16), w_down.astype(BF16))
    for l in range(DEPTH):
        lw = _layer_weights(l, norm1, norm2, w_in, g_qa, w_uq, g_kva, w_ukv, w_out, w_router_grp, w_router_exp)
        final = l == DEPTH - 1
        outs = _projections(xp, mod[l], lw, None, rope=False)
        om, osw, ona = _context_attention(swa_sink[l], *outs[:9])
        for dst, a in zip(caches, outs[9:]):
            dst.append(a)
        x1, h2, gates = _tail(xp, om, osw, ona, mod[l], lw, latent=False)
        xp = _moe(h2, gates, x1, mod[l], l, moe_w, nf, latent=False, final=final)
        qm, km, vm, qs, ks, vs, qn, kn, vn = _projections(xs, mod[l], lw, tabs, rope=True)
        kpe_c = cache_mla_kpe[:, l].reshape(DEC_BATCH * PAST_LEN, MLA_ROPE)
        kpe_dup = jnp.concatenate([kpe_c, kpe_c, jnp.zeros((DEC_BATCH * PAST_LEN, 64), F32)], axis=1)
        kmc, vmc = _expand_cached_mla(cache_mla_ckv[:, l].reshape(DEC_BATCH * PAST_LEN, MLA_KV_LORA), kpe_dup,
                                      lw["wuk"], lw["wuvt"])
        om = _latent_mla(qm, km, vm, kmc, vmc)
        flat = lambda a: a[:, l].reshape(DEC_BATCH * PAST_LEN, -1)
        osw = _latent_swa(swa_sink[l], qs, ks, vs, flat(cache_swa_k), flat(cache_swa_v))
        ona = _latent_na(na_rpb[l], qn, kn, vn, flat(cache_na_k), flat(cache_na_v))
        x1, h2, gates = _tail(xs, om, osw, ona, mod[l], lw, latent=True)
        xs = _moe(h2, gates, x1, mod[l], l, moe_w, nf, latent=True, final=final)
    def heads_last(parts, n_heads):
        a = jnp.stack(parts, axis=1)
        if n_heads is None:
            return a.transpose(0, 1, 3, 2)
        return a.reshape(BATCH, DEPTH, n_heads, -1, SEQ).transpose(0, 1, 4, 2, 3)

    return (xp.reshape(BATCH, SEQ, D_MODEL), xs.reshape(DEC_BATCH, DEC_SEQ, D_MODEL),
            jnp.stack([p.reshape(BATCH, SEQ, MLA_KV_LORA) for p in caches[0]], axis=1), heads_last(caches[1], None),
            heads_last(caches[2], SWA_KV_HEADS), heads_last(caches[3], SWA_KV_HEADS),
            heads_last(caches[4], NA_HEADS), heads_last(caches[5], NA_HEADS))
```

```python
import functools

import jax
import jax.numpy as jnp
import numpy as np
from jax import lax
from jax.experimental import pallas as pl
from jax.experimental.pallas import tpu as pltpu

D_MODEL = 1024
BATCH = 32
SEQ = 256
DEPTH = 2
DEC_BATCH = 2
DEC_SEQ = 4096
PAST_LEN = 512
GRID_W = 64
HEAD_DIM = 64
MLA_HEADS = 8
MLA_Q_LORA = 256
MLA_KV_LORA = 128
MLA_NOPE = 64
MLA_ROPE = 32
MLA_V = 64
SWA_HEADS = 4
SWA_KV_HEADS = 2
SWA_WINDOW = 128
NA_HEADS = 4
NA_KH = 8
NA_KW = 16
N_GROUPS = 4
EXPERTS_PER_GROUP = 4
N_EXPERTS = 16
EXPERT_FF = 256
N_MOD = 6
ROPE_THETA = 10000.0
EPS = 1e-6
MASK_VALUE = -1e30
MLA_SCALE = (MLA_NOPE + MLA_ROPE) ** -0.5
HEAD_SCALE = HEAD_DIM ** -0.5
LOG2E = 1.4426950408889634

LANES = 128
N_PAIRS = MLA_HEADS // 2
MLA_QK_BLK = 2 * LANES
N_CTX_TOK = BATCH * SEQ
N_LAT_TOK = DEC_BATCH * DEC_SEQ

_C_CQ, _C_CKV, _C_QS, _C_KS, _C_QN, _C_KN, _C_KPE, _C_V, _C_END = 0, 256, 384, 640, 768, 1024, 1280, 1408, 1792
N_V_FEAT = _C_END - _C_V

TM_PROJ = 1024
TM_TAIL = 1024
TAIL_SUB_ROWS = 128
TM_MOE = 1024
LAT_TILES_PER_STEP = 4
CTX_BATCH_PER_STEP = 4
TQ = 256
TQ_MLA = 512
MLA_PAIRS_PER_STEP = 4
KV_CHUNK = 2048
MLA_KV_CHUNK = 512
SCORES_AHEAD = 2
ONES_ROWS = 16
SWA_KWIN = TQ + 2 * SWA_WINDOW
NA_ROWS_PER_TILE = TQ // GRID_W
NA_KEY_ROWS = 12
NA_KWIN = NA_KEY_ROWS * GRID_W
VMEM_LIMIT = 56 * 1024 * 1024

F32 = jnp.float32
BF16 = jnp.bfloat16


def _dot(a, b):
    return jnp.dot(a, b, preferred_element_type=F32)


def _dot_nt(a, b):
    return lax.dot_general(a, b, (((1,), (1,)), ((), ())), preferred_element_type=F32)


def _rms(x, g):
    return x * lax.rsqrt(jnp.mean(x * x, axis=-1, keepdims=True) + EPS) * g


def _cparams(sem):
    return pltpu.CompilerParams(dimension_semantics=sem, vmem_limit_bytes=VMEM_LIMIT)


N_COND = 1 + DEC_BATCH


def _mod_kernel(ct_ref, w_ref, b_ref, o_ref):
    ct = ct_ref[...]
    s = ct * (1.0 / (1.0 + jnp.exp(-ct)))
    w = w_ref[0]
    rows = [jnp.sum(w * s[:, r:r + 1], axis=0, keepdims=True) for r in range(N_COND)]
    rows.append(jnp.zeros((8 - N_COND, w.shape[1]), F32))
    o_ref[0] = jnp.concatenate(rows, axis=0) + b_ref[0]


def _modulation(cpad, w_mod, b_mod):
    nt = 1024
    return pl.pallas_call(
        _mod_kernel,
        grid=(DEPTH, N_MOD * D_MODEL // nt),
        in_specs=[
            pl.BlockSpec((D_MODEL, 8), lambda l, n: (0, 0)),
            pl.BlockSpec((1, D_MODEL, nt), lambda l, n: (l, 0, n)),
            pl.BlockSpec((1, 1, nt), lambda l, n: (l, 0, n)),
        ],
        out_specs=pl.BlockSpec((1, 8, nt), lambda l, n: (l, 0, n)),
        out_shape=jax.ShapeDtypeStruct((DEPTH, 8, N_MOD * D_MODEL), F32),
        compiler_params=_cparams(("arbitrary", "arbitrary")),
        name="modulation",
    )(cpad.T, w_mod, b_mod.reshape(DEPTH, 1, N_MOD * D_MODEL))


def _rope(x, d, cos, sin):
    hh = d // 4
    lane = lax.broadcasted_iota(jnp.int32, x.shape, 1)
    first_half = (lane % (2 * hh)) < hh
    rot = jnp.where(first_half, -pltpu.roll(x, LANES - hh, 1), pltpu.roll(x, hh, 1))
    return x * cos + rot * sin


def _proj_kernel(rope, *refs):
    if rope:
        (x_ref, mod_ref, n1_ref, win_ref, gqa_ref, wuq_ref, gkva_ref, wuk_ref, wuvt_ref,
         c64_ref, s64_ref, c32_ref, s32_ref,
         qm_ref, km_ref, vmt_ref, qs_ref, ks_ref, vs_ref, qn_ref, kn_ref, vn_ref) = refs
    else:
        (x_ref, mod_ref, n1_ref, win_ref, gqa_ref, wuq_ref, gkva_ref, wuk_ref, wuvt_ref,
         qm_ref, km_ref, vmt_ref, qs_ref, ks_ref, vs_ref, qn_ref, kn_ref, vn_ref,
         ckv_o, kpe_o, ks_o, vs_o, kn_o, vn_o) = refs
    m = mod_ref[0]
    x_ref, n1_ref, win_ref = refs[0], refs[2], refs[3]
    n_sub = x_ref.shape[0] // SEQ
    ps = []
    for sub in range(n_sub):
        h = _rms(x_ref[sub * SEQ:(sub + 1) * SEQ, :], n1_ref[...]) * (1.0 + m[1:2]) + m[0:1]
        hb = h.astype(BF16)
        ps.append((_dot_nt(hb, win_ref[0:_C_V, :]), _dot_nt(win_ref[_C_V:_C_END, :], hb)))
    for sub in range(n_sub):
        _proj_rows(sub, rope, ps[sub][0], ps[sub][1], refs)


def _proj_rows(sub, rope, p, vt, refs):
    if rope:
        (x_ref, mod_ref, n1_ref, win_ref, gqa_ref, wuq_ref, gkva_ref, wuk_ref, wuvt_ref,
         c64_ref, s64_ref, c32_ref, s32_ref,
         qm_ref, km_ref, vmt_ref, qs_ref, ks_ref, vs_ref, qn_ref, kn_ref, vn_ref) = refs
    else:
        (x_ref, mod_ref, n1_ref, win_ref, gqa_ref, wuq_ref, gkva_ref, wuk_ref, wuvt_ref,
         qm_ref, km_ref, vmt_ref, qs_ref, ks_ref, vs_ref, qn_ref, kn_ref, vn_ref,
         ckv_o, kpe_o, ks_o, vs_o, kn_o, vn_o) = refs
    r = slice(sub * SEQ, (sub + 1) * SEQ)
    qm = _dot(_rms(p[:, _C_CQ:_C_CKV], gqa_ref[...]).astype(BF16), wuq_ref[...])
    ckv = _rms(p[:, _C_CKV:_C_QS], gkva_ref[...])
    ckv_b = ckv.astype(BF16)
    kn_mla = _dot(ckv_b, wuk_ref[...])
    vmt_ref[:, r] = _dot_nt(wuvt_ref[...], ckv_b).astype(BF16)
    qs = p[:, _C_QS:_C_KS]
    ks = p[:, _C_KS:_C_QN]
    kpe = p[:, _C_KPE:_C_V]
    if rope:
        c64, s64, c32, s32 = c64_ref[r, :], s64_ref[r, :], c32_ref[r, :], s32_ref[r, :]
        qs = jnp.concatenate([_rope(qs[:, j * LANES:(j + 1) * LANES], HEAD_DIM, c64, s64) for j in range(2)], axis=1)
        ks = _rope(ks, HEAD_DIM, c64, s64)
        kpe = _rope(kpe, MLA_ROPE, c32, s32)
    for i in range(N_PAIRS):
        lo = i * MLA_QK_BLK
        qrope = qm[:, lo + LANES:lo + MLA_QK_BLK]
        if rope:
            qrope = _rope(qrope, MLA_ROPE, c32, s32)
        qm_ref[r, lo:lo + LANES] = qm[:, lo:lo + LANES].astype(BF16)
        qm_ref[r, lo + LANES:lo + MLA_QK_BLK] = qrope.astype(BF16)
        km_ref[r, lo:lo + LANES] = kn_mla[:, i * LANES:(i + 1) * LANES].astype(BF16)
        km_ref[r, lo + LANES:lo + MLA_QK_BLK] = kpe.astype(BF16)
    qs_ref[r, :] = qs.astype(BF16)
    ks_ref[r, :] = ks.astype(BF16)
    vs_ref[:, r] = vt[0:128, :].astype(BF16)
    qn_ref[r, :] = p[:, _C_QN:_C_KN].astype(BF16)
    kn_ref[r, :] = p[:, _C_KN:_C_KPE].astype(BF16)
    vn_ref[:, r] = vt[128:N_V_FEAT, :].astype(BF16)
    if not rope:
        ckv_o[r, :] = ckv
        vs_o[sub] = vt[0:128, :]
        vn_o[sub] = vt[128:N_V_FEAT, :]

        def put_feature_major(o_ref, val, n_feat):
            o_ref[sub] = val.T[:n_feat, :]

        put_feature_major(kpe_o, kpe, MLA_ROPE)
        put_feature_major(ks_o, ks, 128)
        put_feature_major(kn_o, p[:, _C_KN:_C_KPE], 256)


def _projections(x, mod_l, lw, tabs, rope):
    n_tok = x.shape[0]
    tm = TM_PROJ
    tiles_per_seq = DEC_SEQ // tm
    win, wuq = lw["win"], lw["wuq"]
    mod_idx = (lambda i: (1 + i // tiles_per_seq, 0, 0)) if rope else (lambda i: (0, 0, 0))
    row = lambda w: pl.BlockSpec((tm, w), lambda i: (i, 0))
    whole = lambda a: pl.BlockSpec(a.shape, lambda i: (0,) * a.ndim)
    in_specs = [row(D_MODEL), pl.BlockSpec((1, N_MOD, D_MODEL), mod_idx), whole(lw["norm1"]), whole(win),
                whole(lw["g_qa"]), whole(wuq), whole(lw["g_kva"]), whole(lw["wuk"]), whole(lw["wuvt"])]
    args = [x, mod_l, lw["norm1"], win, lw["g_qa"], wuq, lw["g_kva"], lw["wuk"], lw["wuvt"]]
    outs = [(N_PAIRS * MLA_QK_BLK, False), (N_PAIRS * MLA_QK_BLK, False), (N_PAIRS * LANES, True), (256, False),
            (128, False), (128, True), (256, False), (256, False), (256, True)]
    out_specs = [pl.BlockSpec((w, tm), lambda i: (0, i)) if fm else row(w) for w, fm in outs]
    out_shape = [jax.ShapeDtypeStruct((w, n_tok) if fm else (n_tok, w), BF16) for w, fm in outs]
    if rope:
        tab_spec = pl.BlockSpec((tm, LANES), lambda i: (i % tiles_per_seq, 0))
        in_specs += [tab_spec] * 4
        args += list(tabs)
    else:
        out_specs.append(row(MLA_KV_LORA))
        out_shape.append(jax.ShapeDtypeStruct((n_tok, MLA_KV_LORA), F32))
        for w in [MLA_ROPE, 128, 128, 256, 256]:
            out_specs.append(pl.BlockSpec((tm // SEQ, w, SEQ), lambda i: (i, 0, 0)))
            out_shape.append(jax.ShapeDtypeStruct((n_tok // SEQ, w, SEQ), F32))
    return pl.pallas_call(
        functools.partial(_proj_kernel, rope),
        grid=(n_tok // tm,),
        in_specs=in_specs,
        out_specs=out_specs,
        out_shape=out_shape,
        compiler_params=_cparams(("arbitrary",)),
        name="proj_lat" if rope else "proj_ctx",
    )(*args)


def _ctxkv_kernel(ckv_ref, kpe_ref, wuk_ref, wuvt_ref, km_ref, vmt_ref):
    ckv_b = ckv_ref[...].astype(BF16)
    kn_mla = _dot(ckv_b, wuk_ref[...])
    kpe = kpe_ref[...].astype(BF16)
    for i in range(N_PAIRS):
        lo = i * MLA_QK_BLK
        km_ref[:, lo:lo + LANES] = kn_mla[:, i * LANES:(i + 1) * LANES].astype(BF16)
        km_ref[:, lo + LANES:lo + MLA_QK_BLK] = kpe
    vmt_ref[...] = _dot_nt(wuvt_ref[...], ckv_b).astype(BF16)


def _expand_cached_mla(ckv_c, kpe_dup, wuk, wuvt):
    n = ckv_c.shape[0]
    whole = lambda a: pl.BlockSpec(a.shape, lambda i: (0,) * a.ndim)
    return pl.pallas_call(
        _ctxkv_kernel,
        grid=(1,),
        in_specs=[whole(ckv_c), whole(kpe_dup), whole(wuk), whole(wuvt)],
        out_specs=[pl.BlockSpec((n, N_PAIRS * MLA_QK_BLK), lambda i: (0, 0)),
                   pl.BlockSpec((N_PAIRS * LANES, n), lambda i: (0, 0))],
        out_shape=[jax.ShapeDtypeStruct((n, N_PAIRS * MLA_QK_BLK), BF16),
                   jax.ShapeDtypeStruct((N_PAIRS * LANES, n), BF16)],
        compiler_params=_cparams(("arbitrary",)),
        name="expand_cached_mla",
    )(ckv_c, kpe_dup, wuk, wuvt)


def _pair_masks(width):
    lane = lax.broadcasted_iota(jnp.int32, (1, width), 1)
    if width == LANES:
        return [lane < HEAD_DIM, lane >= HEAD_DIM]
    m0 = (lane < MLA_NOPE) | ((lane >= LANES) & (lane < LANES + MLA_ROPE))
    m1 = ((lane >= MLA_NOPE) & (lane < LANES)) | ((lane >= LANES + MLA_ROPE) & (lane < LANES + 2 * MLA_ROPE))
    return [m0, m1]


def _attention(jobs):
    steps, qa = [], []
    for j, (q, blocks, _) in enumerate(jobs):
        masks = _pair_masks(q.shape[1])
        qa.append([jnp.where(masks[a], q, jnp.zeros_like(q)) for a in range(2)])
        chunks = []
        for blk in blocks:
            k, vt, post = blk[:3]
            cuts = blk[3] if len(blk) > 3 else list(range(0, k.shape[0], KV_CHUNK)) + [k.shape[0]]
            chunks += [(k, vt, post, lo, hi) for lo, hi in zip(cuts[:-1], cuts[1:])]
        steps += [(j, c, ci == len(chunks) - 1) for ci, c in enumerate(chunks)]

    def scores(step):
        j, (k, _, post, lo, hi), _ = step
        kc = k[lo:hi, :]
        s = [_dot_nt(kc, qa[j][a]) for a in range(2)]
        return s if post is None else [post(a, s[a], lo, hi) for a in range(2)]

    outs = [None] * len(jobs)
    m, acc = [None, None], [None, None]
    pending = [scores(st) for st in steps[:SCORES_AHEAD]]
    for n, (j, (_, vt, _, lo, hi), last) in enumerate(steps):
        if n + SCORES_AHEAD < len(steps):
            pending.append(scores(steps[n + SCORES_AHEAD]))
        s_cur = pending.pop(0)
        ones = jnp.ones((ONES_ROWS, hi - lo), BF16)
        for a in range(2):
            vta = jnp.concatenate([vt[a * HEAD_DIM:(a + 1) * HEAD_DIM, lo:hi], ones], axis=0)
            s = s_cur[a]
            mc = jnp.max(s, axis=0, keepdims=True)
            mn = mc if m[a] is None else jnp.maximum(m[a], mc)
            pv = _dot(vta, jnp.exp2(s - mn).astype(BF16))
            acc[a] = pv if m[a] is None else jnp.exp2(m[a] - mn) * acc[a] + pv
            m[a] = mn
        if last:
            sinks, heads = jobs[j][2], []
            for a in range(2):
                l = acc[a][HEAD_DIM:HEAD_DIM + 1, :]
                if sinks[a] is not None:
                    mf = jnp.maximum(m[a], sinks[a])
                    scale = jnp.exp2(m[a] - mf)
                    l = scale * l + jnp.exp2(sinks[a] - mf)
                    heads.append(acc[a][:HEAD_DIM, :] * (scale / l))
                else:
                    heads.append(acc[a][:HEAD_DIM, :] / l)
            outs[j] = jnp.concatenate(heads, axis=0).T
            m, acc = [None, None], [None, None]
    return outs


_NO_SINKS = (None, None)


def _ctx_attn_kernel(sink_ref, qm_ref, km_ref, vmt_ref, qs_ref, ks_ref, vs_ref, qn_ref, kn_ref, vn_ref,
                     om_ref, os_ref, on_ref):
    jobs = []
    for bb in range(CTX_BATCH_PER_STEP):
        r = slice(bb * SEQ, (bb + 1) * SEQ)
        for i in range(N_PAIRS):
            q = qm_ref[r, i * MLA_QK_BLK:(i + 1) * MLA_QK_BLK]
            k = km_ref[r, i * MLA_QK_BLK:(i + 1) * MLA_QK_BLK]
            vt = vmt_ref[i * LANES:(i + 1) * LANES, r]
            jobs.append((q, [(k, vt, None)], _NO_SINKS))
        ks, vst = ks_ref[r, :], vs_ref[:, r]
        for j in range(2):
            sinks = (sink_ref[j] * LOG2E, sink_ref[j + 2] * LOG2E)
            jobs.append((qs_ref[r, j * LANES:(j + 1) * LANES], [(ks, vst, None)], sinks))
        for j in range(2):
            sl = slice(j * LANES, (j + 1) * LANES)
            jobs.append((qn_ref[r, sl], [(kn_ref[r, sl], vn_ref[sl, r], None)], _NO_SINKS))
    outs = _attention(jobs)
    per_batch = N_PAIRS + 4
    for bb in range(CTX_BATCH_PER_STEP):
        r = slice(bb * SEQ, (bb + 1) * SEQ)
        o = outs[bb * per_batch:(bb + 1) * per_batch]
        for i in range(N_PAIRS):
            om_ref[r, i * LANES:(i + 1) * LANES] = o[i].astype(BF16)
        for j in range(2):
            os_ref[r, j * LANES:(j + 1) * LANES] = o[N_PAIRS + j].astype(BF16)
            on_ref[r, j * LANES:(j + 1) * LANES] = o[N_PAIRS + 2 + j].astype(BF16)


def _context_attention(sink_l, qm, km, vmt, qs, ks, vs, qn, kn, vn):
    rows = CTX_BATCH_PER_STEP * SEQ
    row = lambda a: (pl.BlockSpec((a.shape[0], rows), lambda b: (0, b)) if any(a is v for v in (vmt, vs, vn))
                     else pl.BlockSpec((rows, a.shape[1]), lambda b: (b, 0)))
    ins = [qm, km, vmt, qs, ks, vs, qn, kn, vn]
    widths = [N_PAIRS * LANES, 256, 256]
    return pl.pallas_call(
        _ctx_attn_kernel,
        grid=(BATCH // CTX_BATCH_PER_STEP,),
        in_specs=[pl.BlockSpec(memory_space=pltpu.SMEM)] + [row(a) for a in ins],
        out_specs=[pl.BlockSpec((rows, w), lambda b: (b, 0)) for w in widths],
        out_shape=[jax.ShapeDtypeStruct((N_CTX_TOK, w), BF16) for w in widths],
        compiler_params=_cparams(("arbitrary",)),
        name="attn_ctx",
    )(sink_l, *ins)


def _lat_mla_kernel(q_ref, kl_ref, vlt_ref, kc_ref, vct_ref, o_ref):
    jobs = []
    for i in range(MLA_PAIRS_PER_STEP):
        qk, v = slice(i * MLA_QK_BLK, (i + 1) * MLA_QK_BLK), slice(i * LANES, (i + 1) * LANES)
        blocks = [(kc_ref.at[:, qk], vct_ref.at[v, :], None),
                  (kl_ref.at[:, qk], vlt_ref.at[v, :], None, list(range(0, DEC_SEQ + 1, MLA_KV_CHUNK)))]
        jobs.append((q_ref[:, qk], blocks, _NO_SINKS))
    for i, o in enumerate(_attention(jobs)):
        o_ref[:, i * LANES:(i + 1) * LANES] = o.astype(BF16)


def _latent_mla(qm, km, vmt, kmc, vmct):
    nq = DEC_SEQ // TQ_MLA
    pp = MLA_PAIRS_PER_STEP
    return pl.pallas_call(
        _lat_mla_kernel,
        grid=(DEC_BATCH, N_PAIRS // pp, nq),
        in_specs=[
            pl.BlockSpec((TQ_MLA, pp * MLA_QK_BLK), lambda b, i, t: (b * nq + t, i)),
            pl.BlockSpec((DEC_SEQ, pp * MLA_QK_BLK), lambda b, i, t: (b, i)),
            pl.BlockSpec((pp * LANES, DEC_SEQ), lambda b, i, t: (i, b)),
            pl.BlockSpec((PAST_LEN, pp * MLA_QK_BLK), lambda b, i, t: (b, i)),
            pl.BlockSpec((pp * LANES, PAST_LEN), lambda b, i, t: (i, b)),
        ],
        out_specs=pl.BlockSpec((TQ_MLA, pp * LANES), lambda b, i, t: (b * nq + t, i)),
        out_shape=jax.ShapeDtypeStruct((N_LAT_TOK, N_PAIRS * LANES), BF16),
        compiler_params=_cparams(("arbitrary", "arbitrary", "arbitrary")),
        name="attn_lat_mla",
    )(qm, km, vmt, kmc, vmct)


def _lat_swa_kernel(sink_ref, q_ref, k_ref, v_ref, kc_ref, vc_ref, o_ref):
    kc = kc_ref[...].astype(BF16)
    vc = vc_ref[...].astype(BF16)
    jobs = []
    for u in range(LAT_TILES_PER_STEP):
        q0 = (pl.program_id(1) * LAT_TILES_PER_STEP + u) * TQ
        ws = pl.multiple_of(jnp.clip(q0 - SWA_WINDOW, 0, DEC_SEQ - SWA_KWIN), SWA_WINDOW)
        kw = k_ref[pl.ds(ws, SWA_KWIN), :]
        vw = v_ref[:, pl.ds(ws, SWA_KWIN)]
        kpos = ws + lax.broadcasted_iota(jnp.int32, (SWA_KWIN, TQ), 0)
        qpos = q0 + lax.broadcasted_iota(jnp.int32, (SWA_KWIN, TQ), 1)
        in_band = jnp.abs(kpos - qpos) <= SWA_WINDOW

        def band(a, s, lo, hi, in_band=in_band):
            return jnp.where(in_band[lo:hi, :], s, MASK_VALUE)

        for j in range(2):
            sinks = (sink_ref[j] * LOG2E, sink_ref[j + 2] * LOG2E)
            q = q_ref[u * TQ:(u + 1) * TQ, j * LANES:(j + 1) * LANES]
            jobs.append((q, [(kw, vw, band), (kc, vc, None)], sinks))
    for n, o in enumerate(_attention(jobs)):
        u, j = divmod(n, 2)
        o_ref[u * TQ:(u + 1) * TQ, j * LANES:(j + 1) * LANES] = o.astype(BF16)


def _latent_swa(sink_l, qs, ks, vs, ksc, vsc):
    rows = LAT_TILES_PER_STEP * TQ
    nq = DEC_SEQ // rows
    return pl.pallas_call(
        _lat_swa_kernel,
        grid=(DEC_BATCH, nq),
        in_specs=[
            pl.BlockSpec(memory_space=pltpu.SMEM),
            pl.BlockSpec((rows, 256), lambda b, t: (b * nq + t, 0)),
            pl.BlockSpec((DEC_SEQ, LANES), lambda b, t: (b, 0)),
            pl.BlockSpec((LANES, DEC_SEQ), lambda b, t: (0, b)),
            pl.BlockSpec((PAST_LEN, LANES), lambda b, t: (b, 0)),
            pl.BlockSpec((LANES, PAST_LEN), lambda b, t: (b, 0)),
        ],
        out_specs=pl.BlockSpec((rows, 256), lambda b, t: (b * nq + t, 0)),
        out_shape=jax.ShapeDtypeStruct((N_LAT_TOK, 256), BF16),
        compiler_params=_cparams(("arbitrary", "arbitrary")),
        name="attn_lat_swa",
    )(sink_l, qs, ks, vs, ksc, vsc)


_NA_DR = 2 * NA_KH - 1


def _build_na_bias_tiles(rpb_ref, tile_ref):
    kc = lax.broadcasted_iota(jnp.int32, (GRID_W, LANES), 0)
    qc = lax.broadcasted_iota(jnp.int32, (GRID_W, LANES), 1) % GRID_W
    rel = jnp.clip(kc - qc, -(NA_KW - 1), NA_KW - 1) + NA_KW - 1
    col_start = jnp.clip(qc - NA_KW // 2, 0, GRID_W - NA_KW)
    col_ok = (kc >= col_start) & (kc < col_start + NA_KW)
    masked = jnp.full((GRID_W, LANES), MASK_VALUE, F32)
    n_rel = 2 * NA_KW - 1

    def one_tile(hd, carry):
        tile = masked
        for c in range(n_rel):
            tile = jnp.where(col_ok & (rel == c), rpb_ref[hd * n_rel + c] * LOG2E, tile)
        tile_ref[(hd // _NA_DR) * (_NA_DR + 1) + hd % _NA_DR] = tile
        return carry

    lax.fori_loop(0, NA_HEADS * _NA_DR, one_tile, 0)
    for h in range(NA_HEADS):
        tile_ref[h * (_NA_DR + 1) + _NA_DR] = masked


def _na_tile_kinds():
    n_rows = DEC_SEQ // GRID_W
    return (0, NA_ROWS_PER_TILE, n_rows - NA_ROWS_PER_TILE)


def _assemble_na_bias(tile_ref, bias_ref):
    n_rows = DEC_SEQ // GRID_W
    lane_lo = lax.broadcasted_iota(jnp.int32, (GRID_W, LANES), 1) < GRID_W
    for kind, r0 in enumerate(_na_tile_kinds()):
        ws_row = int(np.clip(r0 - NA_KH // 2, 0, n_rows - NA_KEY_ROWS))

        def tile(h, a_q, jj):
            r, kr = r0 + a_q, ws_row + jj
            rs = int(np.clip(r - NA_KH // 2, 0, n_rows - NA_KH))
            return tile_ref[h * (_NA_DR + 1) + (kr - r + NA_KH - 1 if rs <= kr < rs + NA_KH else _NA_DR)]

        for h in range(NA_HEADS):
            for jj in range(NA_KEY_ROWS):
                for u in range(NA_ROWS_PER_TILE // 2):
                    bias_ref[kind * NA_HEADS + h, jj * GRID_W:(jj + 1) * GRID_W, u * LANES:(u + 1) * LANES] = (
                        jnp.where(lane_lo, tile(h, 2 * u, jj), tile(h, 2 * u + 1, jj)))


def _lat_na_kernel(rpb_ref, q_ref, k_ref, v_ref, kc_ref, vc_ref, o_ref, tile_ref, bias_ref):
    t = pl.program_id(1)

    @pl.when((pl.program_id(0) == 0) & (t == 0))
    def _():
        _build_na_bias_tiles(rpb_ref, tile_ref)
        _assemble_na_bias(tile_ref, bias_ref)

    n_rows = DEC_SEQ // GRID_W
    last_tile = DEC_SEQ // TQ - 1
    jobs = []
    for u in range(LAT_TILES_PER_STEP):
        tile = t * LAT_TILES_PER_STEP + u
        r0 = tile * NA_ROWS_PER_TILE
        ws_row = jnp.clip(r0 - NA_KH // 2, 0, n_rows - NA_KEY_ROWS)
        ws = pl.multiple_of(ws_row * GRID_W, NA_ROWS_PER_TILE * GRID_W)
        kind = jnp.where(tile == 0, 0, jnp.where(tile == last_tile, 2, 1))
        for j in range(2):
            sl = slice(j * LANES, (j + 1) * LANES)
            kw = k_ref[pl.ds(ws, NA_KWIN), sl]
            vw = v_ref[sl, pl.ds(ws, NA_KWIN)]
            kc = kc_ref[:, sl].astype(BF16)
            vc = vc_ref[sl, :].astype(BF16)

            def add_bias(a, s, lo, hi, j=j, kind=kind):
                return s + bias_ref[kind * NA_HEADS + 2 * j + a, lo:hi, :]

            jobs.append((q_ref[u * TQ:(u + 1) * TQ, sl], [(kw, vw, add_bias), (kc, vc, None)], _NO_SINKS))
    for n, o in enumerate(_attention(jobs)):
        u, j = divmod(n, 2)
        o_ref[u * TQ:(u + 1) * TQ, j * LANES:(j + 1) * LANES] = o.astype(BF16)


def _latent_na(rpb_l, qn, kn, vn, knc, vnc):
    rows = LAT_TILES_PER_STEP * TQ
    nq = DEC_SEQ // rows
    return pl.pallas_call(
        _lat_na_kernel,
        grid=(DEC_BATCH, nq),
        in_specs=[
            pl.BlockSpec(memory_space=pltpu.SMEM),
            pl.BlockSpec((rows, 256), lambda b, t: (b * nq + t, 0)),
            pl.BlockSpec((DEC_SEQ, 256), lambda b, t: (b, 0)),
            pl.BlockSpec((256, DEC_SEQ), lambda b, t: (0, b)),
            pl.BlockSpec((PAST_LEN, 256), lambda b, t: (b, 0)),
            pl.BlockSpec((256, PAST_LEN), lambda b, t: (b, 0)),
        ],
        out_specs=pl.BlockSpec((rows, 256), lambda b, t: (b * nq + t, 0)),
        out_shape=jax.ShapeDtypeStruct((N_LAT_TOK, 256), BF16),
        scratch_shapes=[pltpu.VMEM((NA_HEADS * (_NA_DR + 1), GRID_W, LANES), F32),
                        pltpu.VMEM((3 * NA_HEADS, NA_KWIN, TQ), F32)],
        compiler_params=_cparams(("arbitrary", "arbitrary")),
        name="attn_lat_na",
    )(rpb_l.reshape(-1), qn, kn, vn, knc, vnc)


_GRP_LANE0 = 0
_EXP_LANE0 = N_GROUPS


def _lane_first_max(x, valid, lane):
    xm = jnp.where(valid, x, -jnp.inf)
    mx = jnp.max(xm, axis=-1, keepdims=True)
    idx = jnp.min(jnp.where(valid & (xm == mx), lane, LANES), axis=-1, keepdims=True)
    return mx, idx


def _tail_kernel(x_ref, om_ref, os_ref, on_ref, mod_ref, n2_ref, wo_ref, wr_ref, x1_ref, h2_ref, gates_ref):
    n_sub = x_ref.shape[0] // TAIL_SUB_ROWS
    subs = [slice(i * TAIL_SUB_ROWS, (i + 1) * TAIL_SUB_ROWS) for i in range(n_sub)]
    wo = wo_ref
    attn = [(_dot(om_ref[r, :], wo[0:512, :]) + _dot(os_ref[r, :], wo[512:768, :]) + _dot(on_ref[r, :], wo[768:1024, :]))
            for r in subs]
    for r, attn_r in zip(subs, attn):
        _tail_rows(r, attn_r, x_ref, mod_ref, n2_ref, wr_ref, x1_ref, h2_ref, gates_ref)


def _tail_rows(r, attn, x_ref, mod_ref, n2_ref, wr_ref, x1_ref, h2_ref, gates_ref):
    m = mod_ref[0]
    x1 = x_ref[r, :] + m[2:3] * attn
    x1_ref[r, :] = x1
    h2 = _rms(x1, n2_ref[...]) * (1.0 + m[4:5]) + m[3:4]
    h2_ref[r, :] = h2.astype(BF16)
    hi = h2.astype(BF16)
    lo = (h2 - hi.astype(F32)).astype(BF16)
    a = _dot(hi, wr_ref[...])
    logits = a[:, :LANES] + a[:, LANES:] + _dot(lo, wr_ref[:, :LANES])
    lane = lax.broadcasted_iota(jnp.int32, logits.shape, 1)
    is_grp = lane < N_GROUPS
    gmax, gidx = _lane_first_max(logits, is_grp, lane)
    gden = jnp.sum(jnp.where(is_grp, jnp.exp(logits - gmax), 0.0), axis=-1, keepdims=True)
    grp_gate = 1.0 / gden
    in_grp = (lane >= _EXP_LANE0) & (lane < _EXP_LANE0 + N_EXPERTS) & ((lane // EXPERTS_PER_GROUP - 1) == gidx)
    v1, i1 = _lane_first_max(logits, in_grp, lane)
    v2, i2 = _lane_first_max(logits, in_grp & (lane != i1), lane)
    e2 = jnp.exp(v2 - v1)
    w1 = grp_gate / (1.0 + e2)
    w2 = grp_gate * e2 / (1.0 + e2)
    gates_ref[r, :] = jnp.where(lane == i1, w1, 0.0) + jnp.where(lane == i2, w2, 0.0)


def _tail(x, om, osw, ona, mod_l, lw, latent):
    n_tok = x.shape[0]
    tm = TM_TAIL
    tiles_per_seq = DEC_SEQ // tm
    mod_idx = (lambda i: (1 + i // tiles_per_seq, 0, 0)) if latent else (lambda i: (0, 0, 0))
    row = lambda w: pl.BlockSpec((tm, w), lambda i: (i, 0))
    whole = lambda a: pl.BlockSpec(a.shape, lambda i: (0,) * a.ndim)
    return pl.pallas_call(
        _tail_kernel,
        grid=(n_tok // tm,),
        in_specs=[row(D_MODEL), row(512), row(256), row(256), pl.BlockSpec((1, N_MOD, D_MODEL), mod_idx),
                  whole(lw["norm2"]), whole(lw["wout"]), whole(lw["wr"])],
        out_specs=[row(D_MODEL), row(D_MODEL), row(LANES)],
        out_shape=[jax.ShapeDtypeStruct((n_tok, D_MODEL), F32), jax.ShapeDtypeStruct((n_tok, D_MODEL), BF16),
                   jax.ShapeDtypeStruct((n_tok, LANES), F32)],
        compiler_params=_cparams(("arbitrary",)),
        name="tail_lat" if latent else "tail_ctx",
    )(x, om, osw, ona, mod_l, lw["norm2"], lw["wout"], lw["wr"])


def _moe_kernel(final, h2_ref, gates_ref, x1_ref, mod_ref, nf_ref, wg_ref, wu_ref, wd_ref, o_ref, acc_ref):
    g = pl.program_id(1)
    h2 = h2_ref[...]
    gates = gates_ref[...]
    lane = lax.broadcasted_iota(jnp.int32, gates.shape, 1)
    acts = []
    for e in range(EXPERTS_PER_GROUP):
        ge = jnp.sum(jnp.where(lane == _EXP_LANE0 + g * EXPERTS_PER_GROUP + e, gates, 0.0), axis=-1, keepdims=True)
        hg = _dot(h2, wg_ref[0, e])
        hu = _dot(h2, wu_ref[0, e])
        acts.append((hg * (1.0 / (1.0 + jnp.exp(-hg))) * hu * ge).astype(BF16))
    contrib = _dot(jnp.concatenate(acts, axis=1), wd_ref[0].reshape(EXPERTS_PER_GROUP * EXPERT_FF, D_MODEL))

    @pl.when(g == 0)
    def _():
        acc_ref[...] = contrib

    @pl.when((g > 0) & (g < N_GROUPS - 1))
    def _():
        acc_ref[...] += contrib

    @pl.when(g == N_GROUPS - 1)
    def _():
        y = x1_ref[...] + mod_ref[0][5:6] * (acc_ref[...] + contrib)
        if final:
            y = _rms(y, nf_ref[...])
        o_ref[...] = y


def _moe(h2, gates, x1, mod_l, l, moe_w, norm_final, latent, final):
    n_tok = h2.shape[0]
    tm = TM_MOE
    tiles_per_seq = DEC_SEQ // tm
    mod_idx = (lambda i, g: (1 + i // tiles_per_seq, 0, 0)) if latent else (lambda i, g: (0, 0, 0))
    epg = EXPERTS_PER_GROUP
    return pl.pallas_call(
        functools.partial(_moe_kernel, final),
        grid=(n_tok // tm, N_GROUPS),
        in_specs=[
            pl.BlockSpec((tm, D_MODEL), lambda i, g: (i, 0)),
            pl.BlockSpec((tm, LANES), lambda i, g: (i, 0)),
            pl.BlockSpec((tm, D_MODEL), lambda i, g: (i, 0)),
            pl.BlockSpec((1, N_MOD, D_MODEL), mod_idx),
            pl.BlockSpec((1, D_MODEL), lambda i, g: (0, 0)),
            pl.BlockSpec((1, epg, D_MODEL, EXPERT_FF), lambda i, g: (l, g, 0, 0)),
            pl.BlockSpec((1, epg, D_MODEL, EXPERT_FF), lambda i, g: (l, g, 0, 0)),
            pl.BlockSpec((1, epg, EXPERT_FF, D_MODEL), lambda i, g: (l, g, 0, 0)),
        ],
        out_specs=pl.BlockSpec((tm, D_MODEL), lambda i, g: (i, 0)),
        out_shape=jax.ShapeDtypeStruct((n_tok, D_MODEL), F32),
        scratch_shapes=[pltpu.VMEM((tm, D_MODEL), F32)],
        compiler_params=_cparams(("arbitrary", "arbitrary")),
        name=("moe_lat" if latent else "moe_ctx") + ("_final" if final else ""),
    )(h2, gates, x1, mod_l, norm_final, *moe_w)


def _heads(w, d, order, axis):
    parts = [lax.slice_in_dim(w, h * d, (h + 1) * d, axis=axis) for h in order]
    return jnp.concatenate(parts, axis=axis)


def _layer_weights(l, norm1, norm2, w_in, g_qa, w_uq, g_kva, w_ukv, w_out, w_router_grp, w_router_exp):
    wi = w_in[l].T
    z64 = jnp.zeros((64, D_MODEL), F32)
    cq, ckv, kpe = wi[0:256], wi[256:384], wi[384:416]
    qs = _heads(wi[416:672], HEAD_DIM, (0, 2, 1, 3), 0) * (HEAD_SCALE * LOG2E)
    ks, vs = wi[672:800], wi[800:928]
    qn, kn, vn = wi[928:1184] * (HEAD_SCALE * LOG2E), wi[1184:1440], wi[1440:1696]
    kped = jnp.concatenate([kpe, kpe, z64], axis=0)
    win_rows = [cq, ckv, qs, ks, qn, kn, kped, vs, vn]
    wq = w_uq[l].reshape(MLA_Q_LORA, MLA_HEADS, MLA_NOPE + MLA_ROPE) * (MLA_SCALE * LOG2E)
    nope, ropew = wq[:, :, :MLA_NOPE], wq[:, :, MLA_NOPE:]
    z64q = jnp.zeros((MLA_Q_LORA, 64), F32)
    blocks = []
    for i in range(N_PAIRS):
        blocks += [nope[:, 2 * i], nope[:, 2 * i + 1], ropew[:, 2 * i], ropew[:, 2 * i + 1], z64q]
    wkv = w_ukv[l].reshape(MLA_KV_LORA, MLA_HEADS, MLA_NOPE + MLA_V)
    wuk = wkv[:, :, :MLA_NOPE].reshape(MLA_KV_LORA, -1)
    wuvt = wkv[:, :, MLA_NOPE:].reshape(MLA_KV_LORA, -1).T
    wo = w_out[l]
    wout = jnp.concatenate([wo[:512], _heads(wo[512:768], HEAD_DIM, (0, 2, 1, 3), 0), wo[768:]], axis=0)
    wr = jnp.concatenate([w_router_grp[l], w_router_exp[l], jnp.zeros((D_MODEL, LANES - N_GROUPS - N_EXPERTS), F32)],
                         axis=1)
    wr_hi = wr.astype(BF16)
    wr_lo = (wr - wr_hi.astype(F32)).astype(BF16)
    return {
        "norm1": norm1[l][None], "norm2": norm2[l][None], "g_qa": g_qa[l][None], "g_kva": g_kva[l][None],
        "win": jnp.concatenate(win_rows, axis=0).astype(BF16),
        "wuq": jnp.concatenate(blocks, axis=1).astype(BF16),
        "wuk": wuk.astype(BF16),
        "wuvt": wuvt.astype(BF16),
        "wout": wout.astype(BF16),
        "wr": jnp.concatenate([wr_hi, wr_lo], axis=1),
    }


def _rope_tables():
    n_rows = DEC_SEQ // GRID_W
    lane = np.arange(LANES)

    def tab(d, used_lanes):
        hh = d // 4
        i = lane % d
        freq = ROPE_THETA ** (-jnp.asarray(i % hh, F32) / hh)
        by_row = jnp.asarray((i // (2 * hh)) == 0)[None, None, :]
        valid = jnp.asarray(lane < used_lanes)[None, None, :]
        ang_r = jnp.arange(n_rows, dtype=F32)[:, None] * freq[None, :]
        ang_c = jnp.arange(GRID_W, dtype=F32)[:, None] * freq[None, :]

        def expand(fn):
            t = jnp.where(by_row, fn(ang_r)[:, None, :], fn(ang_c)[None, :, :])
            return jnp.where(valid, t, 0.0).reshape(DEC_SEQ, LANES)

        return expand(jnp.cos), expand(jnp.sin)

    c64, s64 = tab(HEAD_DIM, LANES)
    c32, s32 = tab(MLA_ROPE, 2 * MLA_ROPE)
    return c64, s64, c32, s32


def kernel(x_prompt, x_sample, cache_mla_ckv, cache_mla_kpe, cache_swa_k, cache_swa_v, cache_na_k, cache_na_v, c, c_ctx, w_mod, b_mod, norm1, norm2, w_in, g_qa, w_uq, g_kva, w_ukv, swa_sink, na_rpb, w_out, w_router_grp, w_router_exp, w_gate, w_up, w_down, norm_final):
    cpad = jnp.concatenate([c_ctx[None], c, jnp.zeros((8 - 1 - DEC_BATCH, D_MODEL), F32)], axis=0)
    mod = _modulation(cpad, w_mod, b_mod).reshape(DEPTH, 8, N_MOD, D_MODEL)
    tabs = _rope_tables()
    nf = norm_final[None]
    xp = x_prompt.reshape(N_CTX_TOK, D_MODEL)
    xs = x_sample.reshape(N_LAT_TOK, D_MODEL)
    caches = [[] for _ in range(6)]
    moe_w = (w_gate.astype(BF16), w_up.astype(BF16), w_down.astype(BF16))
    for l in range(DEPTH):
        lw = _layer_weights(l, norm1, norm2, w_in, g_qa, w_uq, g_kva, w_ukv, w_out, w_router_grp, w_router_exp)
        final = l == DEPTH - 1
        outs = _projections(xp, mod[l], lw, None, rope=False)
        om, osw, ona = _context_attention(swa_sink[l], *outs[:9])
        for dst, a in zip(caches, outs[9:]):
            dst.append(a)
        x1, h2, gates = _tail(xp, om, osw, ona, mod[l], lw, latent=False)
        xp = _moe(h2, gates, x1, mod[l], l, moe_w, nf, latent=False, final=final)
        qm, km, vm, qs, ks, vs, qn, kn, vn = _projections(xs, mod[l], lw, tabs, rope=True)
        kpe_c = cache_mla_kpe[:, l].reshape(DEC_BATCH * PAST_LEN, MLA_ROPE)
        kpe_dup = jnp.concatenate([kpe_c, kpe_c, jnp.zeros((DEC_BATCH * PAST_LEN, 64), F32)], axis=1)
        kmc, vmc = _expand_cached_mla(cache_mla_ckv[:, l].reshape(DEC_BATCH * PAST_LEN, MLA_KV_LORA), kpe_dup,
                                      lw["wuk"], lw["wuvt"])
        om = _latent_mla(qm, km, vm, kmc, vmc)
        flat = lambda a: a[:, l].reshape(DEC_BATCH * PAST_LEN, -1)
        flat_t = lambda a: a[:, l].reshape(DEC_BATCH, PAST_LEN, -1).transpose(0, 2, 1).reshape(-1, PAST_LEN)
        osw = _latent_swa(swa_sink[l], qs, ks, vs, flat(cache_swa_k), flat_t(cache_swa_v))
        ona = _latent_na(na_rpb[l], qn, kn, vn, flat(cache_na_k), flat_t(cache_na_v))
        x1, h2, gates = _tail(xs, om, osw, ona, mod[l], lw, latent=True)
        xs = _moe(h2, gates, x1, mod[l], l, moe_w, nf, latent=True, final=final)
    def heads_last(parts, n_heads):
        a = jnp.stack(parts, axis=1)
        if n_heads is None:
            return a.transpose(0, 1, 3, 2)
        return a.reshape(BATCH, DEPTH, n_heads, -1, SEQ).transpose(0, 1, 4, 2, 3)

    return (xp.reshape(BATCH, SEQ, D_MODEL), xs.reshape(DEC_BATCH, DEC_SEQ, D_MODEL),
            jnp.stack([p.reshape(BATCH, SEQ, MLA_KV_LORA) for p in caches[0]], axis=1), heads_last(caches[1], None),
            heads_last(caches[2], SWA_KV_HEADS), heads_last(caches[3], SWA_KV_HEADS),
            heads_last(caches[4], NA_HEADS), heads_last(caches[5], NA_HEADS))
```

```python
import functools

import jax
import jax.numpy as jnp
import numpy as np
from jax import lax
from jax.experimental import pallas as pl
from jax.experimental.pallas import tpu as pltpu

D_MODEL = 1024
BATCH = 32
SEQ = 256
DEPTH = 2
DEC_BATCH = 2
DEC_SEQ = 4096
PAST_LEN = 512
GRID_W = 64
HEAD_DIM = 64
MLA_HEADS = 8
MLA_Q_LORA = 256
MLA_KV_LORA = 128
MLA_NOPE = 64
MLA_ROPE = 32
MLA_V = 64
SWA_HEADS = 4
SWA_KV_HEADS = 2
SWA_WINDOW = 128
NA_HEADS = 4
NA_KH = 8
NA_KW = 16
N_GROUPS = 4
EXPERTS_PER_GROUP = 4
N_EXPERTS = 16
EXPERT_FF = 256
N_MOD = 6
ROPE_THETA = 10000.0
EPS = 1e-6
MASK_VALUE = -1e30
MLA_SCALE = (MLA_NOPE + MLA_ROPE) ** -0.5
HEAD_SCALE = HEAD_DIM ** -0.5
LOG2E = 1.4426950408889634

LANES = 128
N_PAIRS = MLA_HEADS // 2
MLA_QK_BLK = 2 * LANES
N_CTX_TOK = BATCH * SEQ
N_LAT_TOK = DEC_BATCH * DEC_SEQ

_C_CQ, _C_CKV, _C_QS, _C_KS, _C_QN, _C_KN, _C_KPE, _C_V, _C_END = 0, 256, 384, 640, 768, 1024, 1280, 1408, 1792
N_V_FEAT = _C_END - _C_V

TM_PROJ = 1024
TM_TAIL = 1024
TAIL_SUB_ROWS = 128
TM_MOE = 1024
LAT_TILES_PER_STEP = 4
CTX_BATCH_PER_STEP = 4
TQ = 256
TQ_MLA = 512
MLA_PAIRS_PER_STEP = 4
KV_CHUNK = 2048
MLA_KV_CHUNK = 512
SCORES_AHEAD = 2
ONES_ROWS = 16
SWA_KWIN = TQ + 2 * SWA_WINDOW
NA_ROWS_PER_TILE = TQ // GRID_W
NA_KEY_ROWS = 12
NA_KWIN = NA_KEY_ROWS * GRID_W
VMEM_LIMIT = 56 * 1024 * 1024

F32 = jnp.float32
BF16 = jnp.bfloat16


def _dot(a, b):
    return jnp.dot(a, b, preferred_element_type=F32)


def _dot_nt(a, b):
    return lax.dot_general(a, b, (((1,), (1,)), ((), ())), preferred_element_type=F32)


def _rms(x, g):
    return x * lax.rsqrt(jnp.mean(x * x, axis=-1, keepdims=True) + EPS) * g


def _cparams(sem):
    return pltpu.CompilerParams(dimension_semantics=sem, vmem_limit_bytes=VMEM_LIMIT)


N_COND = 1 + DEC_BATCH


def _mod_kernel(ct_ref, w_ref, b_ref, o_ref):
    ct = ct_ref[...]
    s = ct * (1.0 / (1.0 + jnp.exp(-ct)))
    w = w_ref[0]
    rows = [jnp.sum(w * s[:, r:r + 1], axis=0, keepdims=True) for r in range(N_COND)]
    rows.append(jnp.zeros((8 - N_COND, w.shape[1]), F32))
    o_ref[0] = jnp.concatenate(rows, axis=0) + b_ref[0]


def _modulation(cpad, w_mod, b_mod):
    nt = 1024
    return pl.pallas_call(
        _mod_kernel,
        grid=(DEPTH, N_MOD * D_MODEL // nt),
        in_specs=[
            pl.BlockSpec((D_MODEL, 8), lambda l, n: (0, 0)),
            pl.BlockSpec((1, D_MODEL, nt), lambda l, n: (l, 0, n)),
            pl.BlockSpec((1, 1, nt), lambda l, n: (l, 0, n)),
        ],
        out_specs=pl.BlockSpec((1, 8, nt), lambda l, n: (l, 0, n)),
        out_shape=jax.ShapeDtypeStruct((DEPTH, 8, N_MOD * D_MODEL), F32),
        compiler_params=_cparams(("arbitrary", "arbitrary")),
        name="modulation",
    )(cpad.T, w_mod, b_mod.reshape(DEPTH, 1, N_MOD * D_MODEL))


def _rope(x, d, cos, sin):
    hh = d // 4
    lane = lax.broadcasted_iota(jnp.int32, x.shape, 1)
    first_half = (lane % (2 * hh)) < hh
    rot = jnp.where(first_half, -pltpu.roll(x, LANES - hh, 1), pltpu.roll(x, hh, 1))
    return x * cos + rot * sin


def _proj_kernel(rope, n_handed_on, *refs):
    refs = refs[:N_PROJ_IN] + refs[N_PROJ_IN + n_handed_on:]
    x_ref, mod_ref, n1_ref, win_ref = refs[:4]
    m = mod_ref[0]
    n_sub = x_ref.shape[0] // SEQ
    ps = []
    for sub in range(n_sub):
        h = _rms(x_ref[sub * SEQ:(sub + 1) * SEQ, :], n1_ref[...]) * (1.0 + m[1:2]) + m[0:1]
        hb = h.astype(BF16)
        ps.append((_dot_nt(hb, win_ref[0:_C_V, :]), _dot_nt(win_ref[_C_V:_C_END, :], hb)))
    for sub in range(n_sub):
        _proj_rows(sub, rope, ps[sub][0], ps[sub][1], refs)


def _proj_rows(sub, rope, p, vt, refs):
    if rope:
        (x_ref, mod_ref, n1_ref, win_ref, gqa_ref, wuq_ref, gkva_ref, wuk_ref, wuvt_ref,
         c64_ref, s64_ref, c32_ref, s32_ref,
         qm_ref, km_ref, vmt_ref, qs_ref, ks_ref, vs_ref, qn_ref, kn_ref, vn_ref) = refs
    else:
        (x_ref, mod_ref, n1_ref, win_ref, gqa_ref, wuq_ref, gkva_ref, wuk_ref, wuvt_ref,
         qm_ref, km_ref, vmt_ref, qs_ref, ks_ref, vs_ref, qn_ref, kn_ref, vn_ref,
         ckv_o, kpe_o, ks_o, vs_o, kn_o, vn_o) = refs
    r = slice(sub * SEQ, (sub + 1) * SEQ)
    qm = _dot(_rms(p[:, _C_CQ:_C_CKV], gqa_ref[...]).astype(BF16), wuq_ref[...])
    ckv = _rms(p[:, _C_CKV:_C_QS], gkva_ref[...])
    ckv_b = ckv.astype(BF16)
    kn_mla = _dot(ckv_b, wuk_ref[...])
    vmt_ref[:, r] = _dot_nt(wuvt_ref[...], ckv_b).astype(BF16)
    qs = p[:, _C_QS:_C_KS]
    ks = p[:, _C_KS:_C_QN]
    kpe = p[:, _C_KPE:_C_V]
    if rope:
        c64, s64, c32, s32 = c64_ref[r, :], s64_ref[r, :], c32_ref[r, :], s32_ref[r, :]
        qs = jnp.concatenate([_rope(qs[:, j * LANES:(j + 1) * LANES], HEAD_DIM, c64, s64) for j in range(2)], axis=1)
        ks = _rope(ks, HEAD_DIM, c64, s64)
        kpe = _rope(kpe, MLA_ROPE, c32, s32)
    for i in range(N_PAIRS):
        lo = i * MLA_QK_BLK
        qrope = qm[:, lo + LANES:lo + MLA_QK_BLK]
        if rope:
            qrope = _rope(qrope, MLA_ROPE, c32, s32)
        qm_ref[r, lo:lo + LANES] = qm[:, lo:lo + LANES].astype(BF16)
        qm_ref[r, lo + LANES:lo + MLA_QK_BLK] = qrope.astype(BF16)
        km_ref[r, lo:lo + LANES] = kn_mla[:, i * LANES:(i + 1) * LANES].astype(BF16)
        km_ref[r, lo + LANES:lo + MLA_QK_BLK] = kpe.astype(BF16)
    qs_ref[r, :] = qs.astype(BF16)
    ks_ref[r, :] = ks.astype(BF16)
    vs_ref[:, r] = vt[0:128, :].astype(BF16)
    qn_ref[r, :] = p[:, _C_QN:_C_KN].astype(BF16)
    kn_ref[r, :] = p[:, _C_KN:_C_KPE].astype(BF16)
    vn_ref[:, r] = vt[128:N_V_FEAT, :].astype(BF16)
    if not rope:
        ckv_o[sub] = ckv
        vs_o[sub] = vt[0:128, :]
        vn_o[sub] = vt[128:N_V_FEAT, :]

        def put_feature_major(o_ref, val, n_feat):
            o_ref[sub] = val.T[:n_feat, :]

        put_feature_major(kpe_o, kpe, MLA_ROPE)
        put_feature_major(ks_o, ks, 128)
        put_feature_major(kn_o, p[:, _C_KN:_C_KPE], 256)


N_PROJ_IN = 9


def _projections(x, mod_l, lw, tabs, rope, layer=None, new_caches=()):
    n_tok = x.shape[0]
    tm = TM_PROJ
    tiles_per_seq = DEC_SEQ // tm
    win, wuq = lw["win"], lw["wuq"]
    mod_idx = (lambda i: (1 + i // tiles_per_seq, 0, 0)) if rope else (lambda i: (0, 0, 0))
    row = lambda w: pl.BlockSpec((tm, w), lambda i: (i, 0))
    whole = lambda a: pl.BlockSpec(a.shape, lambda i: (0,) * a.ndim)
    in_specs = [row(D_MODEL), pl.BlockSpec((1, N_MOD, D_MODEL), mod_idx), whole(lw["norm1"]), whole(win),
                whole(lw["g_qa"]), whole(wuq), whole(lw["g_kva"]), whole(lw["wuk"]), whole(lw["wuvt"])]
    args = [x, mod_l, lw["norm1"], win, lw["g_qa"], wuq, lw["g_kva"], lw["wuk"], lw["wuvt"]]
    outs = [(N_PAIRS * MLA_QK_BLK, False), (N_PAIRS * MLA_QK_BLK, False), (N_PAIRS * LANES, True), (256, False),
            (128, False), (128, True), (256, False), (256, False), (256, True)]
    out_specs = [pl.BlockSpec((w, tm), lambda i: (0, i)) if fm else row(w) for w, fm in outs]
    out_shape = [jax.ShapeDtypeStruct((w, n_tok) if fm else (n_tok, w), BF16) for w, fm in outs]
    aliases = {}
    if rope:
        tab_spec = pl.BlockSpec((tm, LANES), lambda i: (i % tiles_per_seq, 0))
        in_specs += [tab_spec] * 4
        args += list(tabs)
    else:
        for dims in [(SEQ, MLA_KV_LORA)] + [(w, SEQ) for w in [MLA_ROPE, 128, 128, 256, 256]]:
            out_specs.append(pl.BlockSpec((tm // SEQ, None) + dims, lambda i: (i, layer, 0, 0)))
            out_shape.append(jax.ShapeDtypeStruct((n_tok // SEQ, DEPTH) + dims, F32))
        aliases = {len(args) + k: len(outs) + k for k in range(len(new_caches))}
        in_specs += [pl.BlockSpec(memory_space=pl.ANY)] * len(new_caches)
        args += list(new_caches)
    return pl.pallas_call(
        functools.partial(_proj_kernel, rope, len(new_caches)),
        grid=(n_tok // tm,),
        in_specs=in_specs,
        out_specs=out_specs,
        out_shape=out_shape,
        input_output_aliases=aliases,
        compiler_params=_cparams(("arbitrary",)),
        name="proj_lat" if rope else "proj_ctx",
    )(*args)


def _ctxkv_kernel(ckv_ref, kpe_ref, wuk_ref, wuvt_ref, km_ref, vmt_ref):
    ckv_b = ckv_ref[...].astype(BF16)
    kn_mla = _dot(ckv_b, wuk_ref[...])
    kpe = kpe_ref[...].astype(BF16)
    for i in range(N_PAIRS):
        lo = i * MLA_QK_BLK
        km_ref[:, lo:lo + LANES] = kn_mla[:, i * LANES:(i + 1) * LANES].astype(BF16)
        km_ref[:, lo + LANES:lo + MLA_QK_BLK] = kpe
    vmt_ref[...] = _dot_nt(wuvt_ref[...], ckv_b).astype(BF16)


def _expand_cached_mla(ckv_c, kpe_dup, wuk, wuvt):
    n = ckv_c.shape[0]
    whole = lambda a: pl.BlockSpec(a.shape, lambda i: (0,) * a.ndim)
    return pl.pallas_call(
        _ctxkv_kernel,
        grid=(1,),
        in_specs=[whole(ckv_c), whole(kpe_dup), whole(wuk), whole(wuvt)],
        out_specs=[pl.BlockSpec((n, N_PAIRS * MLA_QK_BLK), lambda i: (0, 0)),
                   pl.BlockSpec((N_PAIRS * LANES, n), lambda i: (0, 0))],
        out_shape=[jax.ShapeDtypeStruct((n, N_PAIRS * MLA_QK_BLK), BF16),
                   jax.ShapeDtypeStruct((N_PAIRS * LANES, n), BF16)],
        compiler_params=_cparams(("arbitrary",)),
        name="expand_cached_mla",
    )(ckv_c, kpe_dup, wuk, wuvt)


def _pair_masks(width):
    lane = lax.broadcasted_iota(jnp.int32, (1, width), 1)
    if width == LANES:
        return [lane < HEAD_DIM, lane >= HEAD_DIM]
    m0 = (lane < MLA_NOPE) | ((lane >= LANES) & (lane < LANES + MLA_ROPE))
    m1 = ((lane >= MLA_NOPE) & (lane < LANES)) | ((lane >= LANES + MLA_ROPE) & (lane < LANES + 2 * MLA_ROPE))
    return [m0, m1]


def _attention(jobs):
    steps, qa = [], []
    for j, (q, blocks, _) in enumerate(jobs):
        masks = _pair_masks(q.shape[1])
        qa.append([jnp.where(masks[a], q, jnp.zeros_like(q)) for a in range(2)])
        chunks = []
        for blk in blocks:
            k, vt, post = blk[:3]
            cuts = blk[3] if len(blk) > 3 else list(range(0, k.shape[0], KV_CHUNK)) + [k.shape[0]]
            chunks += [(k, vt, post, lo, hi) for lo, hi in zip(cuts[:-1], cuts[1:])]
        steps += [(j, c, ci == len(chunks) - 1) for ci, c in enumerate(chunks)]

    def scores(step):
        j, (k, _, post, lo, hi), _ = step
        kc = k[lo:hi, :]
        s = [_dot_nt(kc, qa[j][a]) for a in range(2)]
        return s if post is None else [post(a, s[a], lo, hi) for a in range(2)]

    outs = [None] * len(jobs)
    m, acc = [None, None], [None, None]
    pending = [scores(st) for st in steps[:SCORES_AHEAD]]
    for n, (j, (_, vt, _, lo, hi), last) in enumerate(steps):
        if n + SCORES_AHEAD < len(steps):
            pending.append(scores(steps[n + SCORES_AHEAD]))
        s_cur = pending.pop(0)
        ones = jnp.ones((ONES_ROWS, hi - lo), BF16)
        for a in range(2):
            vta = jnp.concatenate([vt[a * HEAD_DIM:(a + 1) * HEAD_DIM, lo:hi], ones], axis=0)
            s = s_cur[a]
            mc = jnp.max(s, axis=0, keepdims=True)
            mn = mc if m[a] is None else jnp.maximum(m[a], mc)
            pv = _dot(vta, jnp.exp2(s - mn).astype(BF16))
            acc[a] = pv if m[a] is None else jnp.exp2(m[a] - mn) * acc[a] + pv
            m[a] = mn
        if last:
            sinks, heads = jobs[j][2], []
            for a in range(2):
                l = acc[a][HEAD_DIM:HEAD_DIM + 1, :]
                if sinks[a] is not None:
                    mf = jnp.maximum(m[a], sinks[a])
                    scale = jnp.exp2(m[a] - mf)
                    l = scale * l + jnp.exp2(sinks[a] - mf)
                    heads.append(acc[a][:HEAD_DIM, :] * (scale / l))
                else:
                    heads.append(acc[a][:HEAD_DIM, :] / l)
            outs[j] = jnp.concatenate(heads, axis=0).T
            m, acc = [None, None], [None, None]
    return outs


_NO_SINKS = (None, None)


def _ctx_attn_kernel(sink_ref, qm_ref, km_ref, vmt_ref, qs_ref, ks_ref, vs_ref, qn_ref, kn_ref, vn_ref,
                     om_ref, os_ref, on_ref):
    jobs = []
    for bb in range(CTX_BATCH_PER_STEP):
        r = slice(bb * SEQ, (bb + 1) * SEQ)
        for i in range(N_PAIRS):
            q = qm_ref[r, i * MLA_QK_BLK:(i + 1) * MLA_QK_BLK]
            k = km_ref[r, i * MLA_QK_BLK:(i + 1) * MLA_QK_BLK]
            vt = vmt_ref[i * LANES:(i + 1) * LANES, r]
            jobs.append((q, [(k, vt, None)], _NO_SINKS))
        ks, vst = ks_ref[r, :], vs_ref[:, r]
        for j in range(2):
            sinks = (sink_ref[j] * LOG2E, sink_ref[j + 2] * LOG2E)
            jobs.append((qs_ref[r, j * LANES:(j + 1) * LANES], [(ks, vst, None)], sinks))
        for j in range(2):
            sl = slice(j * LANES, (j + 1) * LANES)
            jobs.append((qn_ref[r, sl], [(kn_ref[r, sl], vn_ref[sl, r], None)], _NO_SINKS))
    outs = _attention(jobs)
    per_batch = N_PAIRS + 4
    for bb in range(CTX_BATCH_PER_STEP):
        r = slice(bb * SEQ, (bb + 1) * SEQ)
        o = outs[bb * per_batch:(bb + 1) * per_batch]
        for i in range(N_PAIRS):
            om_ref[r, i * LANES:(i + 1) * LANES] = o[i].astype(BF16)
        for j in range(2):
            os_ref[r, j * LANES:(j + 1) * LANES] = o[N_PAIRS + j].astype(BF16)
            on_ref[r, j * LANES:(j + 1) * LANES] = o[N_PAIRS + 2 + j].astype(BF16)


def _context_attention(sink_l, qm, km, vmt, qs, ks, vs, qn, kn, vn):
    rows = CTX_BATCH_PER_STEP * SEQ
    row = lambda a: (pl.BlockSpec((a.shape[0], rows), lambda b: (0, b)) if any(a is v for v in (vmt, vs, vn))
                     else pl.BlockSpec((rows, a.shape[1]), lambda b: (b, 0)))
    ins = [qm, km, vmt, qs, ks, vs, qn, kn, vn]
    widths = [N_PAIRS * LANES, 256, 256]
    return pl.pallas_call(
        _ctx_attn_kernel,
        grid=(BATCH // CTX_BATCH_PER_STEP,),
        in_specs=[pl.BlockSpec(memory_space=pltpu.SMEM)] + [row(a) for a in ins],
        out_specs=[pl.BlockSpec((rows, w), lambda b: (b, 0)) for w in widths],
        out_shape=[jax.ShapeDtypeStruct((N_CTX_TOK, w), BF16) for w in widths],
        compiler_params=_cparams(("arbitrary",)),
        name="attn_ctx",
    )(sink_l, *ins)


def _lat_mla_kernel(q_ref, kl_ref, vlt_ref, kc_ref, vct_ref, o_ref):
    jobs = []
    for i in range(MLA_PAIRS_PER_STEP):
        qk, v = slice(i * MLA_QK_BLK, (i + 1) * MLA_QK_BLK), slice(i * LANES, (i + 1) * LANES)
        blocks = [(kc_ref.at[:, qk], vct_ref.at[v, :], None),
                  (kl_ref.at[:, qk], vlt_ref.at[v, :], None, list(range(0, DEC_SEQ + 1, MLA_KV_CHUNK)))]
        jobs.append((q_ref[:, qk], blocks, _NO_SINKS))
    for i, o in enumerate(_attention(jobs)):
        o_ref[:, i * LANES:(i + 1) * LANES] = o.astype(BF16)


def _latent_mla(qm, km, vmt, kmc, vmct):
    nq = DEC_SEQ // TQ_MLA
    pp = MLA_PAIRS_PER_STEP
    return pl.pallas_call(
        _lat_mla_kernel,
        grid=(DEC_BATCH, N_PAIRS // pp, nq),
        in_specs=[
            pl.BlockSpec((TQ_MLA, pp * MLA_QK_BLK), lambda b, i, t: (b * nq + t, i)),
            pl.BlockSpec((DEC_SEQ, pp * MLA_QK_BLK), lambda b, i, t: (b, i)),
            pl.BlockSpec((pp * LANES, DEC_SEQ), lambda b, i, t: (i, b)),
            pl.BlockSpec((PAST_LEN, pp * MLA_QK_BLK), lambda b, i, t: (b, i)),
            pl.BlockSpec((pp * LANES, PAST_LEN), lambda b, i, t: (i, b)),
        ],
        out_specs=pl.BlockSpec((TQ_MLA, pp * LANES), lambda b, i, t: (b * nq + t, i)),
        out_shape=jax.ShapeDtypeStruct((N_LAT_TOK, N_PAIRS * LANES), BF16),
        compiler_params=_cparams(("arbitrary", "arbitrary", "arbitrary")),
        name="attn_lat_mla",
    )(qm, km, vmt, kmc, vmct)


def _lat_swa_kernel(sink_ref, q_ref, k_ref, v_ref, kc_ref, vc_ref, o_ref):
    kc = kc_ref[...].astype(BF16)
    vc = vc_ref[...].astype(BF16)
    jobs = []
    for u in range(LAT_TILES_PER_STEP):
        q0 = (pl.program_id(1) * LAT_TILES_PER_STEP + u) * TQ
        ws = pl.multiple_of(jnp.clip(q0 - SWA_WINDOW, 0, DEC_SEQ - SWA_KWIN), SWA_WINDOW)
        kw = k_ref[pl.ds(ws, SWA_KWIN), :]
        vw = v_ref[:, pl.ds(ws, SWA_KWIN)]
        kpos = ws + lax.broadcasted_iota(jnp.int32, (SWA_KWIN, TQ), 0)
        qpos = q0 + lax.broadcasted_iota(jnp.int32, (SWA_KWIN, TQ), 1)
        in_band = jnp.abs(kpos - qpos) <= SWA_WINDOW

        def band(a, s, lo, hi, in_band=in_band):
            return jnp.where(in_band[lo:hi, :], s, MASK_VALUE)

        for j in range(2):
            sinks = (sink_ref[j] * LOG2E, sink_ref[j + 2] * LOG2E)
            q = q_ref[u * TQ:(u + 1) * TQ, j * LANES:(j + 1) * LANES]
            jobs.append((q, [(kw, vw, band), (kc, vc, None)], sinks))
    for n, o in enumerate(_attention(jobs)):
        u, j = divmod(n, 2)
        o_ref[u * TQ:(u + 1) * TQ, j * LANES:(j + 1) * LANES] = o.astype(BF16)


def _latent_swa(sink_l, qs, ks, vs, ksc, vsc):
    rows = LAT_TILES_PER_STEP * TQ
    nq = DEC_SEQ // rows
    return pl.pallas_call(
        _lat_swa_kernel,
        grid=(DEC_BATCH, nq),
        in_specs=[
            pl.BlockSpec(memory_space=pltpu.SMEM),
            pl.BlockSpec((rows, 256), lambda b, t: (b * nq + t, 0)),
            pl.BlockSpec((DEC_SEQ, LANES), lambda b, t: (b, 0)),
            pl.BlockSpec((LANES, DEC_SEQ), lambda b, t: (0, b)),
            pl.BlockSpec((PAST_LEN, LANES), lambda b, t: (b, 0)),
            pl.BlockSpec((LANES, PAST_LEN), lambda b, t: (b, 0)),
        ],
        out_specs=pl.BlockSpec((rows, 256), lambda b, t: (b * nq + t, 0)),
        out_shape=jax.ShapeDtypeStruct((N_LAT_TOK, 256), BF16),
        compiler_params=_cparams(("arbitrary", "arbitrary")),
        name="attn_lat_swa",
    )(sink_l, qs, ks, vs, ksc, vsc)


_NA_DR = 2 * NA_KH - 1


def _build_na_bias_tiles(rpb_ref, tile_ref):
    kc = lax.broadcasted_iota(jnp.int32, (GRID_W, LANES), 0)
    qc = lax.broadcasted_iota(jnp.int32, (GRID_W, LANES), 1) % GRID_W
    rel = jnp.clip(kc - qc, -(NA_KW - 1), NA_KW - 1) + NA_KW - 1
    col_start = jnp.clip(qc - NA_KW // 2, 0, GRID_W - NA_KW)
    col_ok = (kc >= col_start) & (kc < col_start + NA_KW)
    masked = jnp.full((GRID_W, LANES), MASK_VALUE, F32)
    n_rel = 2 * NA_KW - 1

    def one_tile(hd, carry):
        tile = masked
        for c in range(n_rel):
            tile = jnp.where(col_ok & (rel == c), rpb_ref[hd * n_rel + c] * LOG2E, tile)
        tile_ref[(hd // _NA_DR) * (_NA_DR + 1) + hd % _NA_DR] = tile
        return carry

    lax.fori_loop(0, NA_HEADS * _NA_DR, one_tile, 0)
    for h in range(NA_HEADS):
        tile_ref[h * (_NA_DR + 1) + _NA_DR] = masked


def _na_tile_kinds():
    n_rows = DEC_SEQ // GRID_W
    return (0, NA_ROWS_PER_TILE, n_rows - NA_ROWS_PER_TILE)


def _assemble_na_bias(tile_ref, bias_ref):
    n_rows = DEC_SEQ // GRID_W
    lane_lo = lax.broadcasted_iota(jnp.int32, (GRID_W, LANES), 1) < GRID_W
    for kind, r0 in enumerate(_na_tile_kinds()):
        ws_row = int(np.clip(r0 - NA_KH // 2, 0, n_rows - NA_KEY_ROWS))

        def tile(h, a_q, jj):
            r, kr = r0 + a_q, ws_row + jj
            rs = int(np.clip(r - NA_KH // 2, 0, n_rows - NA_KH))
            return tile_ref[h * (_NA_DR + 1) + (kr - r + NA_KH - 1 if rs <= kr < rs + NA_KH else _NA_DR)]

        for h in range(NA_HEADS):
            for jj in range(NA_KEY_ROWS):
                for u in range(NA_ROWS_PER_TILE // 2):
                    bias_ref[kind * NA_HEADS + h, jj * GRID_W:(jj + 1) * GRID_W, u * LANES:(u + 1) * LANES] = (
                        jnp.where(lane_lo, tile(h, 2 * u, jj), tile(h, 2 * u + 1, jj)))


def _lat_na_kernel(rpb_ref, q_ref, k_ref, v_ref, kc_ref, vc_ref, o_ref, tile_ref, bias_ref):
    t = pl.program_id(1)

    @pl.when((pl.program_id(0) == 0) & (t == 0))
    def _():
        _build_na_bias_tiles(rpb_ref, tile_ref)
        _assemble_na_bias(tile_ref, bias_ref)

    n_rows = DEC_SEQ // GRID_W
    last_tile = DEC_SEQ // TQ - 1
    jobs = []
    for u in range(LAT_TILES_PER_STEP):
        tile = t * LAT_TILES_PER_STEP + u
        r0 = tile * NA_ROWS_PER_TILE
        ws_row = jnp.clip(r0 - NA_KH // 2, 0, n_rows - NA_KEY_ROWS)
        ws = pl.multiple_of(ws_row * GRID_W, NA_ROWS_PER_TILE * GRID_W)
        kind = jnp.where(tile == 0, 0, jnp.where(tile == last_tile, 2, 1))
        for j in range(2):
            sl = slice(j * LANES, (j + 1) * LANES)
            kw = k_ref[pl.ds(ws, NA_KWIN), sl]
            vw = v_ref[sl, pl.ds(ws, NA_KWIN)]
            kc = kc_ref[:, sl].astype(BF16)
            vc = vc_ref[sl, :].astype(BF16)

            def add_bias(a, s, lo, hi, j=j, kind=kind):
                return s + bias_ref[kind * NA_HEADS + 2 * j + a, lo:hi, :]

            jobs.append((q_ref[u * TQ:(u + 1) * TQ, sl], [(kw, vw, add_bias), (kc, vc, None)], _NO_SINKS))
    for n, o in enumerate(_attention(jobs)):
        u, j = divmod(n, 2)
        o_ref[u * TQ:(u + 1) * TQ, j * LANES:(j + 1) * LANES] = o.astype(BF16)


def _latent_na(rpb_l, qn, kn, vn, knc, vnc):
    rows = LAT_TILES_PER_STEP * TQ
    nq = DEC_SEQ // rows
    return pl.pallas_call(
        _lat_na_kernel,
        grid=(DEC_BATCH, nq),
        in_specs=[
            pl.BlockSpec(memory_space=pltpu.SMEM),
            pl.BlockSpec((rows, 256), lambda b, t: (b * nq + t, 0)),
            pl.BlockSpec((DEC_SEQ, 256), lambda b, t: (b, 0)),
            pl.BlockSpec((256, DEC_SEQ), lambda b, t: (0, b)),
            pl.BlockSpec((PAST_LEN, 256), lambda b, t: (b, 0)),
            pl.BlockSpec((256, PAST_LEN), lambda b, t: (b, 0)),
        ],
        out_specs=pl.BlockSpec((rows, 256), lambda b, t: (b * nq + t, 0)),
        out_shape=jax.ShapeDtypeStruct((N_LAT_TOK, 256), BF16),
        scratch_shapes=[pltpu.VMEM((NA_HEADS * (_NA_DR + 1), GRID_W, LANES), F32),
                        pltpu.VMEM((3 * NA_HEADS, NA_KWIN, TQ), F32)],
        compiler_params=_cparams(("arbitrary", "arbitrary")),
        name="attn_lat_na",
    )(rpb_l.reshape(-1), qn, kn, vn, knc, vnc)


_GRP_LANE0 = 0
_EXP_LANE0 = N_GROUPS


def _lane_first_max(x, valid, lane):
    xm = jnp.where(valid, x, -jnp.inf)
    mx = jnp.max(xm, axis=-1, keepdims=True)
    idx = jnp.min(jnp.where(valid & (xm == mx), lane, LANES), axis=-1, keepdims=True)
    return mx, idx


def _tail_kernel(x_ref, om_ref, os_ref, on_ref, mod_ref, n2_ref, wo_ref, wr_ref, x1_ref, h2_ref, gates_ref):
    n_sub = x_ref.shape[0] // TAIL_SUB_ROWS
    subs = [slice(i * TAIL_SUB_ROWS, (i + 1) * TAIL_SUB_ROWS) for i in range(n_sub)]
    wo = wo_ref
    attn = [(_dot(om_ref[r, :], wo[0:512, :]) + _dot(os_ref[r, :], wo[512:768, :]) + _dot(on_ref[r, :], wo[768:1024, :]))
            for r in subs]
    for r, attn_r in zip(subs, attn):
        _tail_rows(r, attn_r, x_ref, mod_ref, n2_ref, wr_ref, x1_ref, h2_ref, gates_ref)


def _tail_rows(r, attn, x_ref, mod_ref, n2_ref, wr_ref, x1_ref, h2_ref, gates_ref):
    m = mod_ref[0]
    x1 = x_ref[r, :] + m[2:3] * attn
    x1_ref[r, :] = x1
    h2 = _rms(x1, n2_ref[...]) * (1.0 + m[4:5]) + m[3:4]
    h2_ref[r, :] = h2.astype(BF16)
    hi = h2.astype(BF16)
    lo = (h2 - hi.astype(F32)).astype(BF16)
    a = _dot(hi, wr_ref[...])
    logits = a[:, :LANES] + a[:, LANES:] + _dot(lo, wr_ref[:, :LANES])
    lane = lax.broadcasted_iota(jnp.int32, logits.shape, 1)
    is_grp = lane < N_GROUPS
    gmax, gidx = _lane_first_max(logits, is_grp, lane)
    gden = jnp.sum(jnp.where(is_grp, jnp.exp(logits - gmax), 0.0), axis=-1, keepdims=True)
    grp_gate = 1.0 / gden
    in_grp = (lane >= _EXP_LANE0) & (lane < _EXP_LANE0 + N_EXPERTS) & ((lane // EXPERTS_PER_GROUP - 1) == gidx)
    v1, i1 = _lane_first_max(logits, in_grp, lane)
    v2, i2 = _lane_first_max(logits, in_grp & (lane != i1), lane)
    e2 = jnp.exp(v2 - v1)
    w1 = grp_gate / (1.0 + e2)
    w2 = grp_gate * e2 / (1.0 + e2)
    gates_ref[r, :] = jnp.where(lane == i1, w1, 0.0) + jnp.where(lane == i2, w2, 0.0)


def _tail(x, om, osw, ona, mod_l, lw, latent):
    n_tok = x.shape[0]
    tm = TM_TAIL
    tiles_per_seq = DEC_SEQ // tm
    mod_idx = (lambda i: (1 + i // tiles_per_seq, 0, 0)) if latent else (lambda i: (0, 0, 0))
    row = lambda w: pl.BlockSpec((tm, w), lambda i: (i, 0))
    whole = lambda a: pl.BlockSpec(a.shape, lambda i: (0,) * a.ndim)
    return pl.pallas_call(
        _tail_kernel,
        grid=(n_tok // tm,),
        in_specs=[row(D_MODEL), row(512), row(256), row(256), pl.BlockSpec((1, N_MOD, D_MODEL), mod_idx),
                  whole(lw["norm2"]), whole(lw["wout"]), whole(lw["wr"])],
        out_specs=[row(D_MODEL), row(D_MODEL), row(LANES)],
        out_shape=[jax.ShapeDtypeStruct((n_tok, D_MODEL), F32), jax.ShapeDtypeStruct((n_tok, D_MODEL), BF16),
                   jax.ShapeDtypeStruct((n_tok, LANES), F32)],
        compiler_params=_cparams(("arbitrary",)),
        name="tail_lat" if latent else "tail_ctx",
    )(x, om, osw, ona, mod_l, lw["norm2"], lw["wout"], lw["wr"])


def _moe_kernel(final, h2_ref, gates_ref, x1_ref, mod_ref, nf_ref, wg_ref, wu_ref, wd_ref, o_ref, acc_ref):
    g = pl.program_id(1)
    h2 = h2_ref[...]
    gates = gates_ref[...]
    lane = lax.broadcasted_iota(jnp.int32, gates.shape, 1)
    acts = []
    for e in range(EXPERTS_PER_GROUP):
        ge = jnp.sum(jnp.where(lane == _EXP_LANE0 + g * EXPERTS_PER_GROUP + e, gates, 0.0), axis=-1, keepdims=True)
        hg = _dot(h2, wg_ref[0, e])
        hu = _dot(h2, wu_ref[0, e])
        acts.append((hg * (1.0 / (1.0 + jnp.exp(-hg))) * hu * ge).astype(BF16))
    contrib = _dot(jnp.concatenate(acts, axis=1), wd_ref[0].reshape(EXPERTS_PER_GROUP * EXPERT_FF, D_MODEL))

    @pl.when(g == 0)
    def _():
        acc_ref[...] = contrib

    @pl.when((g > 0) & (g < N_GROUPS - 1))
    def _():
        acc_ref[...] += contrib

    @pl.when(g == N_GROUPS - 1)
    def _():
        y = x1_ref[...] + mod_ref[0][5:6] * (acc_ref[...] + contrib)
        if final:
            y = _rms(y, nf_ref[...])
        o_ref[...] = y


def _moe(h2, gates, x1, mod_l, l, moe_w, norm_final, latent, final):
    n_tok = h2.shape[0]
    tm = TM_MOE
    tiles_per_seq = DEC_SEQ // tm
    mod_idx = (lambda i, g: (1 + i // tiles_per_seq, 0, 0)) if latent else (lambda i, g: (0, 0, 0))
    epg = EXPERTS_PER_GROUP
    return pl.pallas_call(
        functools.partial(_moe_kernel, final),
        grid=(n_tok // tm, N_GROUPS),
        in_specs=[
            pl.BlockSpec((tm, D_MODEL), lambda i, g: (i, 0)),
            pl.BlockSpec((tm, LANES), lambda i, g: (i, 0)),
            pl.BlockSpec((tm, D_MODEL), lambda i, g: (i, 0)),
            pl.BlockSpec((1, N_MOD, D_MODEL), mod_idx),
            pl.BlockSpec((1, D_MODEL), lambda i, g: (0, 0)),
            pl.BlockSpec((1, epg, D_MODEL, EXPERT_FF), lambda i, g: (l, g, 0, 0)),
            pl.BlockSpec((1, epg, D_MODEL, EXPERT_FF), lambda i, g: (l, g, 0, 0)),
            pl.BlockSpec((1, epg, EXPERT_FF, D_MODEL), lambda i, g: (l, g, 0, 0)),
        ],
        out_specs=pl.BlockSpec((tm, D_MODEL), lambda i, g: (i, 0)),
        out_shape=jax.ShapeDtypeStruct((n_tok, D_MODEL), F32),
        scratch_shapes=[pltpu.VMEM((tm, D_MODEL), F32)],
        compiler_params=_cparams(("arbitrary", "arbitrary")),
        name=("moe_lat" if latent else "moe_ctx") + ("_final" if final else ""),
    )(h2, gates, x1, mod_l, norm_final, *moe_w)


def _heads(w, d, order, axis):
    parts = [lax.slice_in_dim(w, h * d, (h + 1) * d, axis=axis) for h in order]
    return jnp.concatenate(parts, axis=axis)


def _layer_weights(l, norm1, norm2, w_in, g_qa, w_uq, g_kva, w_ukv, w_out, w_router_grp, w_router_exp):
    wi = w_in[l].T
    z64 = jnp.zeros((64, D_MODEL), F32)
    cq, ckv, kpe = wi[0:256], wi[256:384], wi[384:416]
    qs = _heads(wi[416:672], HEAD_DIM, (0, 2, 1, 3), 0) * (HEAD_SCALE * LOG2E)
    ks, vs = wi[672:800], wi[800:928]
    qn, kn, vn = wi[928:1184] * (HEAD_SCALE * LOG2E), wi[1184:1440], wi[1440:1696]
    kped = jnp.concatenate([kpe, kpe, z64], axis=0)
    win_rows = [cq, ckv, qs, ks, qn, kn, kped, vs, vn]
    wq = w_uq[l].reshape(MLA_Q_LORA, MLA_HEADS, MLA_NOPE + MLA_ROPE) * (MLA_SCALE * LOG2E)
    nope, ropew = wq[:, :, :MLA_NOPE], wq[:, :, MLA_NOPE:]
    z64q = jnp.zeros((MLA_Q_LORA, 64), F32)
    blocks = []
    for i in range(N_PAIRS):
        blocks += [nope[:, 2 * i], nope[:, 2 * i + 1], ropew[:, 2 * i], ropew[:, 2 * i + 1], z64q]
    wkv = w_ukv[l].reshape(MLA_KV_LORA, MLA_HEADS, MLA_NOPE + MLA_V)
    wuk = wkv[:, :, :MLA_NOPE].reshape(MLA_KV_LORA, -1)
    wuvt = wkv[:, :, MLA_NOPE:].reshape(MLA_KV_LORA, -1).T
    wo = w_out[l]
    wout = jnp.concatenate([wo[:512], _heads(wo[512:768], HEAD_DIM, (0, 2, 1, 3), 0), wo[768:]], axis=0)
    wr = jnp.concatenate([w_router_grp[l], w_router_exp[l], jnp.zeros((D_MODEL, LANES - N_GROUPS - N_EXPERTS), F32)],
                         axis=1)
    wr_hi = wr.astype(BF16)
    wr_lo = (wr - wr_hi.astype(F32)).astype(BF16)
    return {
        "norm1": norm1[l][None], "norm2": norm2[l][None], "g_qa": g_qa[l][None], "g_kva": g_kva[l][None],
        "win": jnp.concatenate(win_rows, axis=0).astype(BF16),
        "wuq": jnp.concatenate(blocks, axis=1).astype(BF16),
        "wuk": wuk.astype(BF16),
        "wuvt": wuvt.astype(BF16),
        "wout": wout.astype(BF16),
        "wr": jnp.concatenate([wr_hi, wr_lo], axis=1),
    }


def _rope_tables():
    n_rows = DEC_SEQ // GRID_W
    lane = np.arange(LANES)

    def tab(d, used_lanes):
        hh = d // 4
        i = lane % d
        freq = ROPE_THETA ** (-jnp.asarray(i % hh, F32) / hh)
        by_row = jnp.asarray((i // (2 * hh)) == 0)[None, None, :]
        valid = jnp.asarray(lane < used_lanes)[None, None, :]
        ang_r = jnp.arange(n_rows, dtype=F32)[:, None] * freq[None, :]
        ang_c = jnp.arange(GRID_W, dtype=F32)[:, None] * freq[None, :]

        def expand(fn):
            t = jnp.where(by_row, fn(ang_r)[:, None, :], fn(ang_c)[None, :, :])
            return jnp.where(valid, t, 0.0).reshape(DEC_SEQ, LANES)

        return expand(jnp.cos), expand(jnp.sin)

    c64, s64 = tab(HEAD_DIM, LANES)
    c32, s32 = tab(MLA_ROPE, 2 * MLA_ROPE)
    return c64, s64, c32, s32


def kernel(x_prompt, x_sample, cache_mla_ckv, cache_mla_kpe, cache_swa_k, cache_swa_v, cache_na_k, cache_na_v, c, c_ctx, w_mod, b_mod, norm1, norm2, w_in, g_qa, w_uq, g_kva, w_ukv, swa_sink, na_rpb, w_out, w_router_grp, w_router_exp, w_gate, w_up, w_down, norm_final):
    cpad = jnp.concatenate([c_ctx[None], c, jnp.zeros((8 - 1 - DEC_BATCH, D_MODEL), F32)], axis=0)
    mod = _modulation(cpad, w_mod, b_mod).reshape(DEPTH, 8, N_MOD, D_MODEL)
    tabs = _rope_tables()
    nf = norm_final[None]
    xp = x_prompt.reshape(N_CTX_TOK, D_MODEL)
    xs = x_sample.reshape(N_LAT_TOK, D_MODEL)
    new_caches = ()
    moe_w = (w_gate.astype(BF16), w_up.astype(BF16), w_down.astype(BF16))
    for l in range(DEPTH):
        lw = _layer_weights(l, norm1, norm2, w_in, g_qa, w_uq, g_kva, w_ukv, w_out, w_router_grp, w_router_exp)
        final = l == DEPTH - 1
        outs = _projections(xp, mod[l], lw, None, rope=False, layer=l, new_caches=new_caches)
        om, osw, ona = _context_attention(swa_sink[l], *outs[:9])
        new_caches = outs[9:]
        x1, h2, gates = _tail(xp, om, osw, ona, mod[l], lw, latent=False)
        xp = _moe(h2, gates, x1, mod[l], l, moe_w, nf, latent=False, final=final)
        qm, km, vm, qs, ks, vs, qn, kn, vn = _projections(xs, mod[l], lw, tabs, rope=True)
        kpe_c = cache_mla_kpe[:, l].reshape(DEC_BATCH * PAST_LEN, MLA_ROPE)
        kpe_dup = jnp.concatenate([kpe_c, kpe_c, jnp.zeros((DEC_BATCH * PAST_LEN, 64), F32)], axis=1)
        kmc, vmc = _expand_cached_mla(cache_mla_ckv[:, l].reshape(DEC_BATCH * PAST_LEN, MLA_KV_LORA), kpe_dup,
                                      lw["wuk"], lw["wuvt"])
        om = _latent_mla(qm, km, vm, kmc, vmc)
        flat = lambda a: a[:, l].reshape(DEC_BATCH * PAST_LEN, -1)
        flat_t = lambda a: a[:, l].reshape(DEC_BATCH, PAST_LEN, -1).transpose(0, 2, 1).reshape(-1, PAST_LEN)
        osw = _latent_swa(swa_sink[l], qs, ks, vs, flat(cache_swa_k), flat_t(cache_swa_v))
        ona = _latent_na(na_rpb[l], qn, kn, vn, flat(cache_na_k), flat_t(cache_na_v))
        x1, h2, gates = _tail(xs, om, osw, ona, mod[l], lw, latent=True)
        xs = _moe(h2, gates, x1, mod[l], l, moe_w, nf, latent=True, final=final)
    def heads_last(a, n_heads):
        if n_heads is None:
            return a.transpose(0, 1, 3, 2)
        return a.reshape(BATCH, DEPTH, n_heads, -1, SEQ).transpose(0, 1, 4, 2, 3)

    ckv_new, kpe_new, ks_new, vs_new, kn_new, vn_new = new_caches
    return (xp.reshape(BATCH, SEQ, D_MODEL), xs.reshape(DEC_BATCH, DEC_SEQ, D_MODEL),
            ckv_new, heads_last(kpe_new, None),
            heads_last(ks_new, SWA_KV_HEADS), heads_last(vs_new, SWA_KV_HEADS),
            heads_last(kn_new, NA_HEADS), heads_last(vn_new, NA_HEADS))
```

```python
import functools

import jax
import jax.numpy as jnp
import numpy as np
from jax import lax
from jax.experimental import pallas as pl
from jax.experimental.pallas import tpu as pltpu

D_MODEL = 1024
BATCH = 32
SEQ = 256
DEPTH = 2
DEC_BATCH = 2
DEC_SEQ = 4096
PAST_LEN = 512
GRID_W = 64
HEAD_DIM = 64
MLA_HEADS = 8
MLA_Q_LORA = 256
MLA_KV_LORA = 128
MLA_NOPE = 64
MLA_ROPE = 32
MLA_V = 64
SWA_HEADS = 4
SWA_KV_HEADS = 2
SWA_WINDOW = 128
NA_HEADS = 4
NA_KH = 8
NA_KW = 16
N_GROUPS = 4
EXPERTS_PER_GROUP = 4
N_EXPERTS = 16
EXPERT_FF = 256
N_MOD = 6
ROPE_THETA = 10000.0
EPS = 1e-6
MASK_VALUE = -1e30
MLA_SCALE = (MLA_NOPE + MLA_ROPE) ** -0.5
HEAD_SCALE = HEAD_DIM ** -0.5
LOG2E = 1.4426950408889634

LANES = 128
N_PAIRS = MLA_HEADS // 2
MLA_QK_BLK = 2 * LANES
N_CTX_TOK = BATCH * SEQ
N_LAT_TOK = DEC_BATCH * DEC_SEQ

_C_CQ, _C_CKV, _C_QS, _C_KS, _C_QN, _C_KN, _C_KPE, _C_V, _C_END = 0, 256, 384, 640, 768, 1024, 1280, 1408, 1792
N_V_FEAT = _C_END - _C_V

TM_PROJ = 1024
TM_TAIL = 1024
TAIL_SUB_ROWS = 128
TM_MOE = 1024
LAT_TILES_PER_STEP = 4
CTX_BATCH_PER_STEP = 4
TQ = 256
TQ_MLA = 512
MLA_PAIRS_PER_STEP = 4
KV_CHUNK = 2048
MLA_KV_CHUNK = 512
SCORES_AHEAD = 2
ONES_ROWS = 16
SWA_KWIN = TQ + 2 * SWA_WINDOW
NA_ROWS_PER_TILE = TQ // GRID_W
NA_KEY_ROWS = 12
NA_KWIN = NA_KEY_ROWS * GRID_W
VMEM_LIMIT = 56 * 1024 * 1024

F32 = jnp.float32
BF16 = jnp.bfloat16


def _dot(a, b):
    return jnp.dot(a, b, preferred_element_type=F32)


def _dot_nt(a, b):
    return lax.dot_general(a, b, (((1,), (1,)), ((), ())), preferred_element_type=F32)


def _rms(x, g):
    return x * lax.rsqrt(jnp.mean(x * x, axis=-1, keepdims=True) + EPS) * g


def _cparams(sem):
    return pltpu.CompilerParams(dimension_semantics=sem, vmem_limit_bytes=VMEM_LIMIT)


N_COND = 1 + DEC_BATCH


def _mod_kernel(ct_ref, w_ref, b_ref, o_ref):
    ct = ct_ref[...]
    s = ct * (1.0 / (1.0 + jnp.exp(-ct)))
    w = w_ref[0]
    rows = [jnp.sum(w * s[:, r:r + 1], axis=0, keepdims=True) for r in range(N_COND)]
    rows.append(jnp.zeros((8 - N_COND, w.shape[1]), F32))
    o_ref[0] = jnp.concatenate(rows, axis=0) + b_ref[0]


def _modulation(cpad, w_mod, b_mod):
    nt = 1024
    return pl.pallas_call(
        _mod_kernel,
        grid=(DEPTH, N_MOD * D_MODEL // nt),
        in_specs=[
            pl.BlockSpec((D_MODEL, 8), lambda l, n: (0, 0)),
            pl.BlockSpec((1, D_MODEL, nt), lambda l, n: (l, 0, n)),
            pl.BlockSpec((1, 1, nt), lambda l, n: (l, 0, n)),
        ],
        out_specs=pl.BlockSpec((1, 8, nt), lambda l, n: (l, 0, n)),
        out_shape=jax.ShapeDtypeStruct((DEPTH, 8, N_MOD * D_MODEL), F32),
        compiler_params=_cparams(("arbitrary", "arbitrary")),
        name="modulation",
    )(cpad.T, w_mod, b_mod.reshape(DEPTH, 1, N_MOD * D_MODEL))


def _rope(x, d, cos, sin):
    hh = d // 4
    lane = lax.broadcasted_iota(jnp.int32, x.shape, 1)
    first_half = (lane % (2 * hh)) < hh
    rot = jnp.where(first_half, -pltpu.roll(x, LANES - hh, 1), pltpu.roll(x, hh, 1))
    return x * cos + rot * sin


def _proj_kernel(rope, n_handed_on, *refs):
    refs = refs[:N_PROJ_IN] + refs[N_PROJ_IN + n_handed_on:]
    x_ref, mod_ref, n1_ref, win_ref = refs[:4]
    m = mod_ref[0]
    n_sub = x_ref.shape[0] // SEQ
    ps = []
    for sub in range(n_sub):
        h = _rms(x_ref[sub * SEQ:(sub + 1) * SEQ, :], n1_ref[...]) * (1.0 + m[1:2]) + m[0:1]
        hb = h.astype(BF16)
        ps.append((_dot_nt(hb, win_ref[0:_C_V, :]), _dot_nt(win_ref[_C_V:_C_END, :], hb)))
    for sub in range(n_sub):
        _proj_rows(sub, rope, ps[sub][0], ps[sub][1], refs)


def _proj_rows(sub, rope, p, vt, refs):
    if rope:
        (x_ref, mod_ref, n1_ref, win_ref, gqa_ref, wuq_ref, gkva_ref, wuk_ref, wuvt_ref,
         c64_ref, s64_ref, c32_ref, s32_ref,
         qm_ref, km_ref, vmt_ref, qs_ref, ks_ref, vs_ref, qn_ref, kn_ref, vn_ref) = refs
    else:
        (x_ref, mod_ref, n1_ref, win_ref, gqa_ref, wuq_ref, gkva_ref, wuk_ref, wuvt_ref,
         qm_ref, km_ref, vmt_ref, qs_ref, ks_ref, vs_ref, qn_ref, kn_ref, vn_ref,
         ckv_o, kpe_o, ks_o, vs_o, kn_o, vn_o) = refs
    r = slice(sub * SEQ, (sub + 1) * SEQ)
    qm = _dot(_rms(p[:, _C_CQ:_C_CKV], gqa_ref[...]).astype(BF16), wuq_ref[...])
    ckv = _rms(p[:, _C_CKV:_C_QS], gkva_ref[...])
    ckv_b = ckv.astype(BF16)
    kn_mla = _dot(ckv_b, wuk_ref[...])
    vmt_ref[:, r] = _dot_nt(wuvt_ref[...], ckv_b).astype(BF16)
    qs = p[:, _C_QS:_C_KS]
    ks = p[:, _C_KS:_C_QN]
    kpe = p[:, _C_KPE:_C_V]
    if rope:
        c64, s64, c32, s32 = c64_ref[r, :], s64_ref[r, :], c32_ref[r, :], s32_ref[r, :]
        qs = jnp.concatenate([_rope(qs[:, j * LANES:(j + 1) * LANES], HEAD_DIM, c64, s64) for j in range(2)], axis=1)
        ks = _rope(ks, HEAD_DIM, c64, s64)
        kpe = _rope(kpe, MLA_ROPE, c32, s32)
    for i in range(N_PAIRS):
        lo = i * MLA_QK_BLK
        qrope = qm[:, lo + LANES:lo + MLA_QK_BLK]
        if rope:
            qrope = _rope(qrope, MLA_ROPE, c32, s32)
        qm_ref[r, lo:lo + LANES] = qm[:, lo:lo + LANES].astype(BF16)
        qm_ref[r, lo + LANES:lo + MLA_QK_BLK] = qrope.astype(BF16)
        km_ref[r, lo:lo + LANES] = kn_mla[:, i * LANES:(i + 1) * LANES].astype(BF16)
        km_ref[r, lo + LANES:lo + MLA_QK_BLK] = kpe.astype(BF16)
    qs_ref[r, :] = qs.astype(BF16)
    ks_ref[r, :] = ks.astype(BF16)
    vs_ref[:, r] = vt[0:128, :].astype(BF16)
    qn_ref[r, :] = p[:, _C_QN:_C_KN].astype(BF16)
    kn_ref[r, :] = p[:, _C_KN:_C_KPE].astype(BF16)
    vn_ref[:, r] = vt[128:N_V_FEAT, :].astype(BF16)
    if not rope:
        ckv_o[sub] = ckv
        vs_o[sub] = vt[0:128, :]
        vn_o[sub] = vt[128:N_V_FEAT, :]

        def put_feature_major(o_ref, val, n_feat):
            o_ref[sub] = val.T[:n_feat, :]

        put_feature_major(kpe_o, kpe, MLA_ROPE)
        put_feature_major(ks_o, ks, 128)
        put_feature_major(kn_o, p[:, _C_KN:_C_KPE], 256)


N_PROJ_IN = 9


def _of_layer(a, l):
    return pl.BlockSpec((None,) + a.shape[1:], lambda *_: (l,) + (0,) * (a.ndim - 1))


def _projections(x, mod_l, lw, tabs, rope, layer, new_caches=()):
    n_tok = x.shape[0]
    tm = TM_PROJ
    tiles_per_seq = DEC_SEQ // tm
    win, wuq = lw["win"], lw["wuq"]
    mod_idx = (lambda i: (1 + i // tiles_per_seq, 0, 0)) if rope else (lambda i: (0, 0, 0))
    row = lambda w: pl.BlockSpec((tm, w), lambda i: (i, 0))
    whole = lambda a: _of_layer(a, layer)
    in_specs = [row(D_MODEL), pl.BlockSpec((1, N_MOD, D_MODEL), mod_idx), whole(lw["norm1"]), whole(win),
                whole(lw["g_qa"]), whole(wuq), whole(lw["g_kva"]), whole(lw["wuk"]), whole(lw["wuvt"])]
    args = [x, mod_l, lw["norm1"], win, lw["g_qa"], wuq, lw["g_kva"], lw["wuk"], lw["wuvt"]]
    outs = [(N_PAIRS * MLA_QK_BLK, False), (N_PAIRS * MLA_QK_BLK, False), (N_PAIRS * LANES, True), (256, False),
            (128, False), (128, True), (256, False), (256, False), (256, True)]
    out_specs = [pl.BlockSpec((w, tm), lambda i: (0, i)) if fm else row(w) for w, fm in outs]
    out_shape = [jax.ShapeDtypeStruct((w, n_tok) if fm else (n_tok, w), BF16) for w, fm in outs]
    aliases = {}
    if rope:
        tab_spec = pl.BlockSpec((tm, LANES), lambda i: (i % tiles_per_seq, 0))
        in_specs += [tab_spec] * 4
        args += list(tabs)
    else:
        for dims in [(SEQ, MLA_KV_LORA)] + [(w, SEQ) for w in [MLA_ROPE, 128, 128, 256, 256]]:
            out_specs.append(pl.BlockSpec((tm // SEQ, None) + dims, lambda i: (i, layer, 0, 0)))
            out_shape.append(jax.ShapeDtypeStruct((n_tok // SEQ, DEPTH) + dims, F32))
        aliases = {len(args) + k: len(outs) + k for k in range(len(new_caches))}
        in_specs += [pl.BlockSpec(memory_space=pl.ANY)] * len(new_caches)
        args += list(new_caches)
    return pl.pallas_call(
        functools.partial(_proj_kernel, rope, len(new_caches)),
        grid=(n_tok // tm,),
        in_specs=in_specs,
        out_specs=out_specs,
        out_shape=out_shape,
        input_output_aliases=aliases,
        compiler_params=_cparams(("arbitrary",)),
        name="proj_lat" if rope else "proj_ctx",
    )(*args)


def _ctxkv_kernel(ckv_ref, kpe_ref, wuk_ref, wuvt_ref, km_ref, vmt_ref):
    ckv_b = ckv_ref[...].astype(BF16)
    kn_mla = _dot(ckv_b, wuk_ref[...])
    kpe = kpe_ref[...].astype(BF16)
    for i in range(N_PAIRS):
        lo = i * MLA_QK_BLK
        km_ref[:, lo:lo + LANES] = kn_mla[:, i * LANES:(i + 1) * LANES].astype(BF16)
        km_ref[:, lo + LANES:lo + MLA_QK_BLK] = kpe
    vmt_ref[...] = _dot_nt(wuvt_ref[...], ckv_b).astype(BF16)


def _expand_cached_mla(ckv_c, kpe_dup, wuk, wuvt, layer):
    n = ckv_c.shape[0]
    whole = lambda a: pl.BlockSpec(a.shape, lambda i: (0,) * a.ndim)
    return pl.pallas_call(
        _ctxkv_kernel,
        grid=(1,),
        in_specs=[whole(ckv_c), whole(kpe_dup), _of_layer(wuk, layer), _of_layer(wuvt, layer)],
        out_specs=[pl.BlockSpec((n, N_PAIRS * MLA_QK_BLK), lambda i: (0, 0)),
                   pl.BlockSpec((N_PAIRS * LANES, n), lambda i: (0, 0))],
        out_shape=[jax.ShapeDtypeStruct((n, N_PAIRS * MLA_QK_BLK), BF16),
                   jax.ShapeDtypeStruct((N_PAIRS * LANES, n), BF16)],
        compiler_params=_cparams(("arbitrary",)),
        name="expand_cached_mla",
    )(ckv_c, kpe_dup, wuk, wuvt)


def _pair_masks(width):
    lane = lax.broadcasted_iota(jnp.int32, (1, width), 1)
    if width == LANES:
        return [lane < HEAD_DIM, lane >= HEAD_DIM]
    m0 = (lane < MLA_NOPE) | ((lane >= LANES) & (lane < LANES + MLA_ROPE))
    m1 = ((lane >= MLA_NOPE) & (lane < LANES)) | ((lane >= LANES + MLA_ROPE) & (lane < LANES + 2 * MLA_ROPE))
    return [m0, m1]


def _attention(jobs):
    steps, qa = [], []
    for j, (q, blocks, _) in enumerate(jobs):
        masks = _pair_masks(q.shape[1])
        qa.append([jnp.where(masks[a], q, jnp.zeros_like(q)) for a in range(2)])
        chunks = []
        for blk in blocks:
            k, vt, post = blk[:3]
            cuts = blk[3] if len(blk) > 3 else list(range(0, k.shape[0], KV_CHUNK)) + [k.shape[0]]
            chunks += [(k, vt, post, lo, hi) for lo, hi in zip(cuts[:-1], cuts[1:])]
        steps += [(j, c, ci == len(chunks) - 1) for ci, c in enumerate(chunks)]

    def scores(step):
        j, (k, _, post, lo, hi), _ = step
        kc = k[lo:hi, :]
        s = [_dot_nt(kc, qa[j][a]) for a in range(2)]
        return s if post is None else [post(a, s[a], lo, hi) for a in range(2)]

    outs = [None] * len(jobs)
    m, acc = [None, None], [None, None]
    pending = [scores(st) for st in steps[:SCORES_AHEAD]]
    for n, (j, (_, vt, _, lo, hi), last) in enumerate(steps):
        if n + SCORES_AHEAD < len(steps):
            pending.append(scores(steps[n + SCORES_AHEAD]))
        s_cur = pending.pop(0)
        ones = jnp.ones((ONES_ROWS, hi - lo), BF16)
        for a in range(2):
            vta = jnp.concatenate([vt[a * HEAD_DIM:(a + 1) * HEAD_DIM, lo:hi], ones], axis=0)
            s = s_cur[a]
            mc = jnp.max(s, axis=0, keepdims=True)
            mn = mc if m[a] is None else jnp.maximum(m[a], mc)
            pv = _dot(vta, jnp.exp2(s - mn).astype(BF16))
            acc[a] = pv if m[a] is None else jnp.exp2(m[a] - mn) * acc[a] + pv
            m[a] = mn
        if last:
            sinks, heads = jobs[j][2], []
            for a in range(2):
                l = acc[a][HEAD_DIM:HEAD_DIM + 1, :]
                if sinks[a] is not None:
                    mf = jnp.maximum(m[a], sinks[a])
                    scale = jnp.exp2(m[a] - mf)
                    l = scale * l + jnp.exp2(sinks[a] - mf)
                    heads.append(acc[a][:HEAD_DIM, :] * (scale / l))
                else:
                    heads.append(acc[a][:HEAD_DIM, :] / l)
            outs[j] = jnp.concatenate(heads, axis=0).T
            m, acc = [None, None], [None, None]
    return outs


_NO_SINKS = (None, None)


def _ctx_attn_kernel(sink_ref, qm_ref, km_ref, vmt_ref, qs_ref, ks_ref, vs_ref, qn_ref, kn_ref, vn_ref,
                     om_ref, os_ref, on_ref):
    jobs = []
    for bb in range(CTX_BATCH_PER_STEP):
        r = slice(bb * SEQ, (bb + 1) * SEQ)
        for i in range(N_PAIRS):
            q = qm_ref[r, i * MLA_QK_BLK:(i + 1) * MLA_QK_BLK]
            k = km_ref[r, i * MLA_QK_BLK:(i + 1) * MLA_QK_BLK]
            vt = vmt_ref[i * LANES:(i + 1) * LANES, r]
            jobs.append((q, [(k, vt, None)], _NO_SINKS))
        ks, vst = ks_ref[r, :], vs_ref[:, r]
        for j in range(2):
            sinks = (sink_ref[j] * LOG2E, sink_ref[j + 2] * LOG2E)
            jobs.append((qs_ref[r, j * LANES:(j + 1) * LANES], [(ks, vst, None)], sinks))
        for j in range(2):
            sl = slice(j * LANES, (j + 1) * LANES)
            jobs.append((qn_ref[r, sl], [(kn_ref[r, sl], vn_ref[sl, r], None)], _NO_SINKS))
    outs = _attention(jobs)
    per_batch = N_PAIRS + 4
    for bb in range(CTX_BATCH_PER_STEP):
        r = slice(bb * SEQ, (bb + 1) * SEQ)
        o = outs[bb * per_batch:(bb + 1) * per_batch]
        for i in range(N_PAIRS):
            om_ref[r, i * LANES:(i + 1) * LANES] = o[i].astype(BF16)
        for j in range(2):
            os_ref[r, j * LANES:(j + 1) * LANES] = o[N_PAIRS + j].astype(BF16)
            on_ref[r, j * LANES:(j + 1) * LANES] = o[N_PAIRS + 2 + j].astype(BF16)


def _context_attention(sink_l, qm, km, vmt, qs, ks, vs, qn, kn, vn):
    rows = CTX_BATCH_PER_STEP * SEQ
    row = lambda a: (pl.BlockSpec((a.shape[0], rows), lambda b: (0, b)) if any(a is v for v in (vmt, vs, vn))
                     else pl.BlockSpec((rows, a.shape[1]), lambda b: (b, 0)))
    ins = [qm, km, vmt, qs, ks, vs, qn, kn, vn]
    widths = [N_PAIRS * LANES, 256, 256]
    return pl.pallas_call(
        _ctx_attn_kernel,
        grid=(BATCH // CTX_BATCH_PER_STEP,),
        in_specs=[pl.BlockSpec(memory_space=pltpu.SMEM)] + [row(a) for a in ins],
        out_specs=[pl.BlockSpec((rows, w), lambda b: (b, 0)) for w in widths],
        out_shape=[jax.ShapeDtypeStruct((N_CTX_TOK, w), BF16) for w in widths],
        compiler_params=_cparams(("arbitrary",)),
        name="attn_ctx",
    )(sink_l, *ins)


def _lat_mla_kernel(q_ref, kl_ref, vlt_ref, kc_ref, vct_ref, o_ref):
    jobs = []
    for i in range(MLA_PAIRS_PER_STEP):
        qk, v = slice(i * MLA_QK_BLK, (i + 1) * MLA_QK_BLK), slice(i * LANES, (i + 1) * LANES)
        blocks = [(kc_ref.at[:, qk], vct_ref.at[v, :], None),
                  (kl_ref.at[:, qk], vlt_ref.at[v, :], None, list(range(0, DEC_SEQ + 1, MLA_KV_CHUNK)))]
        jobs.append((q_ref[:, qk], blocks, _NO_SINKS))
    for i, o in enumerate(_attention(jobs)):
        o_ref[:, i * LANES:(i + 1) * LANES] = o.astype(BF16)


def _latent_mla(qm, km, vmt, kmc, vmct):
    nq = DEC_SEQ // TQ_MLA
    pp = MLA_PAIRS_PER_STEP
    return pl.pallas_call(
        _lat_mla_kernel,
        grid=(DEC_BATCH, N_PAIRS // pp, nq),
        in_specs=[
            pl.BlockSpec((TQ_MLA, pp * MLA_QK_BLK), lambda b, i, t: (b * nq + t, i)),
            pl.BlockSpec((DEC_SEQ, pp * MLA_QK_BLK), lambda b, i, t: (b, i)),
            pl.BlockSpec((pp * LANES, DEC_SEQ), lambda b, i, t: (i, b)),
            pl.BlockSpec((PAST_LEN, pp * MLA_QK_BLK), lambda b, i, t: (b, i)),
            pl.BlockSpec((pp * LANES, PAST_LEN), lambda b, i, t: (i, b)),
        ],
        out_specs=pl.BlockSpec((TQ_MLA, pp * LANES), lambda b, i, t: (b * nq + t, i)),
        out_shape=jax.ShapeDtypeStruct((N_LAT_TOK, N_PAIRS * LANES), BF16),
        compiler_params=_cparams(("arbitrary", "arbitrary", "arbitrary")),
        name="attn_lat_mla",
    )(qm, km, vmt, kmc, vmct)


def _lat_swa_kernel(sink_ref, q_ref, k_ref, v_ref, kc_ref, vc_ref, o_ref):
    kc = kc_ref[...].astype(BF16)
    vc = vc_ref[...].astype(BF16)
    jobs = []
    for u in range(LAT_TILES_PER_STEP):
        q0 = (pl.program_id(1) * LAT_TILES_PER_STEP + u) * TQ
        ws = pl.multiple_of(jnp.clip(q0 - SWA_WINDOW, 0, DEC_SEQ - SWA_KWIN), SWA_WINDOW)
        kw = k_ref[pl.ds(ws, SWA_KWIN), :]
        vw = v_ref[:, pl.ds(ws, SWA_KWIN)]
        kpos = ws + lax.broadcasted_iota(jnp.int32, (SWA_KWIN, TQ), 0)
        qpos = q0 + lax.broadcasted_iota(jnp.int32, (SWA_KWIN, TQ), 1)
        in_band = jnp.abs(kpos - qpos) <= SWA_WINDOW

        def band(a, s, lo, hi, in_band=in_band):
            return jnp.where(in_band[lo:hi, :], s, MASK_VALUE)

        for j in range(2):
            sinks = (sink_ref[j] * LOG2E, sink_ref[j + 2] * LOG2E)
            q = q_ref[u * TQ:(u + 1) * TQ, j * LANES:(j + 1) * LANES]
            jobs.append((q, [(kw, vw, band), (kc, vc, None)], sinks))
    for n, o in enumerate(_attention(jobs)):
        u, j = divmod(n, 2)
        o_ref[u * TQ:(u + 1) * TQ, j * LANES:(j + 1) * LANES] = o.astype(BF16)


def _latent_swa(sink_l, qs, ks, vs, ksc, vsc):
    rows = LAT_TILES_PER_STEP * TQ
    nq = DEC_SEQ // rows
    return pl.pallas_call(
        _lat_swa_kernel,
        grid=(DEC_BATCH, nq),
        in_specs=[
            pl.BlockSpec(memory_space=pltpu.SMEM),
            pl.BlockSpec((rows, 256), lambda b, t: (b * nq + t, 0)),
            pl.BlockSpec((DEC_SEQ, LANES), lambda b, t: (b, 0)),
            pl.BlockSpec((LANES, DEC_SEQ), lambda b, t: (0, b)),
            pl.BlockSpec((PAST_LEN, LANES), lambda b, t: (b, 0)),
            pl.BlockSpec((LANES, PAST_LEN), lambda b, t: (b, 0)),
        ],
        out_specs=pl.BlockSpec((rows, 256), lambda b, t: (b * nq + t, 0)),
        out_shape=jax.ShapeDtypeStruct((N_LAT_TOK, 256), BF16),
        compiler_params=_cparams(("arbitrary", "arbitrary")),
        name="attn_lat_swa",
    )(sink_l, qs, ks, vs, ksc, vsc)


_NA_DR = 2 * NA_KH - 1


def _build_na_bias_tiles(rpb_ref, tile_ref):
    kc = lax.broadcasted_iota(jnp.int32, (GRID_W, LANES), 0)
    qc = lax.broadcasted_iota(jnp.int32, (GRID_W, LANES), 1) % GRID_W
    rel = jnp.clip(kc - qc, -(NA_KW - 1), NA_KW - 1) + NA_KW - 1
    col_start = jnp.clip(qc - NA_KW // 2, 0, GRID_W - NA_KW)
    col_ok = (kc >= col_start) & (kc < col_start + NA_KW)
    masked = jnp.full((GRID_W, LANES), MASK_VALUE, F32)
    n_rel = 2 * NA_KW - 1

    def one_tile(hd, carry):
        tile = masked
        for c in range(n_rel):
            tile = jnp.where(col_ok & (rel == c), rpb_ref[hd * n_rel + c] * LOG2E, tile)
        tile_ref[(hd // _NA_DR) * (_NA_DR + 1) + hd % _NA_DR] = tile
        return carry

    lax.fori_loop(0, NA_HEADS * _NA_DR, one_tile, 0)
    for h in range(NA_HEADS):
        tile_ref[h * (_NA_DR + 1) + _NA_DR] = masked


def _na_tile_kinds():
    n_rows = DEC_SEQ // GRID_W
    return (0, NA_ROWS_PER_TILE, n_rows - NA_ROWS_PER_TILE)


def _assemble_na_bias(tile_ref, bias_ref):
    n_rows = DEC_SEQ // GRID_W
    lane_lo = lax.broadcasted_iota(jnp.int32, (GRID_W, LANES), 1) < GRID_W
    for kind, r0 in enumerate(_na_tile_kinds()):
        ws_row = int(np.clip(r0 - NA_KH // 2, 0, n_rows - NA_KEY_ROWS))

        def tile(h, a_q, jj):
            r, kr = r0 + a_q, ws_row + jj
            rs = int(np.clip(r - NA_KH // 2, 0, n_rows - NA_KH))
            return tile_ref[h * (_NA_DR + 1) + (kr - r + NA_KH - 1 if rs <= kr < rs + NA_KH else _NA_DR)]

        for h in range(NA_HEADS):
            for jj in range(NA_KEY_ROWS):
                for u in range(NA_ROWS_PER_TILE // 2):
                    bias_ref[kind * NA_HEADS + h, jj * GRID_W:(jj + 1) * GRID_W, u * LANES:(u + 1) * LANES] = (
                        jnp.where(lane_lo, tile(h, 2 * u, jj), tile(h, 2 * u + 1, jj)))


def _lat_na_kernel(rpb_ref, q_ref, k_ref, v_ref, kc_ref, vc_ref, o_ref, tile_ref, bias_ref):
    t = pl.program_id(1)

    @pl.when((pl.program_id(0) == 0) & (t == 0))
    def _():
        _build_na_bias_tiles(rpb_ref, tile_ref)
        _assemble_na_bias(tile_ref, bias_ref)

    n_rows = DEC_SEQ // GRID_W
    last_tile = DEC_SEQ // TQ - 1
    jobs = []
    for u in range(LAT_TILES_PER_STEP):
        tile = t * LAT_TILES_PER_STEP + u
        r0 = tile * NA_ROWS_PER_TILE
        ws_row = jnp.clip(r0 - NA_KH // 2, 0, n_rows - NA_KEY_ROWS)
        ws = pl.multiple_of(ws_row * GRID_W, NA_ROWS_PER_TILE * GRID_W)
        kind = jnp.where(tile == 0, 0, jnp.where(tile == last_tile, 2, 1))
        for j in range(2):
            sl = slice(j * LANES, (j + 1) * LANES)
            kw = k_ref[pl.ds(ws, NA_KWIN), sl]
            vw = v_ref[sl, pl.ds(ws, NA_KWIN)]
            kc = kc_ref[:, sl].astype(BF16)
            vc = vc_ref[sl, :].astype(BF16)

            def add_bias(a, s, lo, hi, j=j, kind=kind):
                return s + bias_ref[kind * NA_HEADS + 2 * j + a, lo:hi, :]

            jobs.append((q_ref[u * TQ:(u + 1) * TQ, sl], [(kw, vw, add_bias), (kc, vc, None)], _NO_SINKS))
    for n, o in enumerate(_attention(jobs)):
        u, j = divmod(n, 2)
        o_ref[u * TQ:(u + 1) * TQ, j * LANES:(j + 1) * LANES] = o.astype(BF16)


def _latent_na(rpb_l, qn, kn, vn, knc, vnc):
    rows = LAT_TILES_PER_STEP * TQ
    nq = DEC_SEQ // rows
    return pl.pallas_call(
        _lat_na_kernel,
        grid=(DEC_BATCH, nq),
        in_specs=[
            pl.BlockSpec(memory_space=pltpu.SMEM),
            pl.BlockSpec((rows, 256), lambda b, t: (b * nq + t, 0)),
            pl.BlockSpec((DEC_SEQ, 256), lambda b, t: (b, 0)),
            pl.BlockSpec((256, DEC_SEQ), lambda b, t: (0, b)),
            pl.BlockSpec((PAST_LEN, 256), lambda b, t: (b, 0)),
            pl.BlockSpec((256, PAST_LEN), lambda b, t: (b, 0)),
        ],
        out_specs=pl.BlockSpec((rows, 256), lambda b, t: (b * nq + t, 0)),
        out_shape=jax.ShapeDtypeStruct((N_LAT_TOK, 256), BF16),
        scratch_shapes=[pltpu.VMEM((NA_HEADS * (_NA_DR + 1), GRID_W, LANES), F32),
                        pltpu.VMEM((3 * NA_HEADS, NA_KWIN, TQ), F32)],
        compiler_params=_cparams(("arbitrary", "arbitrary")),
        name="attn_lat_na",
    )(rpb_l.reshape(-1), qn, kn, vn, knc, vnc)


_GRP_LANE0 = 0
_EXP_LANE0 = N_GROUPS


def _lane_first_max(x, valid, lane):
    xm = jnp.where(valid, x, -jnp.inf)
    mx = jnp.max(xm, axis=-1, keepdims=True)
    idx = jnp.min(jnp.where(valid & (xm == mx), lane, LANES), axis=-1, keepdims=True)
    return mx, idx


def _tail_kernel(x_ref, om_ref, os_ref, on_ref, mod_ref, n2_ref, wo_ref, wr_ref, x1_ref, h2_ref, gates_ref):
    n_sub = x_ref.shape[0] // TAIL_SUB_ROWS
    subs = [slice(i * TAIL_SUB_ROWS, (i + 1) * TAIL_SUB_ROWS) for i in range(n_sub)]
    wo = wo_ref
    attn = [(_dot(om_ref[r, :], wo[0:512, :]) + _dot(os_ref[r, :], wo[512:768, :]) + _dot(on_ref[r, :], wo[768:1024, :]))
            for r in subs]
    for r, attn_r in zip(subs, attn):
        _tail_rows(r, attn_r, x_ref, mod_ref, n2_ref, wr_ref, x1_ref, h2_ref, gates_ref)


def _tail_rows(r, attn, x_ref, mod_ref, n2_ref, wr_ref, x1_ref, h2_ref, gates_ref):
    m = mod_ref[0]
    x1 = x_ref[r, :] + m[2:3] * attn
    x1_ref[r, :] = x1
    h2 = _rms(x1, n2_ref[...]) * (1.0 + m[4:5]) + m[3:4]
    h2_ref[r, :] = h2.astype(BF16)
    hi = h2.astype(BF16)
    lo = (h2 - hi.astype(F32)).astype(BF16)
    a = _dot(hi, wr_ref[...])
    logits = a[:, :LANES] + a[:, LANES:] + _dot(lo, wr_ref[:, :LANES])
    lane = lax.broadcasted_iota(jnp.int32, logits.shape, 1)
    is_grp = lane < N_GROUPS
    gmax, gidx = _lane_first_max(logits, is_grp, lane)
    gden = jnp.sum(jnp.where(is_grp, jnp.exp(logits - gmax), 0.0), axis=-1, keepdims=True)
    grp_gate = 1.0 / gden
    in_grp = (lane >= _EXP_LANE0) & (lane < _EXP_LANE0 + N_EXPERTS) & ((lane // EXPERTS_PER_GROUP - 1) == gidx)
    v1, i1 = _lane_first_max(logits, in_grp, lane)
    v2, i2 = _lane_first_max(logits, in_grp & (lane != i1), lane)
    e2 = jnp.exp(v2 - v1)
    w1 = grp_gate / (1.0 + e2)
    w2 = grp_gate * e2 / (1.0 + e2)
    gates_ref[r, :] = jnp.where(lane == i1, w1, 0.0) + jnp.where(lane == i2, w2, 0.0)


def _tail(x, om, osw, ona, mod_l, lw, layer, latent):
    n_tok = x.shape[0]
    tm = TM_TAIL
    tiles_per_seq = DEC_SEQ // tm
    mod_idx = (lambda i: (1 + i // tiles_per_seq, 0, 0)) if latent else (lambda i: (0, 0, 0))
    row = lambda w: pl.BlockSpec((tm, w), lambda i: (i, 0))
    whole = lambda a: _of_layer(a, layer)
    return pl.pallas_call(
        _tail_kernel,
        grid=(n_tok // tm,),
        in_specs=[row(D_MODEL), row(512), row(256), row(256), pl.BlockSpec((1, N_MOD, D_MODEL), mod_idx),
                  whole(lw["norm2"]), whole(lw["wout"]), whole(lw["wr"])],
        out_specs=[row(D_MODEL), row(D_MODEL), row(LANES)],
        out_shape=[jax.ShapeDtypeStruct((n_tok, D_MODEL), F32), jax.ShapeDtypeStruct((n_tok, D_MODEL), BF16),
                   jax.ShapeDtypeStruct((n_tok, LANES), F32)],
        compiler_params=_cparams(("arbitrary",)),
        name="tail_lat" if latent else "tail_ctx",
    )(x, om, osw, ona, mod_l, lw["norm2"], lw["wout"], lw["wr"])


def _moe_kernel(final, h2_ref, gates_ref, x1_ref, mod_ref, nf_ref, wg_ref, wu_ref, wd_ref, o_ref, acc_ref):
    g = pl.program_id(1)
    h2 = h2_ref[...]
    gates = gates_ref[...]
    lane = lax.broadcasted_iota(jnp.int32, gates.shape, 1)
    acts = []
    for e in range(EXPERTS_PER_GROUP):
        ge = jnp.sum(jnp.where(lane == _EXP_LANE0 + g * EXPERTS_PER_GROUP + e, gates, 0.0), axis=-1, keepdims=True)
        hg = _dot(h2, wg_ref[0, e])
        hu = _dot(h2, wu_ref[0, e])
        acts.append((hg * (1.0 / (1.0 + jnp.exp(-hg))) * hu * ge).astype(BF16))
    contrib = _dot(jnp.concatenate(acts, axis=1), wd_ref[0].reshape(EXPERTS_PER_GROUP * EXPERT_FF, D_MODEL))

    @pl.when(g == 0)
    def _():
        acc_ref[...] = contrib

    @pl.when((g > 0) & (g < N_GROUPS - 1))
    def _():
        acc_ref[...] += contrib

    @pl.when(g == N_GROUPS - 1)
    def _():
        y = x1_ref[...] + mod_ref[0][5:6] * (acc_ref[...] + contrib)
        if final:
            y = _rms(y, nf_ref[...])
        o_ref[...] = y


def _moe(h2, gates, x1, mod_l, l, moe_w, norm_final, latent, final):
    n_tok = h2.shape[0]
    tm = TM_MOE
    tiles_per_seq = DEC_SEQ // tm
    mod_idx = (lambda i, g: (1 + i // tiles_per_seq, 0, 0)) if latent else (lambda i, g: (0, 0, 0))
    epg = EXPERTS_PER_GROUP
    return pl.pallas_call(
        functools.partial(_moe_kernel, final),
        grid=(n_tok // tm, N_GROUPS),
        in_specs=[
            pl.BlockSpec((tm, D_MODEL), lambda i, g: (i, 0)),
            pl.BlockSpec((tm, LANES), lambda i, g: (i, 0)),
            pl.BlockSpec((tm, D_MODEL), lambda i, g: (i, 0)),
            pl.BlockSpec((1, N_MOD, D_MODEL), mod_idx),
            pl.BlockSpec((1, D_MODEL), lambda i, g: (0, 0)),
            pl.BlockSpec((1, epg, D_MODEL, EXPERT_FF), lambda i, g: (l, g, 0, 0)),
            pl.BlockSpec((1, epg, D_MODEL, EXPERT_FF), lambda i, g: (l, g, 0, 0)),
            pl.BlockSpec((1, epg, EXPERT_FF, D_MODEL), lambda i, g: (l, g, 0, 0)),
        ],
        out_specs=pl.BlockSpec((tm, D_MODEL), lambda i, g: (i, 0)),
        out_shape=jax.ShapeDtypeStruct((n_tok, D_MODEL), F32),
        scratch_shapes=[pltpu.VMEM((tm, D_MODEL), F32)],
        compiler_params=_cparams(("arbitrary", "arbitrary")),
        name=("moe_lat" if latent else "moe_ctx") + ("_final" if final else ""),
    )(h2, gates, x1, mod_l, norm_final, *moe_w)


def _heads(w, d, order, axis):
    parts = [lax.slice_in_dim(w, h * d, (h + 1) * d, axis=axis) for h in order]
    return jnp.concatenate(parts, axis=axis)


def _layer_weights(norm1, norm2, w_in, g_qa, w_uq, g_kva, w_ukv, w_out, w_router_grp, w_router_exp):
    wi = w_in.transpose(0, 2, 1)
    z64 = jnp.zeros((DEPTH, 64, D_MODEL), F32)
    cq, ckv, kpe = wi[:, 0:256], wi[:, 256:384], wi[:, 384:416]
    qs = _heads(wi[:, 416:672], HEAD_DIM, (0, 2, 1, 3), 1) * (HEAD_SCALE * LOG2E)
    ks, vs = wi[:, 672:800], wi[:, 800:928]
    qn, kn, vn = wi[:, 928:1184] * (HEAD_SCALE * LOG2E), wi[:, 1184:1440], wi[:, 1440:1696]
    kped = jnp.concatenate([kpe, kpe, z64], axis=1)
    win_rows = [cq, ckv, qs, ks, qn, kn, kped, vs, vn]
    wq = w_uq.reshape(DEPTH, MLA_Q_LORA, MLA_HEADS, MLA_NOPE + MLA_ROPE) * (MLA_SCALE * LOG2E)
    nope, ropew = wq[..., :MLA_NOPE], wq[..., MLA_NOPE:]
    z64q = jnp.zeros((DEPTH, MLA_Q_LORA, 64), F32)
    blocks = []
    for i in range(N_PAIRS):
        blocks += [nope[:, :, 2 * i], nope[:, :, 2 * i + 1], ropew[:, :, 2 * i], ropew[:, :, 2 * i + 1], z64q]
    wkv = w_ukv.reshape(DEPTH, MLA_KV_LORA, MLA_HEADS, MLA_NOPE + MLA_V)
    wuk = wkv[..., :MLA_NOPE].reshape(DEPTH, MLA_KV_LORA, -1)
    wuvt = wkv[..., MLA_NOPE:].reshape(DEPTH, MLA_KV_LORA, -1).transpose(0, 2, 1)
    wout = jnp.concatenate([w_out[:, :512], _heads(w_out[:, 512:768], HEAD_DIM, (0, 2, 1, 3), 1), w_out[:, 768:]],
                           axis=1)
    wr = jnp.concatenate([w_router_grp, w_router_exp,
                          jnp.zeros((DEPTH, D_MODEL, LANES - N_GROUPS - N_EXPERTS), F32)], axis=2)
    wr_hi = wr.astype(BF16)
    wr_lo = (wr - wr_hi.astype(F32)).astype(BF16)
    return {
        "norm1": norm1[:, None], "norm2": norm2[:, None], "g_qa": g_qa[:, None], "g_kva": g_kva[:, None],
        "win": jnp.concatenate(win_rows, axis=1).astype(BF16),
        "wuq": jnp.concatenate(blocks, axis=2).astype(BF16),
        "wuk": wuk.astype(BF16),
        "wuvt": wuvt.astype(BF16),
        "wout": wout.astype(BF16),
        "wr": jnp.concatenate([wr_hi, wr_lo], axis=2),
    }


def _rope_tables():
    n_rows = DEC_SEQ // GRID_W
    lane = np.arange(LANES)

    def tab(d, used_lanes):
        hh = d // 4
        i = lane % d
        freq = ROPE_THETA ** (-jnp.asarray(i % hh, F32) / hh)
        by_row = jnp.asarray((i // (2 * hh)) == 0)[None, None, :]
        valid = jnp.asarray(lane < used_lanes)[None, None, :]
        ang_r = jnp.arange(n_rows, dtype=F32)[:, None] * freq[None, :]
        ang_c = jnp.arange(GRID_W, dtype=F32)[:, None] * freq[None, :]

        def expand(fn):
            t = jnp.where(by_row, fn(ang_r)[:, None, :], fn(ang_c)[None, :, :])
            return jnp.where(valid, t, 0.0).reshape(DEC_SEQ, LANES)

        return expand(jnp.cos), expand(jnp.sin)

    c64, s64 = tab(HEAD_DIM, LANES)
    c32, s32 = tab(MLA_ROPE, 2 * MLA_ROPE)
    return c64, s64, c32, s32


def kernel(x_prompt, x_sample, cache_mla_ckv, cache_mla_kpe, cache_swa_k, cache_swa_v, cache_na_k, cache_na_v, c, c_ctx, w_mod, b_mod, norm1, norm2, w_in, g_qa, w_uq, g_kva, w_ukv, swa_sink, na_rpb, w_out, w_router_grp, w_router_exp, w_gate, w_up, w_down, norm_final):
    cpad = jnp.concatenate([c_ctx[None], c, jnp.zeros((8 - 1 - DEC_BATCH, D_MODEL), F32)], axis=0)
    mod = _modulation(cpad, w_mod, b_mod).reshape(DEPTH, 8, N_MOD, D_MODEL)
    tabs = _rope_tables()
    nf = norm_final[None]
    xp = x_prompt.reshape(N_CTX_TOK, D_MODEL)
    xs = x_sample.reshape(N_LAT_TOK, D_MODEL)
    new_caches = ()
    moe_w = (w_gate.astype(BF16), w_up.astype(BF16), w_down.astype(BF16))
    lw = _layer_weights(norm1, norm2, w_in, g_qa, w_uq, g_kva, w_ukv, w_out, w_router_grp, w_router_exp)
    for l in range(DEPTH):
        final = l == DEPTH - 1
        outs = _projections(xp, mod[l], lw, None, rope=False, layer=l, new_caches=new_caches)
        om, osw, ona = _context_attention(swa_sink[l], *outs[:9])
        new_caches = outs[9:]
        x1, h2, gates = _tail(xp, om, osw, ona, mod[l], lw, l, latent=False)
        xp = _moe(h2, gates, x1, mod[l], l, moe_w, nf, latent=False, final=final)
        qm, km, vm, qs, ks, vs, qn, kn, vn = _projections(xs, mod[l], lw, tabs, rope=True, layer=l)
        kpe_c = cache_mla_kpe[:, l].reshape(DEC_BATCH * PAST_LEN, MLA_ROPE)
        kpe_dup = jnp.concatenate([kpe_c, kpe_c, jnp.zeros((DEC_BATCH * PAST_LEN, 64), F32)], axis=1)
        kmc, vmc = _expand_cached_mla(cache_mla_ckv[:, l].reshape(DEC_BATCH * PAST_LEN, MLA_KV_LORA), kpe_dup,
                                      lw["wuk"], lw["wuvt"], l)
        om = _latent_mla(qm, km, vm, kmc, vmc)
        flat = lambda a: a[:, l].reshape(DEC_BATCH * PAST_LEN, -1)
        flat_t = lambda a: a[:, l].reshape(DEC_BATCH, PAST_LEN, -1).transpose(0, 2, 1).reshape(-1, PAST_LEN)
        osw = _latent_swa(swa_sink[l], qs, ks, vs, flat(cache_swa_k), flat_t(cache_swa_v))
        ona = _latent_na(na_rpb[l], qn, kn, vn, flat(cache_na_k), flat_t(cache_na_v))
        x1, h2, gates = _tail(xs, om, osw, ona, mod[l], lw, l, latent=True)
        xs = _moe(h2, gates, x1, mod[l], l, moe_w, nf, latent=True, final=final)
    def heads_last(a, n_heads):
        if n_heads is None:
            return a.transpose(0, 1, 3, 2)
        return a.reshape(BATCH, DEPTH, n_heads, -1, SEQ).transpose(0, 1, 4, 2, 3)

    ckv_new, kpe_new, ks_new, vs_new, kn_new, vn_new = new_caches
    return (xp.reshape(BATCH, SEQ, D_MODEL), xs.reshape(DEC_BATCH, DEC_SEQ, D_MODEL),
            ckv_new, heads_last(kpe_new, None),
            heads_last(ks_new, SWA_KV_HEADS), heads_last(vs_new, SWA_KV_HEADS),
            heads_last(kn_new, NA_HEADS), heads_last(vn_new, NA_HEADS))
```

```python
import functools

import jax
import jax.numpy as jnp
import numpy as np
from jax import lax
from jax.experimental import pallas as pl
from jax.experimental.pallas import tpu as pltpu

D_MODEL = 1024
BATCH = 32
SEQ = 256
DEPTH = 2
DEC_BATCH = 2
DEC_SEQ = 4096
PAST_LEN = 512
GRID_W = 64
HEAD_DIM = 64
MLA_HEADS = 8
MLA_Q_LORA = 256
MLA_KV_LORA = 128
MLA_NOPE = 64
MLA_ROPE = 32
MLA_V = 64
SWA_HEADS = 4
SWA_KV_HEADS = 2
SWA_WINDOW = 128
NA_HEADS = 4
NA_KH = 8
NA_KW = 16
N_GROUPS = 4
EXPERTS_PER_GROUP = 4
N_EXPERTS = 16
EXPERT_FF = 256
N_MOD = 6
ROPE_THETA = 10000.0
EPS = 1e-6
MASK_VALUE = -1e30
MLA_SCALE = (MLA_NOPE + MLA_ROPE) ** -0.5
HEAD_SCALE = HEAD_DIM ** -0.5
LOG2E = 1.4426950408889634

LANES = 128
N_PAIRS = MLA_HEADS // 2
MLA_QK_BLK = 2 * LANES
N_CTX_TOK = BATCH * SEQ
N_LAT_TOK = DEC_BATCH * DEC_SEQ

_C_CQ, _C_CKV, _C_QS, _C_KS, _C_QN, _C_KN, _C_KPE, _C_V, _C_END = 0, 256, 384, 640, 768, 1024, 1280, 1408, 1792
N_V_FEAT = _C_END - _C_V

TM_PROJ = 1024
TM_TAIL = 1024
TAIL_SUB_ROWS = 128
TM_MOE = 1024
LAT_TILES_PER_STEP = 4
CTX_BATCH_PER_STEP = 4
TQ = 256
TQ_MLA = 512
MLA_PAIRS_PER_STEP = 4
KV_CHUNK = 2048
MLA_KV_CHUNK = 512
SCORES_AHEAD = 2
ONES_ROWS = 16
SWA_KWIN = TQ + 2 * SWA_WINDOW
NA_ROWS_PER_TILE = TQ // GRID_W
NA_KEY_ROWS = 12
NA_KWIN = NA_KEY_ROWS * GRID_W
VMEM_LIMIT = 56 * 1024 * 1024

F32 = jnp.float32
BF16 = jnp.bfloat16


def _dot(a, b):
    return jnp.dot(a, b, preferred_element_type=F32)


def _dot_nt(a, b):
    return lax.dot_general(a, b, (((1,), (1,)), ((), ())), preferred_element_type=F32)


def _rms(x, g):
    return x * lax.rsqrt(jnp.mean(x * x, axis=-1, keepdims=True) + EPS) * g


def _cparams(sem):
    return pltpu.CompilerParams(dimension_semantics=sem, vmem_limit_bytes=VMEM_LIMIT)


N_COND = 1 + DEC_BATCH


def _mod_kernel(ct_ref, w_ref, b_ref, o_ref):
    ct = ct_ref[...]
    s = ct * (1.0 / (1.0 + jnp.exp(-ct)))
    w = w_ref[0]
    rows = [jnp.sum(w * s[:, r:r + 1], axis=0, keepdims=True) for r in range(N_COND)]
    rows.append(jnp.zeros((8 - N_COND, w.shape[1]), F32))
    o_ref[0] = jnp.concatenate(rows, axis=0) + b_ref[0]


def _modulation(cpad, w_mod, b_mod):
    nt = 1024
    return pl.pallas_call(
        _mod_kernel,
        grid=(DEPTH, N_MOD * D_MODEL // nt),
        in_specs=[
            pl.BlockSpec((D_MODEL, 8), lambda l, n: (0, 0)),
            pl.BlockSpec((1, D_MODEL, nt), lambda l, n: (l, 0, n)),
            pl.BlockSpec((1, 1, nt), lambda l, n: (l, 0, n)),
        ],
        out_specs=pl.BlockSpec((1, 8, nt), lambda l, n: (l, 0, n)),
        out_shape=jax.ShapeDtypeStruct((DEPTH, 8, N_MOD * D_MODEL), F32),
        compiler_params=_cparams(("arbitrary", "arbitrary")),
        name="modulation",
    )(cpad.T, w_mod, b_mod.reshape(DEPTH, 1, N_MOD * D_MODEL))


def _rope(x, d, cos, sin):
    hh = d // 4
    lane = lax.broadcasted_iota(jnp.int32, x.shape, 1)
    first_half = (lane % (2 * hh)) < hh
    rot = jnp.where(first_half, -pltpu.roll(x, LANES - hh, 1), pltpu.roll(x, hh, 1))
    return x * cos + rot * sin


def _proj_kernel(rope, n_handed_on, *refs):
    refs = refs[:N_PROJ_IN] + refs[N_PROJ_IN + n_handed_on:]
    x_ref, mod_ref, n1_ref, win_ref = refs[:4]
    m = mod_ref[0]
    n_sub = x_ref.shape[0] // SEQ
    ps = []
    for sub in range(n_sub):
        h = _rms(x_ref[sub * SEQ:(sub + 1) * SEQ, :], n1_ref[...]) * (1.0 + m[1:2]) + m[0:1]
        hb = h.astype(BF16)
        ps.append((_dot_nt(hb, win_ref[0:_C_V, :]), _dot_nt(win_ref[_C_V:_C_END, :], hb)))
    for sub in range(n_sub):
        _proj_rows(sub, rope, ps[sub][0], ps[sub][1], refs)


def _proj_rows(sub, rope, p, vt, refs):
    if rope:
        (x_ref, mod_ref, n1_ref, win_ref, gqa_ref, wuq_ref, gkva_ref, wuk_ref, wuvt_ref,
         c64_ref, s64_ref, c32_ref, s32_ref,
         qm_ref, km_ref, vmt_ref, qs_ref, ks_ref, vs_ref, qn_ref, kn_ref, vn_ref) = refs
    else:
        (x_ref, mod_ref, n1_ref, win_ref, gqa_ref, wuq_ref, gkva_ref, wuk_ref, wuvt_ref,
         qm_ref, km_ref, vmt_ref, qs_ref, ks_ref, vs_ref, qn_ref, kn_ref, vn_ref,
         ckv_o, kpe_o, ks_o, vs_o, kn_o, vn_o) = refs
    r = slice(sub * SEQ, (sub + 1) * SEQ)
    qm = _dot(_rms(p[:, _C_CQ:_C_CKV], gqa_ref[...]).astype(BF16), wuq_ref[...])
    ckv = _rms(p[:, _C_CKV:_C_QS], gkva_ref[...])
    ckv_b = ckv.astype(BF16)
    kn_mla = _dot(ckv_b, wuk_ref[...])
    vmt_ref[:, r] = _dot_nt(wuvt_ref[...], ckv_b).astype(BF16)
    qs = p[:, _C_QS:_C_KS]
    ks = p[:, _C_KS:_C_QN]
    kpe = p[:, _C_KPE:_C_V]
    if rope:
        c64, s64, c32, s32 = c64_ref[r, :], s64_ref[r, :], c32_ref[r, :], s32_ref[r, :]
        qs = jnp.concatenate([_rope(qs[:, j * LANES:(j + 1) * LANES], HEAD_DIM, c64, s64) for j in range(2)], axis=1)
        ks = _rope(ks, HEAD_DIM, c64, s64)
        kpe = _rope(kpe, MLA_ROPE, c32, s32)
    for i in range(N_PAIRS):
        lo = i * MLA_QK_BLK
        qrope = qm[:, lo + LANES:lo + MLA_QK_BLK]
        if rope:
            qrope = _rope(qrope, MLA_ROPE, c32, s32)
        qm_ref[r, lo:lo + LANES] = qm[:, lo:lo + LANES].astype(BF16)
        qm_ref[r, lo + LANES:lo + MLA_QK_BLK] = qrope.astype(BF16)
        km_ref[r, lo:lo + LANES] = kn_mla[:, i * LANES:(i + 1) * LANES].astype(BF16)
        km_ref[r, lo + LANES:lo + MLA_QK_BLK] = kpe.astype(BF16)
    qs_ref[r, :] = qs.astype(BF16)
    ks_ref[r, :] = ks.astype(BF16)
    vs_ref[:, r] = vt[0:128, :].astype(BF16)
    qn_ref[r, :] = p[:, _C_QN:_C_KN].astype(BF16)
    kn_ref[r, :] = p[:, _C_KN:_C_KPE].astype(BF16)
    vn_ref[:, r] = vt[128:N_V_FEAT, :].astype(BF16)
    if not rope:
        ckv_o[sub] = ckv
        vs_o[sub] = vt[0:128, :]
        vn_o[sub] = vt[128:N_V_FEAT, :]

        def put_feature_major(o_ref, val, n_feat):
            o_ref[sub] = val.T[:n_feat, :]

        put_feature_major(kpe_o, kpe, MLA_ROPE)
        put_feature_major(ks_o, ks, 128)
        put_feature_major(kn_o, p[:, _C_KN:_C_KPE], 256)


N_PROJ_IN = 9


def _of_layer(a, l):
    return pl.BlockSpec((None,) + a.shape[1:], lambda *_: (l,) + (0,) * (a.ndim - 1))


def _projections(x, mod_l, lw, tabs, rope, layer, new_caches=()):
    n_tok = x.shape[0]
    tm = TM_PROJ
    tiles_per_seq = DEC_SEQ // tm
    win, wuq = lw["win"], lw["wuq"]
    mod_idx = (lambda i: (1 + i // tiles_per_seq, 0, 0)) if rope else (lambda i: (0, 0, 0))
    row = lambda w: pl.BlockSpec((tm, w), lambda i: (i, 0))
    whole = lambda a: _of_layer(a, layer)
    in_specs = [row(D_MODEL), pl.BlockSpec((1, N_MOD, D_MODEL), mod_idx), whole(lw["norm1"]), whole(win),
                whole(lw["g_qa"]), whole(wuq), whole(lw["g_kva"]), whole(lw["wuk"]), whole(lw["wuvt"])]
    args = [x, mod_l, lw["norm1"], win, lw["g_qa"], wuq, lw["g_kva"], lw["wuk"], lw["wuvt"]]
    outs = [(N_PAIRS * MLA_QK_BLK, False), (N_PAIRS * MLA_QK_BLK, False), (N_PAIRS * LANES, True), (256, False),
            (128, False), (128, True), (256, False), (256, False), (256, True)]
    out_specs = [pl.BlockSpec((w, tm), lambda i: (0, i)) if fm else row(w) for w, fm in outs]
    out_shape = [jax.ShapeDtypeStruct((w, n_tok) if fm else (n_tok, w), BF16) for w, fm in outs]
    aliases = {}
    if rope:
        tab_spec = pl.BlockSpec((tm, LANES), lambda i: (i % tiles_per_seq, 0))
        in_specs += [tab_spec] * 4
        args += list(tabs)
    else:
        for dims in [(SEQ, MLA_KV_LORA)] + [(w, SEQ) for w in [MLA_ROPE, 128, 128, 256, 256]]:
            out_specs.append(pl.BlockSpec((tm // SEQ, None) + dims, lambda i: (i, layer, 0, 0)))
            out_shape.append(jax.ShapeDtypeStruct((n_tok // SEQ, DEPTH) + dims, F32))
        aliases = {len(args) + k: len(outs) + k for k in range(len(new_caches))}
        in_specs += [pl.BlockSpec(memory_space=pl.ANY)] * len(new_caches)
        args += list(new_caches)
    return pl.pallas_call(
        functools.partial(_proj_kernel, rope, len(new_caches)),
        grid=(n_tok // tm,),
        in_specs=in_specs,
        out_specs=out_specs,
        out_shape=out_shape,
        input_output_aliases=aliases,
        compiler_params=_cparams(("arbitrary",)),
        name="proj_lat" if rope else "proj_ctx",
    )(*args)


def _ctxkv_kernel(ckv_ref, kpe_ref, wuk_ref, wuvt_ref, km_ref, vmt_ref):
    ckv_b = ckv_ref[...].astype(BF16)
    kn_mla = _dot(ckv_b, wuk_ref[...])
    kpe = kpe_ref[...].astype(BF16)
    for i in range(N_PAIRS):
        lo = i * MLA_QK_BLK
        km_ref[:, lo:lo + LANES] = kn_mla[:, i * LANES:(i + 1) * LANES].astype(BF16)
        km_ref[:, lo + LANES:lo + MLA_QK_BLK] = kpe
    vmt_ref[...] = _dot_nt(wuvt_ref[...], ckv_b).astype(BF16)


def _expand_cached_mla(ckv_c, kpe_dup, wuk, wuvt, layer):
    n = ckv_c.shape[0]
    whole = lambda a: pl.BlockSpec(a.shape, lambda i: (0,) * a.ndim)
    return pl.pallas_call(
        _ctxkv_kernel,
        grid=(1,),
        in_specs=[whole(ckv_c), whole(kpe_dup), _of_layer(wuk, layer), _of_layer(wuvt, layer)],
        out_specs=[pl.BlockSpec((n, N_PAIRS * MLA_QK_BLK), lambda i: (0, 0)),
                   pl.BlockSpec((N_PAIRS * LANES, n), lambda i: (0, 0))],
        out_shape=[jax.ShapeDtypeStruct((n, N_PAIRS * MLA_QK_BLK), BF16),
                   jax.ShapeDtypeStruct((N_PAIRS * LANES, n), BF16)],
        compiler_params=_cparams(("arbitrary",)),
        name="expand_cached_mla",
    )(ckv_c, kpe_dup, wuk, wuvt)


def _pair_masks(width):
    lane = lax.broadcasted_iota(jnp.int32, (1, width), 1)
    if width == LANES:
        return [lane < HEAD_DIM, lane >= HEAD_DIM]
    m0 = (lane < MLA_NOPE) | ((lane >= LANES) & (lane < LANES + MLA_ROPE))
    m1 = ((lane >= MLA_NOPE) & (lane < LANES)) | ((lane >= LANES + MLA_ROPE) & (lane < LANES + 2 * MLA_ROPE))
    return [m0, m1]


def _attention(jobs):
    steps, qa = [], []
    for j, (q, blocks, _) in enumerate(jobs):
        masks = _pair_masks(q.shape[1])
        qa.append([jnp.where(masks[a], q, jnp.zeros_like(q)) for a in range(2)])
        chunks = []
        for blk in blocks:
            k, vt, post = blk[:3]
            cuts = blk[3] if len(blk) > 3 else list(range(0, k.shape[0], KV_CHUNK)) + [k.shape[0]]
            chunks += [(k, vt, post, lo, hi) for lo, hi in zip(cuts[:-1], cuts[1:])]
        steps += [(j, c, ci == len(chunks) - 1) for ci, c in enumerate(chunks)]

    def scores(step):
        j, (k, _, post, lo, hi), _ = step
        kc = k[lo:hi, :]
        s = [_dot_nt(kc, qa[j][a]) for a in range(2)]
        return s if post is None else [post(a, s[a], lo, hi) for a in range(2)]

    outs = [None] * len(jobs)
    m, acc = [None, None], [None, None]
    pending = [scores(st) for st in steps[:SCORES_AHEAD]]
    for n, (j, (_, vt, _, lo, hi), last) in enumerate(steps):
        if n + SCORES_AHEAD < len(steps):
            pending.append(scores(steps[n + SCORES_AHEAD]))
        s_cur = pending.pop(0)
        ones = jnp.ones((ONES_ROWS, hi - lo), BF16)
        for a in range(2):
            vta = jnp.concatenate([vt[a * HEAD_DIM:(a + 1) * HEAD_DIM, lo:hi], ones], axis=0)
            s = s_cur[a]
            mc = jnp.max(s, axis=0, keepdims=True)
            mn = mc if m[a] is None else jnp.maximum(m[a], mc)
            pv = _dot(vta, jnp.exp2(s - mn).astype(BF16))
            acc[a] = pv if m[a] is None else jnp.exp2(m[a] - mn) * acc[a] + pv
            m[a] = mn
        if last:
            sinks, heads = jobs[j][2], []
            for a in range(2):
                l = acc[a][HEAD_DIM:HEAD_DIM + 1, :]
                if sinks[a] is not None:
                    mf = jnp.maximum(m[a], sinks[a])
                    scale = jnp.exp2(m[a] - mf)
                    l = scale * l + jnp.exp2(sinks[a] - mf)
                    heads.append(acc[a][:HEAD_DIM, :] * (scale / l))
                else:
                    heads.append(acc[a][:HEAD_DIM, :] / l)
            outs[j] = jnp.concatenate(heads, axis=0).T
            m, acc = [None, None], [None, None]
    return outs


_NO_SINKS = (None, None)


def _ctx_attn_kernel(sink_ref, qm_ref, km_ref, vmt_ref, qs_ref, ks_ref, vs_ref, qn_ref, kn_ref, vn_ref,
                     om_ref, os_ref, on_ref):
    jobs = []
    for bb in range(CTX_BATCH_PER_STEP):
        r = slice(bb * SEQ, (bb + 1) * SEQ)
        for i in range(N_PAIRS):
            q = qm_ref[r, i * MLA_QK_BLK:(i + 1) * MLA_QK_BLK]
            k = km_ref[r, i * MLA_QK_BLK:(i + 1) * MLA_QK_BLK]
            vt = vmt_ref[i * LANES:(i + 1) * LANES, r]
            jobs.append((q, [(k, vt, None)], _NO_SINKS))
        ks, vst = ks_ref[r, :], vs_ref[:, r]
        for j in range(2):
            sinks = (sink_ref[j] * LOG2E, sink_ref[j + 2] * LOG2E)
            jobs.append((qs_ref[r, j * LANES:(j + 1) * LANES], [(ks, vst, None)], sinks))
        for j in range(2):
            sl = slice(j * LANES, (j + 1) * LANES)
            jobs.append((qn_ref[r, sl], [(kn_ref[r, sl], vn_ref[sl, r], None)], _NO_SINKS))
    outs = _attention(jobs)
    per_batch = N_PAIRS + 4
    for bb in range(CTX_BATCH_PER_STEP):
        r = slice(bb * SEQ, (bb + 1) * SEQ)
        o = outs[bb * per_batch:(bb + 1) * per_batch]
        for i in range(N_PAIRS):
            om_ref[r, i * LANES:(i + 1) * LANES] = o[i].astype(BF16)
        for j in range(2):
            os_ref[r, j * LANES:(j + 1) * LANES] = o[N_PAIRS + j].astype(BF16)
            on_ref[r, j * LANES:(j + 1) * LANES] = o[N_PAIRS + 2 + j].astype(BF16)


def _context_attention(sink_l, qm, km, vmt, qs, ks, vs, qn, kn, vn):
    rows = CTX_BATCH_PER_STEP * SEQ
    row = lambda a: (pl.BlockSpec((a.shape[0], rows), lambda b: (0, b)) if any(a is v for v in (vmt, vs, vn))
                     else pl.BlockSpec((rows, a.shape[1]), lambda b: (b, 0)))
    ins = [qm, km, vmt, qs, ks, vs, qn, kn, vn]
    widths = [N_PAIRS * LANES, 256, 256]
    return pl.pallas_call(
        _ctx_attn_kernel,
        grid=(BATCH // CTX_BATCH_PER_STEP,),
        in_specs=[pl.BlockSpec(memory_space=pltpu.SMEM)] + [row(a) for a in ins],
        out_specs=[pl.BlockSpec((rows, w), lambda b: (b, 0)) for w in widths],
        out_shape=[jax.ShapeDtypeStruct((N_CTX_TOK, w), BF16) for w in widths],
        compiler_params=_cparams(("arbitrary",)),
        name="attn_ctx",
    )(sink_l, *ins)


def _lat_mla_kernel(q_ref, kl_ref, vlt_ref, kc_ref, vct_ref, o_ref):
    jobs = []
    for i in range(MLA_PAIRS_PER_STEP):
        qk, v = slice(i * MLA_QK_BLK, (i + 1) * MLA_QK_BLK), slice(i * LANES, (i + 1) * LANES)
        blocks = [(kc_ref.at[:, qk], vct_ref.at[v, :], None),
                  (kl_ref.at[:, qk], vlt_ref.at[v, :], None, list(range(0, DEC_SEQ + 1, MLA_KV_CHUNK)))]
        jobs.append((q_ref[:, qk], blocks, _NO_SINKS))
    for i, o in enumerate(_attention(jobs)):
        o_ref[:, i * LANES:(i + 1) * LANES] = o.astype(BF16)


def _latent_mla(qm, km, vmt, kmc, vmct):
    nq = DEC_SEQ // TQ_MLA
    pp = MLA_PAIRS_PER_STEP
    return pl.pallas_call(
        _lat_mla_kernel,
        grid=(DEC_BATCH, N_PAIRS // pp, nq),
        in_specs=[
            pl.BlockSpec((TQ_MLA, pp * MLA_QK_BLK), lambda b, i, t: (b * nq + t, i)),
            pl.BlockSpec((DEC_SEQ, pp * MLA_QK_BLK), lambda b, i, t: (b, i)),
            pl.BlockSpec((pp * LANES, DEC_SEQ), lambda b, i, t: (i, b)),
            pl.BlockSpec((PAST_LEN, pp * MLA_QK_BLK), lambda b, i, t: (b, i)),
            pl.BlockSpec((pp * LANES, PAST_LEN), lambda b, i, t: (i, b)),
        ],
        out_specs=pl.BlockSpec((TQ_MLA, pp * LANES), lambda b, i, t: (b * nq + t, i)),
        out_shape=jax.ShapeDtypeStruct((N_LAT_TOK, N_PAIRS * LANES), BF16),
        compiler_params=_cparams(("arbitrary", "arbitrary", "arbitrary")),
        name="attn_lat_mla",
    )(qm, km, vmt, kmc, vmct)


def _lat_swa_kernel(sink_ref, q_ref, k_ref, v_ref, kc_ref, vc_ref, o_ref):
    kc = kc_ref[...].astype(BF16)
    vc = vc_ref[...].astype(BF16)
    jobs = []
    for u in range(LAT_TILES_PER_STEP):
        q0 = (pl.program_id(1) * LAT_TILES_PER_STEP + u) * TQ
        ws = pl.multiple_of(jnp.clip(q0 - SWA_WINDOW, 0, DEC_SEQ - SWA_KWIN), SWA_WINDOW)
        kw = k_ref[pl.ds(ws, SWA_KWIN), :]
        vw = v_ref[:, pl.ds(ws, SWA_KWIN)]
        kpos = ws + lax.broadcasted_iota(jnp.int32, (SWA_KWIN, TQ), 0)
        qpos = q0 + lax.broadcasted_iota(jnp.int32, (SWA_KWIN, TQ), 1)
        in_band = jnp.abs(kpos - qpos) <= SWA_WINDOW

        def band(a, s, lo, hi, in_band=in_band):
            return jnp.where(in_band[lo:hi, :], s, MASK_VALUE)

        for j in range(2):
            sinks = (sink_ref[j] * LOG2E, sink_ref[j + 2] * LOG2E)
            q = q_ref[u * TQ:(u + 1) * TQ, j * LANES:(j + 1) * LANES]
            jobs.append((q, [(kw, vw, band), (kc, vc, None)], sinks))
    for n, o in enumerate(_attention(jobs)):
        u, j = divmod(n, 2)
        o_ref[u * TQ:(u + 1) * TQ, j * LANES:(j + 1) * LANES] = o.astype(BF16)


def _latent_swa(sink_l, qs, ks, vs, ksc, vsc, layer):
    rows = LAT_TILES_PER_STEP * TQ
    nq = DEC_SEQ // rows
    cached = lambda a: pl.BlockSpec((None, None) + a.shape[2:], lambda b, t: (b, layer, 0, 0))
    return pl.pallas_call(
        _lat_swa_kernel,
        grid=(DEC_BATCH, nq),
        in_specs=[
            pl.BlockSpec(memory_space=pltpu.SMEM),
            pl.BlockSpec((rows, 256), lambda b, t: (b * nq + t, 0)),
            pl.BlockSpec((DEC_SEQ, LANES), lambda b, t: (b, 0)),
            pl.BlockSpec((LANES, DEC_SEQ), lambda b, t: (0, b)),
            cached(ksc),
            cached(vsc),
        ],
        out_specs=pl.BlockSpec((rows, 256), lambda b, t: (b * nq + t, 0)),
        out_shape=jax.ShapeDtypeStruct((N_LAT_TOK, 256), BF16),
        compiler_params=_cparams(("arbitrary", "arbitrary")),
        name="attn_lat_swa",
    )(sink_l, qs, ks, vs, ksc, vsc)


_NA_DR = 2 * NA_KH - 1


def _build_na_bias_tiles(rpb_ref, tile_ref):
    kc = lax.broadcasted_iota(jnp.int32, (GRID_W, LANES), 0)
    qc = lax.broadcasted_iota(jnp.int32, (GRID_W, LANES), 1) % GRID_W
    rel = jnp.clip(kc - qc, -(NA_KW - 1), NA_KW - 1) + NA_KW - 1
    col_start = jnp.clip(qc - NA_KW // 2, 0, GRID_W - NA_KW)
    col_ok = (kc >= col_start) & (kc < col_start + NA_KW)
    masked = jnp.full((GRID_W, LANES), MASK_VALUE, F32)
    n_rel = 2 * NA_KW - 1

    def one_tile(hd, carry):
        tile = masked
        for c in range(n_rel):
            tile = jnp.where(col_ok & (rel == c), rpb_ref[hd * n_rel + c] * LOG2E, tile)
        tile_ref[(hd // _NA_DR) * (_NA_DR + 1) + hd % _NA_DR] = tile
        return carry

    lax.fori_loop(0, NA_HEADS * _NA_DR, one_tile, 0)
    for h in range(NA_HEADS):
        tile_ref[h * (_NA_DR + 1) + _NA_DR] = masked


def _na_tile_kinds():
    n_rows = DEC_SEQ // GRID_W
    return (0, NA_ROWS_PER_TILE, n_rows - NA_ROWS_PER_TILE)


def _assemble_na_bias(tile_ref, bias_ref):
    n_rows = DEC_SEQ // GRID_W
    lane_lo = lax.broadcasted_iota(jnp.int32, (GRID_W, LANES), 1) < GRID_W
    for kind, r0 in enumerate(_na_tile_kinds()):
        ws_row = int(np.clip(r0 - NA_KH // 2, 0, n_rows - NA_KEY_ROWS))

        def tile(h, a_q, jj):
            r, kr = r0 + a_q, ws_row + jj
            rs = int(np.clip(r - NA_KH // 2, 0, n_rows - NA_KH))
            return tile_ref[h * (_NA_DR + 1) + (kr - r + NA_KH - 1 if rs <= kr < rs + NA_KH else _NA_DR)]

        for h in range(NA_HEADS):
            for jj in range(NA_KEY_ROWS):
                for u in range(NA_ROWS_PER_TILE // 2):
                    bias_ref[kind * NA_HEADS + h, jj * GRID_W:(jj + 1) * GRID_W, u * LANES:(u + 1) * LANES] = (
                        jnp.where(lane_lo, tile(h, 2 * u, jj), tile(h, 2 * u + 1, jj)))


def _lat_na_kernel(rpb_ref, q_ref, k_ref, v_ref, kc_ref, vc_ref, o_ref, tile_ref, bias_ref):
    t = pl.program_id(1)

    @pl.when((pl.program_id(0) == 0) & (t == 0))
    def _():
        _build_na_bias_tiles(rpb_ref, tile_ref)
        _assemble_na_bias(tile_ref, bias_ref)

    n_rows = DEC_SEQ // GRID_W
    last_tile = DEC_SEQ // TQ - 1
    jobs = []
    for u in range(LAT_TILES_PER_STEP):
        tile = t * LAT_TILES_PER_STEP + u
        r0 = tile * NA_ROWS_PER_TILE
        ws_row = jnp.clip(r0 - NA_KH // 2, 0, n_rows - NA_KEY_ROWS)
        ws = pl.multiple_of(ws_row * GRID_W, NA_ROWS_PER_TILE * GRID_W)
        kind = jnp.where(tile == 0, 0, jnp.where(tile == last_tile, 2, 1))
        for j in range(2):
            sl = slice(j * LANES, (j + 1) * LANES)
            kw = k_ref[pl.ds(ws, NA_KWIN), sl]
            vw = v_ref[sl, pl.ds(ws, NA_KWIN)]
            kc = kc_ref[:, sl].astype(BF16)
            vc = vc_ref[sl, :].astype(BF16)

            def add_bias(a, s, lo, hi, j=j, kind=kind):
                return s + bias_ref[kind * NA_HEADS + 2 * j + a, lo:hi, :]

            jobs.append((q_ref[u * TQ:(u + 1) * TQ, sl], [(kw, vw, add_bias), (kc, vc, None)], _NO_SINKS))
    for n, o in enumerate(_attention(jobs)):
        u, j = divmod(n, 2)
        o_ref[u * TQ:(u + 1) * TQ, j * LANES:(j + 1) * LANES] = o.astype(BF16)


def _latent_na(rpb_l, qn, kn, vn, knc, vnc, layer):
    rows = LAT_TILES_PER_STEP * TQ
    nq = DEC_SEQ // rows
    cached = lambda a: pl.BlockSpec((None, None) + a.shape[2:], lambda b, t: (b, layer, 0, 0))
    return pl.pallas_call(
        _lat_na_kernel,
        grid=(DEC_BATCH, nq),
        in_specs=[
            pl.BlockSpec(memory_space=pltpu.SMEM),
            pl.BlockSpec((rows, 256), lambda b, t: (b * nq + t, 0)),
            pl.BlockSpec((DEC_SEQ, 256), lambda b, t: (b, 0)),
            pl.BlockSpec((256, DEC_SEQ), lambda b, t: (0, b)),
            cached(knc),
            cached(vnc),
        ],
        out_specs=pl.BlockSpec((rows, 256), lambda b, t: (b * nq + t, 0)),
        out_shape=jax.ShapeDtypeStruct((N_LAT_TOK, 256), BF16),
        scratch_shapes=[pltpu.VMEM((NA_HEADS * (_NA_DR + 1), GRID_W, LANES), F32),
                        pltpu.VMEM((3 * NA_HEADS, NA_KWIN, TQ), F32)],
        compiler_params=_cparams(("arbitrary", "arbitrary")),
        name="attn_lat_na",
    )(rpb_l.reshape(-1), qn, kn, vn, knc, vnc)


_GRP_LANE0 = 0
_EXP_LANE0 = N_GROUPS


def _lane_first_max(x, valid, lane):
    xm = jnp.where(valid, x, -jnp.inf)
    mx = jnp.max(xm, axis=-1, keepdims=True)
    idx = jnp.min(jnp.where(valid & (xm == mx), lane, LANES), axis=-1, keepdims=True)
    return mx, idx


def _tail_kernel(x_ref, om_ref, os_ref, on_ref, mod_ref, n2_ref, wo_ref, wr_ref, x1_ref, h2_ref, gates_ref):
    n_sub = x_ref.shape[0] // TAIL_SUB_ROWS
    subs = [slice(i * TAIL_SUB_ROWS, (i + 1) * TAIL_SUB_ROWS) for i in range(n_sub)]
    wo = wo_ref
    attn = [(_dot(om_ref[r, :], wo[0:512, :]) + _dot(os_ref[r, :], wo[512:768, :]) + _dot(on_ref[r, :], wo[768:1024, :]))
            for r in subs]
    for r, attn_r in zip(subs, attn):
        _tail_rows(r, attn_r, x_ref, mod_ref, n2_ref, wr_ref, x1_ref, h2_ref, gates_ref)


def _tail_rows(r, attn, x_ref, mod_ref, n2_ref, wr_ref, x1_ref, h2_ref, gates_ref):
    m = mod_ref[0]
    x1 = x_ref[r, :] + m[2:3] * attn
    x1_ref[r, :] = x1
    h2 = _rms(x1, n2_ref[...]) * (1.0 + m[4:5]) + m[3:4]
    h2_ref[r, :] = h2.astype(BF16)
    hi = h2.astype(BF16)
    lo = (h2 - hi.astype(F32)).astype(BF16)
    a = _dot(hi, wr_ref[...])
    logits = a[:, :LANES] + a[:, LANES:] + _dot(lo, wr_ref[:, :LANES])
    lane = lax.broadcasted_iota(jnp.int32, logits.shape, 1)
    is_grp = lane < N_GROUPS
    gmax, gidx = _lane_first_max(logits, is_grp, lane)
    gden = jnp.sum(jnp.where(is_grp, jnp.exp(logits - gmax), 0.0), axis=-1, keepdims=True)
    grp_gate = 1.0 / gden
    in_grp = (lane >= _EXP_LANE0) & (lane < _EXP_LANE0 + N_EXPERTS) & ((lane // EXPERTS_PER_GROUP - 1) == gidx)
    v1, i1 = _lane_first_max(logits, in_grp, lane)
    v2, i2 = _lane_first_max(logits, in_grp & (lane != i1), lane)
    e2 = jnp.exp(v2 - v1)
    w1 = grp_gate / (1.0 + e2)
    w2 = grp_gate * e2 / (1.0 + e2)
    gates_ref[r, :] = jnp.where(lane == i1, w1, 0.0) + jnp.where(lane == i2, w2, 0.0)


def _tail(x, om, osw, ona, mod_l, lw, layer, latent):
    n_tok = x.shape[0]
    tm = TM_TAIL
    tiles_per_seq = DEC_SEQ // tm
    mod_idx = (lambda i: (1 + i // tiles_per_seq, 0, 0)) if latent else (lambda i: (0, 0, 0))
    row = lambda w: pl.BlockSpec((tm, w), lambda i: (i, 0))
    whole = lambda a: _of_layer(a, layer)
    return pl.pallas_call(
        _tail_kernel,
        grid=(n_tok // tm,),
        in_specs=[row(D_MODEL), row(512), row(256), row(256), pl.BlockSpec((1, N_MOD, D_MODEL), mod_idx),
                  whole(lw["norm2"]), whole(lw["wout"]), whole(lw["wr"])],
        out_specs=[row(D_MODEL), row(D_MODEL), row(LANES)],
        out_shape=[jax.ShapeDtypeStruct((n_tok, D_MODEL), F32), jax.ShapeDtypeStruct((n_tok, D_MODEL), BF16),
                   jax.ShapeDtypeStruct((n_tok, LANES), F32)],
        compiler_params=_cparams(("arbitrary",)),
        name="tail_lat" if latent else "tail_ctx",
    )(x, om, osw, ona, mod_l, lw["norm2"], lw["wout"], lw["wr"])


def _moe_kernel(final, h2_ref, gates_ref, x1_ref, mod_ref, nf_ref, wg_ref, wu_ref, wd_ref, o_ref, acc_ref):
    g = pl.program_id(1)
    h2 = h2_ref[...]
    gates = gates_ref[...]
    lane = lax.broadcasted_iota(jnp.int32, gates.shape, 1)
    acts = []
    for e in range(EXPERTS_PER_GROUP):
        ge = jnp.sum(jnp.where(lane == _EXP_LANE0 + g * EXPERTS_PER_GROUP + e, gates, 0.0), axis=-1, keepdims=True)
        hg = _dot(h2, wg_ref[0, e])
        hu = _dot(h2, wu_ref[0, e])
        acts.append((hg * (1.0 / (1.0 + jnp.exp(-hg))) * hu * ge).astype(BF16))
    contrib = _dot(jnp.concatenate(acts, axis=1), wd_ref[0].reshape(EXPERTS_PER_GROUP * EXPERT_FF, D_MODEL))

    @pl.when(g == 0)
    def _():
        acc_ref[...] = contrib

    @pl.when((g > 0) & (g < N_GROUPS - 1))
    def _():
        acc_ref[...] += contrib

    @pl.when(g == N_GROUPS - 1)
    def _():
        y = x1_ref[...] + mod_ref[0][5:6] * (acc_ref[...] + contrib)
        if final:
            y = _rms(y, nf_ref[...])
        o_ref[...] = y


def _moe(h2, gates, x1, mod_l, l, moe_w, norm_final, latent, final):
    n_tok = h2.shape[0]
    tm = TM_MOE
    tiles_per_seq = DEC_SEQ // tm
    mod_idx = (lambda i, g: (1 + i // tiles_per_seq, 0, 0)) if latent else (lambda i, g: (0, 0, 0))
    epg = EXPERTS_PER_GROUP
    return pl.pallas_call(
        functools.partial(_moe_kernel, final),
        grid=(n_tok // tm, N_GROUPS),
        in_specs=[
            pl.BlockSpec((tm, D_MODEL), lambda i, g: (i, 0)),
            pl.BlockSpec((tm, LANES), lambda i, g: (i, 0)),
            pl.BlockSpec((tm, D_MODEL), lambda i, g: (i, 0)),
            pl.BlockSpec((1, N_MOD, D_MODEL), mod_idx),
            pl.BlockSpec((1, D_MODEL), lambda i, g: (0, 0)),
            pl.BlockSpec((1, epg, D_MODEL, EXPERT_FF), lambda i, g: (l, g, 0, 0)),
            pl.BlockSpec((1, epg, D_MODEL, EXPERT_FF), lambda i, g: (l, g, 0, 0)),
            pl.BlockSpec((1, epg, EXPERT_FF, D_MODEL), lambda i, g: (l, g, 0, 0)),
        ],
        out_specs=pl.BlockSpec((tm, D_MODEL), lambda i, g: (i, 0)),
        out_shape=jax.ShapeDtypeStruct((n_tok, D_MODEL), F32),
        scratch_shapes=[pltpu.VMEM((tm, D_MODEL), F32)],
        compiler_params=_cparams(("arbitrary", "arbitrary")),
        name=("moe_lat" if latent else "moe_ctx") + ("_final" if final else ""),
    )(h2, gates, x1, mod_l, norm_final, *moe_w)


def _heads(w, d, order, axis):
    parts = [lax.slice_in_dim(w, h * d, (h + 1) * d, axis=axis) for h in order]
    return jnp.concatenate(parts, axis=axis)


def _layer_weights(norm1, norm2, w_in, g_qa, w_uq, g_kva, w_ukv, w_out, w_router_grp, w_router_exp):
    wi = w_in.transpose(0, 2, 1)
    z64 = jnp.zeros((DEPTH, 64, D_MODEL), F32)
    cq, ckv, kpe = wi[:, 0:256], wi[:, 256:384], wi[:, 384:416]
    qs = _heads(wi[:, 416:672], HEAD_DIM, (0, 2, 1, 3), 1) * (HEAD_SCALE * LOG2E)
    ks, vs = wi[:, 672:800], wi[:, 800:928]
    qn, kn, vn = wi[:, 928:1184] * (HEAD_SCALE * LOG2E), wi[:, 1184:1440], wi[:, 1440:1696]
    kped = jnp.concatenate([kpe, kpe, z64], axis=1)
    win_rows = [cq, ckv, qs, ks, qn, kn, kped, vs, vn]
    wq = w_uq.reshape(DEPTH, MLA_Q_LORA, MLA_HEADS, MLA_NOPE + MLA_ROPE) * (MLA_SCALE * LOG2E)
    nope, ropew = wq[..., :MLA_NOPE], wq[..., MLA_NOPE:]
    z64q = jnp.zeros((DEPTH, MLA_Q_LORA, 64), F32)
    blocks = []
    for i in range(N_PAIRS):
        blocks += [nope[:, :, 2 * i], nope[:, :, 2 * i + 1], ropew[:, :, 2 * i], ropew[:, :, 2 * i + 1], z64q]
    wkv = w_ukv.reshape(DEPTH, MLA_KV_LORA, MLA_HEADS, MLA_NOPE + MLA_V)
    wuk = wkv[..., :MLA_NOPE].reshape(DEPTH, MLA_KV_LORA, -1)
    wuvt = wkv[..., MLA_NOPE:].reshape(DEPTH, MLA_KV_LORA, -1).transpose(0, 2, 1)
    wout = jnp.concatenate([w_out[:, :512], _heads(w_out[:, 512:768], HEAD_DIM, (0, 2, 1, 3), 1), w_out[:, 768:]],
                           axis=1)
    wr = jnp.concatenate([w_router_grp, w_router_exp,
                          jnp.zeros((DEPTH, D_MODEL, LANES - N_GROUPS - N_EXPERTS), F32)], axis=2)
    wr_hi = wr.astype(BF16)
    wr_lo = (wr - wr_hi.astype(F32)).astype(BF16)
    return {
        "norm1": norm1[:, None], "norm2": norm2[:, None], "g_qa": g_qa[:, None], "g_kva": g_kva[:, None],
        "win": jnp.concatenate(win_rows, axis=1).astype(BF16),
        "wuq": jnp.concatenate(blocks, axis=2).astype(BF16),
        "wuk": wuk.astype(BF16),
        "wuvt": wuvt.astype(BF16),
        "wout": wout.astype(BF16),
        "wr": jnp.concatenate([wr_hi, wr_lo], axis=2),
    }


def _rope_tables():
    n_rows = DEC_SEQ // GRID_W
    lane = np.arange(LANES)

    def tab(d, used_lanes):
        hh = d // 4
        i = lane % d
        freq = ROPE_THETA ** (-jnp.asarray(i % hh, F32) / hh)
        by_row = jnp.asarray((i // (2 * hh)) == 0)[None, None, :]
        valid = jnp.asarray(lane < used_lanes)[None, None, :]
        ang_r = jnp.arange(n_rows, dtype=F32)[:, None] * freq[None, :]
        ang_c = jnp.arange(GRID_W, dtype=F32)[:, None] * freq[None, :]

        def expand(fn):
            t = jnp.where(by_row, fn(ang_r)[:, None, :], fn(ang_c)[None, :, :])
            return jnp.where(valid, t, 0.0).reshape(DEC_SEQ, LANES)

        return expand(jnp.cos), expand(jnp.sin)

    c64, s64 = tab(HEAD_DIM, LANES)
    c32, s32 = tab(MLA_ROPE, 2 * MLA_ROPE)
    return c64, s64, c32, s32


def kernel(x_prompt, x_sample, cache_mla_ckv, cache_mla_kpe, cache_swa_k, cache_swa_v, cache_na_k, cache_na_v, c, c_ctx, w_mod, b_mod, norm1, norm2, w_in, g_qa, w_uq, g_kva, w_ukv, swa_sink, na_rpb, w_out, w_router_grp, w_router_exp, w_gate, w_up, w_down, norm_final):
    cpad = jnp.concatenate([c_ctx[None], c, jnp.zeros((8 - 1 - DEC_BATCH, D_MODEL), F32)], axis=0)
    mod = _modulation(cpad, w_mod, b_mod).reshape(DEPTH, 8, N_MOD, D_MODEL)
    tabs = _rope_tables()
    nf = norm_final[None]
    xp = x_prompt.reshape(N_CTX_TOK, D_MODEL)
    xs = x_sample.reshape(N_LAT_TOK, D_MODEL)
    new_caches = ()
    moe_w = (w_gate.astype(BF16), w_up.astype(BF16), w_down.astype(BF16))
    lw = _layer_weights(norm1, norm2, w_in, g_qa, w_uq, g_kva, w_ukv, w_out, w_router_grp, w_router_exp)
    cached_k = lambda a: a.reshape(DEC_BATCH, DEPTH, PAST_LEN, -1)
    cached_vt = lambda a: a.reshape(DEC_BATCH, DEPTH, PAST_LEN, -1).transpose(0, 1, 3, 2)
    for l in range(DEPTH):
        final = l == DEPTH - 1
        outs = _projections(xp, mod[l], lw, None, rope=False, layer=l, new_caches=new_caches)
        om, osw, ona = _context_attention(swa_sink[l], *outs[:9])
        new_caches = outs[9:]
        x1, h2, gates = _tail(xp, om, osw, ona, mod[l], lw, l, latent=False)
        xp = _moe(h2, gates, x1, mod[l], l, moe_w, nf, latent=False, final=final)
        qm, km, vm, qs, ks, vs, qn, kn, vn = _projections(xs, mod[l], lw, tabs, rope=True, layer=l)
        kpe_c = cache_mla_kpe[:, l].reshape(DEC_BATCH * PAST_LEN, MLA_ROPE)
        kpe_dup = jnp.concatenate([kpe_c, kpe_c, jnp.zeros((DEC_BATCH * PAST_LEN, 64), F32)], axis=1)
        kmc, vmc = _expand_cached_mla(cache_mla_ckv[:, l].reshape(DEC_BATCH * PAST_LEN, MLA_KV_LORA), kpe_dup,
                                      lw["wuk"], lw["wuvt"], l)
        om = _latent_mla(qm, km, vm, kmc, vmc)
        osw = _latent_swa(swa_sink[l], qs, ks, vs, cached_k(cache_swa_k), cached_vt(cache_swa_v), l)
        ona = _latent_na(na_rpb[l], qn, kn, vn, cached_k(cache_na_k), cached_vt(cache_na_v), l)
        x1, h2, gates = _tail(xs, om, osw, ona, mod[l], lw, l, latent=True)
        xs = _moe(h2, gates, x1, mod[l], l, moe_w, nf, latent=True, final=final)
    def heads_last(a, n_heads):
        if n_heads is None:
            return a.transpose(0, 1, 3, 2)
        return a.reshape(BATCH, DEPTH, n_heads, -1, SEQ).transpose(0, 1, 4, 2, 3)

    ckv_new, kpe_new, ks_new, vs_new, kn_new, vn_new = new_caches
    return (xp.reshape(BATCH, SEQ, D_MODEL), xs.reshape(DEC_BATCH, DEC_SEQ, D_MODEL),
            ckv_new, heads_last(kpe_new, None),
            heads_last(ks_new, SWA_KV_HEADS), heads_last(vs_new, SWA_KV_HEADS),
            heads_last(kn_new, NA_HEADS), heads_last(vn_new, NA_HEADS))
```

```python
import functools

import jax
import jax.numpy as jnp
import numpy as np
from jax import lax
from jax.experimental import pallas as pl
from jax.experimental.pallas import tpu as pltpu

D_MODEL = 1024
BATCH = 32
SEQ = 256
DEPTH = 2
DEC_BATCH = 2
DEC_SEQ = 4096
PAST_LEN = 512
GRID_W = 64
HEAD_DIM = 64
MLA_HEADS = 8
MLA_Q_LORA = 256
MLA_KV_LORA = 128
MLA_NOPE = 64
MLA_ROPE = 32
MLA_V = 64
SWA_HEADS = 4
SWA_KV_HEADS = 2
SWA_WINDOW = 128
NA_HEADS = 4
NA_KH = 8
NA_KW = 16
N_GROUPS = 4
EXPERTS_PER_GROUP = 4
N_EXPERTS = 16
EXPERT_FF = 256
N_MOD = 6
ROPE_THETA = 10000.0
EPS = 1e-6
MASK_VALUE = -1e30
MLA_SCALE = (MLA_NOPE + MLA_ROPE) ** -0.5
HEAD_SCALE = HEAD_DIM ** -0.5
LOG2E = 1.4426950408889634

LANES = 128
N_PAIRS = MLA_HEADS // 2
MLA_QK_BLK = 2 * LANES
N_CTX_TOK = BATCH * SEQ
N_LAT_TOK = DEC_BATCH * DEC_SEQ

_C_CQ, _C_CKV, _C_QS, _C_KS, _C_QN, _C_KN, _C_KPE, _C_V, _C_END = 0, 256, 384, 640, 768, 1024, 1280, 1408, 1792
N_V_FEAT = _C_END - _C_V

TM_PROJ = 1024
TM_TAIL = 1024
TAIL_SUB_ROWS = 128
TM_MOE = 1024
LAT_TILES_PER_STEP = 4
CTX_BATCH_PER_STEP = 4
TQ = 256
TQ_MLA = 512
MLA_PAIRS_PER_STEP = 4
KV_CHUNK = 2048
MLA_KV_CHUNK = 512
SCORES_AHEAD = 2
ONES_ROWS = 16
SWA_KWIN = TQ + 2 * SWA_WINDOW
NA_ROWS_PER_TILE = TQ // GRID_W
NA_KEY_ROWS = 12
NA_KWIN = NA_KEY_ROWS * GRID_W
VMEM_LIMIT = 56 * 1024 * 1024

F32 = jnp.float32
BF16 = jnp.bfloat16


def _dot(a, b):
    return jnp.dot(a, b, preferred_element_type=F32)


def _dot_nt(a, b):
    return lax.dot_general(a, b, (((1,), (1,)), ((), ())), preferred_element_type=F32)


def _rms(x, g):
    return x * lax.rsqrt(jnp.mean(x * x, axis=-1, keepdims=True) + EPS) * g


def _cparams(sem):
    return pltpu.CompilerParams(dimension_semantics=sem, vmem_limit_bytes=VMEM_LIMIT)


N_COND = 1 + DEC_BATCH


def _mod_kernel(ct_ref, w_ref, b_ref, o_ref):
    ct = ct_ref[...]
    s = ct * (1.0 / (1.0 + jnp.exp(-ct)))
    w = w_ref[0]
    rows = [jnp.sum(w * s[:, r:r + 1], axis=0, keepdims=True) for r in range(N_COND)]
    rows.append(jnp.zeros((8 - N_COND, w.shape[1]), F32))
    o_ref[0] = jnp.concatenate(rows, axis=0) + b_ref[0]


def _modulation(cpad, w_mod, b_mod):
    nt = 1024
    return pl.pallas_call(
        _mod_kernel,
        grid=(DEPTH, N_MOD * D_MODEL // nt),
        in_specs=[
            pl.BlockSpec((D_MODEL, 8), lambda l, n: (0, 0)),
            pl.BlockSpec((1, D_MODEL, nt), lambda l, n: (l, 0, n)),
            pl.BlockSpec((1, 1, nt), lambda l, n: (l, 0, n)),
        ],
        out_specs=pl.BlockSpec((1, 8, nt), lambda l, n: (l, 0, n)),
        out_shape=jax.ShapeDtypeStruct((DEPTH, 8, N_MOD * D_MODEL), F32),
        compiler_params=_cparams(("arbitrary", "arbitrary")),
        name="modulation",
    )(cpad.T, w_mod, b_mod.reshape(DEPTH, 1, N_MOD * D_MODEL))


def _rope(x, d, cos, sin):
    hh = d // 4
    lane = lax.broadcasted_iota(jnp.int32, x.shape, 1)
    first_half = (lane % (2 * hh)) < hh
    rot = jnp.where(first_half, -pltpu.roll(x, LANES - hh, 1), pltpu.roll(x, hh, 1))
    return x * cos + rot * sin


def _proj_kernel(rope, n_handed_on, *refs):
    refs = refs[:N_PROJ_IN] + refs[N_PROJ_IN + n_handed_on:]
    x_ref, mod_ref, n1_ref, win_ref = refs[:4]
    m = mod_ref[0]
    n_sub = x_ref.shape[0] // SEQ
    ps = []
    for sub in range(n_sub):
        h = _rms(x_ref[sub * SEQ:(sub + 1) * SEQ, :], n1_ref[...]) * (1.0 + m[1:2]) + m[0:1]
        hb = h.astype(BF16)
        ps.append((_dot_nt(hb, win_ref[0:_C_V, :]), _dot_nt(win_ref[_C_V:_C_END, :], hb)))
    for sub in range(n_sub):
        _proj_rows(sub, rope, ps[sub][0], ps[sub][1], refs)


def _proj_rows(sub, rope, p, vt, refs):
    if rope:
        (x_ref, mod_ref, n1_ref, win_ref, gqa_ref, wuq_ref, gkva_ref, wuk_ref, wuvt_ref,
         tab_ref,
         qm_ref, km_ref, vmt_ref, qs_ref, ks_ref, vs_ref, qn_ref, kn_ref, vn_ref) = refs
    else:
        (x_ref, mod_ref, n1_ref, win_ref, gqa_ref, wuq_ref, gkva_ref, wuk_ref, wuvt_ref,
         qm_ref, km_ref, vmt_ref, qs_ref, ks_ref, vs_ref, qn_ref, kn_ref, vn_ref,
         ckv_o, kpe_o, ks_o, vs_o, kn_o, vn_o) = refs
    r = slice(sub * SEQ, (sub + 1) * SEQ)
    qm = _dot(_rms(p[:, _C_CQ:_C_CKV], gqa_ref[...]).astype(BF16), wuq_ref[...])
    ckv = _rms(p[:, _C_CKV:_C_QS], gkva_ref[...])
    ckv_b = ckv.astype(BF16)
    kn_mla = _dot(ckv_b, wuk_ref[...])
    vmt_ref[:, r] = _dot_nt(wuvt_ref[...], ckv_b).astype(BF16)
    qs = p[:, _C_QS:_C_KS]
    ks = p[:, _C_KS:_C_QN]
    kpe = p[:, _C_KPE:_C_V]
    if rope:
        grid_rows = SEQ // GRID_W
        tiles_per_seq = DEC_SEQ // x_ref.shape[0]
        row0 = (pl.program_id(0) % tiles_per_seq) * (x_ref.shape[0] // GRID_W) + sub * grid_rows

        def table(t):
            return jnp.concatenate([tab_ref[2 * t, pl.ds(row0 + k, 1), :] + tab_ref[2 * t + 1]
                                    for k in range(grid_rows)], axis=0)

        c64, s64, c32, s32 = (table(t) for t in range(4))
        qs = jnp.concatenate([_rope(qs[:, j * LANES:(j + 1) * LANES], HEAD_DIM, c64, s64) for j in range(2)], axis=1)
        ks = _rope(ks, HEAD_DIM, c64, s64)
        kpe = _rope(kpe, MLA_ROPE, c32, s32)
    for i in range(N_PAIRS):
        lo = i * MLA_QK_BLK
        qrope = qm[:, lo + LANES:lo + MLA_QK_BLK]
        if rope:
            qrope = _rope(qrope, MLA_ROPE, c32, s32)
        qm_ref[r, lo:lo + LANES] = qm[:, lo:lo + LANES].astype(BF16)
        qm_ref[r, lo + LANES:lo + MLA_QK_BLK] = qrope.astype(BF16)
        km_ref[r, lo:lo + LANES] = kn_mla[:, i * LANES:(i + 1) * LANES].astype(BF16)
        km_ref[r, lo + LANES:lo + MLA_QK_BLK] = kpe.astype(BF16)
    qs_ref[r, :] = qs.astype(BF16)
    ks_ref[r, :] = ks.astype(BF16)
    vs_ref[:, r] = vt[0:128, :].astype(BF16)
    qn_ref[r, :] = p[:, _C_QN:_C_KN].astype(BF16)
    kn_ref[r, :] = p[:, _C_KN:_C_KPE].astype(BF16)
    vn_ref[:, r] = vt[128:N_V_FEAT, :].astype(BF16)
    if not rope:
        ckv_o[sub] = ckv
        vs_o[sub] = vt[0:128, :]
        vn_o[sub] = vt[128:N_V_FEAT, :]

        def put_feature_major(o_ref, val, n_feat):
            o_ref[sub] = val.T[:n_feat, :]

        put_feature_major(kpe_o, kpe, MLA_ROPE)
        put_feature_major(ks_o, ks, 128)
        put_feature_major(kn_o, p[:, _C_KN:_C_KPE], 256)


N_PROJ_IN = 9


def _of_layer(a, l):
    return pl.BlockSpec((None,) + a.shape[1:], lambda *_: (l,) + (0,) * (a.ndim - 1))


def _projections(x, mod_l, lw, tabs, rope, layer, new_caches=()):
    n_tok = x.shape[0]
    tm = TM_PROJ
    tiles_per_seq = DEC_SEQ // tm
    win, wuq = lw["win"], lw["wuq"]
    mod_idx = (lambda i: (1 + i // tiles_per_seq, 0, 0)) if rope else (lambda i: (0, 0, 0))
    row = lambda w: pl.BlockSpec((tm, w), lambda i: (i, 0))
    whole = lambda a: _of_layer(a, layer)
    in_specs = [row(D_MODEL), pl.BlockSpec((1, N_MOD, D_MODEL), mod_idx), whole(lw["norm1"]), whole(win),
                whole(lw["g_qa"]), whole(wuq), whole(lw["g_kva"]), whole(lw["wuk"]), whole(lw["wuvt"])]
    args = [x, mod_l, lw["norm1"], win, lw["g_qa"], wuq, lw["g_kva"], lw["wuk"], lw["wuvt"]]
    outs = [(N_PAIRS * MLA_QK_BLK, False), (N_PAIRS * MLA_QK_BLK, False), (N_PAIRS * LANES, True), (256, False),
            (128, False), (128, True), (256, False), (256, False), (256, True)]
    out_specs = [pl.BlockSpec((w, tm), lambda i: (0, i)) if fm else row(w) for w, fm in outs]
    out_shape = [jax.ShapeDtypeStruct((w, n_tok) if fm else (n_tok, w), BF16) for w, fm in outs]
    aliases = {}
    if rope:
        in_specs.append(pl.BlockSpec(tabs.shape, lambda i: (0, 0, 0)))
        args.append(tabs)
    else:
        for dims in [(SEQ, MLA_KV_LORA)] + [(w, SEQ) for w in [MLA_ROPE, 128, 128, 256, 256]]:
            out_specs.append(pl.BlockSpec((tm // SEQ, None) + dims, lambda i: (i, layer, 0, 0)))
            out_shape.append(jax.ShapeDtypeStruct((n_tok // SEQ, DEPTH) + dims, F32))
        aliases = {len(args) + k: len(outs) + k for k in range(len(new_caches))}
        in_specs += [pl.BlockSpec(memory_space=pl.ANY)] * len(new_caches)
        args += list(new_caches)
    return pl.pallas_call(
        functools.partial(_proj_kernel, rope, len(new_caches)),
        grid=(n_tok // tm,),
        in_specs=in_specs,
        out_specs=out_specs,
        out_shape=out_shape,
        input_output_aliases=aliases,
        compiler_params=_cparams(("arbitrary",)),
        name="proj_lat" if rope else "proj_ctx",
    )(*args)


def _ctxkv_kernel(ckv_ref, kpe_ref, wuk_ref, wuvt_ref, km_ref, vmt_ref):
    ckv_b = ckv_ref[...].astype(BF16)
    kn_mla = _dot(ckv_b, wuk_ref[...])
    kpe = kpe_ref[...].astype(BF16)
    for i in range(N_PAIRS):
        lo = i * MLA_QK_BLK
        km_ref[:, lo:lo + LANES] = kn_mla[:, i * LANES:(i + 1) * LANES].astype(BF16)
        km_ref[:, lo + LANES:lo + MLA_QK_BLK] = kpe
    vmt_ref[...] = _dot_nt(wuvt_ref[...], ckv_b).astype(BF16)


def _expand_cached_mla(ckv_c, kpe_dup, wuk, wuvt, layer):
    n = ckv_c.shape[0]
    whole = lambda a: pl.BlockSpec(a.shape, lambda i: (0,) * a.ndim)
    return pl.pallas_call(
        _ctxkv_kernel,
        grid=(1,),
        in_specs=[whole(ckv_c), whole(kpe_dup), _of_layer(wuk, layer), _of_layer(wuvt, layer)],
        out_specs=[pl.BlockSpec((n, N_PAIRS * MLA_QK_BLK), lambda i: (0, 0)),
                   pl.BlockSpec((N_PAIRS * LANES, n), lambda i: (0, 0))],
        out_shape=[jax.ShapeDtypeStruct((n, N_PAIRS * MLA_QK_BLK), BF16),
                   jax.ShapeDtypeStruct((N_PAIRS * LANES, n), BF16)],
        compiler_params=_cparams(("arbitrary",)),
        name="expand_cached_mla",
    )(ckv_c, kpe_dup, wuk, wuvt)


def _pair_masks(width):
    lane = lax.broadcasted_iota(jnp.int32, (1, width), 1)
    if width == LANES:
        return [lane < HEAD_DIM, lane >= HEAD_DIM]
    m0 = (lane < MLA_NOPE) | ((lane >= LANES) & (lane < LANES + MLA_ROPE))
    m1 = ((lane >= MLA_NOPE) & (lane < LANES)) | ((lane >= LANES + MLA_ROPE) & (lane < LANES + 2 * MLA_ROPE))
    return [m0, m1]


def _attention(jobs):
    steps, qa = [], []
    for j, (q, blocks, _) in enumerate(jobs):
        masks = _pair_masks(q.shape[1])
        qa.append([jnp.where(masks[a], q, jnp.zeros_like(q)) for a in range(2)])
        chunks = []
        for blk in blocks:
            k, vt, post = blk[:3]
            cuts = blk[3] if len(blk) > 3 else list(range(0, k.shape[0], KV_CHUNK)) + [k.shape[0]]
            chunks += [(k, vt, post, lo, hi) for lo, hi in zip(cuts[:-1], cuts[1:])]
        steps += [(j, c, ci == len(chunks) - 1) for ci, c in enumerate(chunks)]

    def scores(step):
        j, (k, _, post, lo, hi), _ = step
        kc = k[lo:hi, :]
        s = [_dot_nt(kc, qa[j][a]) for a in range(2)]
        return s if post is None else [post(a, s[a], lo, hi) for a in range(2)]

    outs = [None] * len(jobs)
    m, acc = [None, None], [None, None]
    pending = [scores(st) for st in steps[:SCORES_AHEAD]]
    for n, (j, (_, vt, _, lo, hi), last) in enumerate(steps):
        if n + SCORES_AHEAD < len(steps):
            pending.append(scores(steps[n + SCORES_AHEAD]))
        s_cur = pending.pop(0)
        ones = jnp.ones((ONES_ROWS, hi - lo), BF16)
        for a in range(2):
            vta = jnp.concatenate([vt[a * HEAD_DIM:(a + 1) * HEAD_DIM, lo:hi], ones], axis=0)
            s = s_cur[a]
            mc = jnp.max(s, axis=0, keepdims=True)
            mn = mc if m[a] is None else jnp.maximum(m[a], mc)
            pv = _dot(vta, jnp.exp2(s - mn).astype(BF16))
            acc[a] = pv if m[a] is None else jnp.exp2(m[a] - mn) * acc[a] + pv
            m[a] = mn
        if last:
            sinks, heads = jobs[j][2], []
            for a in range(2):
                l = acc[a][HEAD_DIM:HEAD_DIM + 1, :]
                if sinks[a] is not None:
                    mf = jnp.maximum(m[a], sinks[a])
                    scale = jnp.exp2(m[a] - mf)
                    l = scale * l + jnp.exp2(sinks[a] - mf)
                    heads.append(acc[a][:HEAD_DIM, :] * (scale / l))
                else:
                    heads.append(acc[a][:HEAD_DIM, :] / l)
            outs[j] = jnp.concatenate(heads, axis=0).T
            m, acc = [None, None], [None, None]
    return outs


_NO_SINKS = (None, None)


def _ctx_attn_kernel(sink_ref, qm_ref, km_ref, vmt_ref, qs_ref, ks_ref, vs_ref, qn_ref, kn_ref, vn_ref,
                     om_ref, os_ref, on_ref):
    jobs = []
    for bb in range(CTX_BATCH_PER_STEP):
        r = slice(bb * SEQ, (bb + 1) * SEQ)
        for i in range(N_PAIRS):
            q = qm_ref[r, i * MLA_QK_BLK:(i + 1) * MLA_QK_BLK]
            k = km_ref[r, i * MLA_QK_BLK:(i + 1) * MLA_QK_BLK]
            vt = vmt_ref[i * LANES:(i + 1) * LANES, r]
            jobs.append((q, [(k, vt, None)], _NO_SINKS))
        ks, vst = ks_ref[r, :], vs_ref[:, r]
        for j in range(2):
            sinks = (sink_ref[j] * LOG2E, sink_ref[j + 2] * LOG2E)
            jobs.append((qs_ref[r, j * LANES:(j + 1) * LANES], [(ks, vst, None)], sinks))
        for j in range(2):
            sl = slice(j * LANES, (j + 1) * LANES)
            jobs.append((qn_ref[r, sl], [(kn_ref[r, sl], vn_ref[sl, r], None)], _NO_SINKS))
    outs = _attention(jobs)
    per_batch = N_PAIRS + 4
    for bb in range(CTX_BATCH_PER_STEP):
        r = slice(bb * SEQ, (bb + 1) * SEQ)
        o = outs[bb * per_batch:(bb + 1) * per_batch]
        for i in range(N_PAIRS):
            om_ref[r, i * LANES:(i + 1) * LANES] = o[i].astype(BF16)
        for j in range(2):
            os_ref[r, j * LANES:(j + 1) * LANES] = o[N_PAIRS + j].astype(BF16)
            on_ref[r, j * LANES:(j + 1) * LANES] = o[N_PAIRS + 2 + j].astype(BF16)


def _context_attention(sink_l, qm, km, vmt, qs, ks, vs, qn, kn, vn):
    rows = CTX_BATCH_PER_STEP * SEQ
    row = lambda a: (pl.BlockSpec((a.shape[0], rows), lambda b: (0, b)) if any(a is v for v in (vmt, vs, vn))
                     else pl.BlockSpec((rows, a.shape[1]), lambda b: (b, 0)))
    ins = [qm, km, vmt, qs, ks, vs, qn, kn, vn]
    widths = [N_PAIRS * LANES, 256, 256]
    return pl.pallas_call(
        _ctx_attn_kernel,
        grid=(BATCH // CTX_BATCH_PER_STEP,),
        in_specs=[pl.BlockSpec(memory_space=pltpu.SMEM)] + [row(a) for a in ins],
        out_specs=[pl.BlockSpec((rows, w), lambda b: (b, 0)) for w in widths],
        out_shape=[jax.ShapeDtypeStruct((N_CTX_TOK, w), BF16) for w in widths],
        compiler_params=_cparams(("arbitrary",)),
        name="attn_ctx",
    )(sink_l, *ins)


def _lat_mla_kernel(q_ref, kl_ref, vlt_ref, kc_ref, vct_ref, o_ref):
    jobs = []
    for i in range(MLA_PAIRS_PER_STEP):
        qk, v = slice(i * MLA_QK_BLK, (i + 1) * MLA_QK_BLK), slice(i * LANES, (i + 1) * LANES)
        blocks = [(kc_ref.at[:, qk], vct_ref.at[v, :], None),
                  (kl_ref.at[:, qk], vlt_ref.at[v, :], None, list(range(0, DEC_SEQ + 1, MLA_KV_CHUNK)))]
        jobs.append((q_ref[:, qk], blocks, _NO_SINKS))
    for i, o in enumerate(_attention(jobs)):
        o_ref[:, i * LANES:(i + 1) * LANES] = o.astype(BF16)


def _latent_mla(qm, km, vmt, kmc, vmct):
    nq = DEC_SEQ // TQ_MLA
    pp = MLA_PAIRS_PER_STEP
    return pl.pallas_call(
        _lat_mla_kernel,
        grid=(DEC_BATCH, N_PAIRS // pp, nq),
        in_specs=[
            pl.BlockSpec((TQ_MLA, pp * MLA_QK_BLK), lambda b, i, t: (b * nq + t, i)),
            pl.BlockSpec((DEC_SEQ, pp * MLA_QK_BLK), lambda b, i, t: (b, i)),
            pl.BlockSpec((pp * LANES, DEC_SEQ), lambda b, i, t: (i, b)),
            pl.BlockSpec((PAST_LEN, pp * MLA_QK_BLK), lambda b, i, t: (b, i)),
            pl.BlockSpec((pp * LANES, PAST_LEN), lambda b, i, t: (i, b)),
        ],
        out_specs=pl.BlockSpec((TQ_MLA, pp * LANES), lambda b, i, t: (b * nq + t, i)),
        out_shape=jax.ShapeDtypeStruct((N_LAT_TOK, N_PAIRS * LANES), BF16),
        compiler_params=_cparams(("arbitrary", "arbitrary", "arbitrary")),
        name="attn_lat_mla",
    )(qm, km, vmt, kmc, vmct)


def _lat_swa_kernel(sink_ref, q_ref, k_ref, v_ref, kc_ref, vc_ref, o_ref):
    kc = kc_ref[...].astype(BF16)
    vc = vc_ref[...].astype(BF16)
    jobs = []
    for u in range(LAT_TILES_PER_STEP):
        q0 = (pl.program_id(1) * LAT_TILES_PER_STEP + u) * TQ
        ws = pl.multiple_of(jnp.clip(q0 - SWA_WINDOW, 0, DEC_SEQ - SWA_KWIN), SWA_WINDOW)
        kw = k_ref[pl.ds(ws, SWA_KWIN), :]
        vw = v_ref[:, pl.ds(ws, SWA_KWIN)]
        kpos = ws + lax.broadcasted_iota(jnp.int32, (SWA_KWIN, TQ), 0)
        qpos = q0 + lax.broadcasted_iota(jnp.int32, (SWA_KWIN, TQ), 1)
        in_band = jnp.abs(kpos - qpos) <= SWA_WINDOW

        def band(a, s, lo, hi, in_band=in_band):
            return jnp.where(in_band[lo:hi, :], s, MASK_VALUE)

        for j in range(2):
            sinks = (sink_ref[j] * LOG2E, sink_ref[j + 2] * LOG2E)
            q = q_ref[u * TQ:(u + 1) * TQ, j * LANES:(j + 1) * LANES]
            jobs.append((q, [(kw, vw, band), (kc, vc, None)], sinks))
    for n, o in enumerate(_attention(jobs)):
        u, j = divmod(n, 2)
        o_ref[u * TQ:(u + 1) * TQ, j * LANES:(j + 1) * LANES] = o.astype(BF16)


def _latent_swa(sink_l, qs, ks, vs, ksc, vsc, layer):
    rows = LAT_TILES_PER_STEP * TQ
    nq = DEC_SEQ // rows
    cached = lambda a: pl.BlockSpec((None, None) + a.shape[2:], lambda b, t: (b, layer, 0, 0))
    return pl.pallas_call(
        _lat_swa_kernel,
        grid=(DEC_BATCH, nq),
        in_specs=[
            pl.BlockSpec(memory_space=pltpu.SMEM),
            pl.BlockSpec((rows, 256), lambda b, t: (b * nq + t, 0)),
            pl.BlockSpec((DEC_SEQ, LANES), lambda b, t: (b, 0)),
            pl.BlockSpec((LANES, DEC_SEQ), lambda b, t: (0, b)),
            cached(ksc),
            cached(vsc),
        ],
        out_specs=pl.BlockSpec((rows, 256), lambda b, t: (b * nq + t, 0)),
        out_shape=jax.ShapeDtypeStruct((N_LAT_TOK, 256), BF16),
        compiler_params=_cparams(("arbitrary", "arbitrary")),
        name="attn_lat_swa",
    )(sink_l, qs, ks, vs, ksc, vsc)


_NA_DR = 2 * NA_KH - 1


def _build_na_bias_tiles(rpb_ref, tile_ref):
    kc = lax.broadcasted_iota(jnp.int32, (GRID_W, LANES), 0)
    qc = lax.broadcasted_iota(jnp.int32, (GRID_W, LANES), 1) % GRID_W
    rel = jnp.clip(kc - qc, -(NA_KW - 1), NA_KW - 1) + NA_KW - 1
    col_start = jnp.clip(qc - NA_KW // 2, 0, GRID_W - NA_KW)
    col_ok = (kc >= col_start) & (kc < col_start + NA_KW)
    masked = jnp.full((GRID_W, LANES), MASK_VALUE, F32)
    n_rel = 2 * NA_KW - 1

    def one_tile(hd, carry):
        tile = masked
        for c in range(n_rel):
            tile = jnp.where(col_ok & (rel == c), rpb_ref[hd * n_rel + c] * LOG2E, tile)
        tile_ref[(hd // _NA_DR) * (_NA_DR + 1) + hd % _NA_DR] = tile
        return carry

    lax.fori_loop(0, NA_HEADS * _NA_DR, one_tile, 0)
    for h in range(NA_HEADS):
        tile_ref[h * (_NA_DR + 1) + _NA_DR] = masked


def _na_tile_kinds():
    n_rows = DEC_SEQ // GRID_W
    return (0, NA_ROWS_PER_TILE, n_rows - NA_ROWS_PER_TILE)


def _assemble_na_bias(tile_ref, bias_ref):
    n_rows = DEC_SEQ // GRID_W
    lane_lo = lax.broadcasted_iota(jnp.int32, (GRID_W, LANES), 1) < GRID_W
    for kind, r0 in enumerate(_na_tile_kinds()):
        ws_row = int(np.clip(r0 - NA_KH // 2, 0, n_rows - NA_KEY_ROWS))

        def tile(h, a_q, jj):
            r, kr = r0 + a_q, ws_row + jj
            rs = int(np.clip(r - NA_KH // 2, 0, n_rows - NA_KH))
            return tile_ref[h * (_NA_DR + 1) + (kr - r + NA_KH - 1 if rs <= kr < rs + NA_KH else _NA_DR)]

        for h in range(NA_HEADS):
            for jj in range(NA_KEY_ROWS):
                for u in range(NA_ROWS_PER_TILE // 2):
                    bias_ref[kind * NA_HEADS + h, jj * GRID_W:(jj + 1) * GRID_W, u * LANES:(u + 1) * LANES] = (
                        jnp.where(lane_lo, tile(h, 2 * u, jj), tile(h, 2 * u + 1, jj)))


def _lat_na_kernel(rpb_ref, q_ref, k_ref, v_ref, kc_ref, vc_ref, o_ref, tile_ref, bias_ref):
    t = pl.program_id(1)

    @pl.when((pl.program_id(0) == 0) & (t == 0))
    def _():
        _build_na_bias_tiles(rpb_ref, tile_ref)
        _assemble_na_bias(tile_ref, bias_ref)

    n_rows = DEC_SEQ // GRID_W
    last_tile = DEC_SEQ // TQ - 1
    jobs = []
    for u in range(LAT_TILES_PER_STEP):
        tile = t * LAT_TILES_PER_STEP + u
        r0 = tile * NA_ROWS_PER_TILE
        ws_row = jnp.clip(r0 - NA_KH // 2, 0, n_rows - NA_KEY_ROWS)
        ws = pl.multiple_of(ws_row * GRID_W, NA_ROWS_PER_TILE * GRID_W)
        kind = jnp.where(tile == 0, 0, jnp.where(tile == last_tile, 2, 1))
        for j in range(2):
            sl = slice(j * LANES, (j + 1) * LANES)
            kw = k_ref[pl.ds(ws, NA_KWIN), sl]
            vw = v_ref[sl, pl.ds(ws, NA_KWIN)]
            kc = kc_ref[:, sl].astype(BF16)
            vc = vc_ref[sl, :].astype(BF16)

            def add_bias(a, s, lo, hi, j=j, kind=kind):
                return s + bias_ref[kind * NA_HEADS + 2 * j + a, lo:hi, :]

            jobs.append((q_ref[u * TQ:(u + 1) * TQ, sl], [(kw, vw, add_bias), (kc, vc, None)], _NO_SINKS))
    for n, o in enumerate(_attention(jobs)):
        u, j = divmod(n, 2)
        o_ref[u * TQ:(u + 1) * TQ, j * LANES:(j + 1) * LANES] = o.astype(BF16)


def _latent_na(rpb_l, qn, kn, vn, knc, vnc, layer):
    rows = LAT_TILES_PER_STEP * TQ
    nq = DEC_SEQ // rows
    cached = lambda a: pl.BlockSpec((None, None) + a.shape[2:], lambda b, t: (b, layer, 0, 0))
    return pl.pallas_call(
        _lat_na_kernel,
        grid=(DEC_BATCH, nq),
        in_specs=[
            pl.BlockSpec(memory_space=pltpu.SMEM),
            pl.BlockSpec((rows, 256), lambda b, t: (b * nq + t, 0)),
            pl.BlockSpec((DEC_SEQ, 256), lambda b, t: (b, 0)),
            pl.BlockSpec((256, DEC_SEQ), lambda b, t: (0, b)),
            cached(knc),
            cached(vnc),
        ],
        out_specs=pl.BlockSpec((rows, 256), lambda b, t: (b * nq + t, 0)),
        out_shape=jax.ShapeDtypeStruct((N_LAT_TOK, 256), BF16),
        scratch_shapes=[pltpu.VMEM((NA_HEADS * (_NA_DR + 1), GRID_W, LANES), F32),
                        pltpu.VMEM((3 * NA_HEADS, NA_KWIN, TQ), F32)],
        compiler_params=_cparams(("arbitrary", "arbitrary")),
        name="attn_lat_na",
    )(rpb_l.reshape(-1), qn, kn, vn, knc, vnc)


_GRP_LANE0 = 0
_EXP_LANE0 = N_GROUPS


def _lane_first_max(x, valid, lane):
    xm = jnp.where(valid, x, -jnp.inf)
    mx = jnp.max(xm, axis=-1, keepdims=True)
    idx = jnp.min(jnp.where(valid & (xm == mx), lane, LANES), axis=-1, keepdims=True)
    return mx, idx


def _tail_kernel(x_ref, om_ref, os_ref, on_ref, mod_ref, n2_ref, wo_ref, wr_ref, x1_ref, h2_ref, gates_ref):
    n_sub = x_ref.shape[0] // TAIL_SUB_ROWS
    subs = [slice(i * TAIL_SUB_ROWS, (i + 1) * TAIL_SUB_ROWS) for i in range(n_sub)]
    wo = wo_ref
    attn = [(_dot(om_ref[r, :], wo[0:512, :]) + _dot(os_ref[r, :], wo[512:768, :]) + _dot(on_ref[r, :], wo[768:1024, :]))
            for r in subs]
    for r, attn_r in zip(subs, attn):
        _tail_rows(r, attn_r, x_ref, mod_ref, n2_ref, wr_ref, x1_ref, h2_ref, gates_ref)


def _tail_rows(r, attn, x_ref, mod_ref, n2_ref, wr_ref, x1_ref, h2_ref, gates_ref):
    m = mod_ref[0]
    x1 = x_ref[r, :] + m[2:3] * attn
    x1_ref[r, :] = x1
    h2 = _rms(x1, n2_ref[...]) * (1.0 + m[4:5]) + m[3:4]
    h2_ref[r, :] = h2.astype(BF16)
    hi = h2.astype(BF16)
    lo = (h2 - hi.astype(F32)).astype(BF16)
    a = _dot(hi, wr_ref[...])
    logits = a[:, :LANES] + a[:, LANES:] + _dot(lo, wr_ref[:, :LANES])
    lane = lax.broadcasted_iota(jnp.int32, logits.shape, 1)
    is_grp = lane < N_GROUPS
    gmax, gidx = _lane_first_max(logits, is_grp, lane)
    gden = jnp.sum(jnp.where(is_grp, jnp.exp(logits - gmax), 0.0), axis=-1, keepdims=True)
    grp_gate = 1.0 / gden
    in_grp = (lane >= _EXP_LANE0) & (lane < _EXP_LANE0 + N_EXPERTS) & ((lane // EXPERTS_PER_GROUP - 1) == gidx)
    v1, i1 = _lane_first_max(logits, in_grp, lane)
    v2, i2 = _lane_first_max(logits, in_grp & (lane != i1), lane)
    e2 = jnp.exp(v2 - v1)
    w1 = grp_gate / (1.0 + e2)
    w2 = grp_gate * e2 / (1.0 + e2)
    gates_ref[r, :] = jnp.where(lane == i1, w1, 0.0) + jnp.where(lane == i2, w2, 0.0)


def _tail(x, om, osw, ona, mod_l, lw, layer, latent):
    n_tok = x.shape[0]
    tm = TM_TAIL
    tiles_per_seq = DEC_SEQ // tm
    mod_idx = (lambda i: (1 + i // tiles_per_seq, 0, 0)) if latent else (lambda i: (0, 0, 0))
    row = lambda w: pl.BlockSpec((tm, w), lambda i: (i, 0))
    whole = lambda a: _of_layer(a, layer)
    return pl.pallas_call(
        _tail_kernel,
        grid=(n_tok // tm,),
        in_specs=[row(D_MODEL), row(512), row(256), row(256), pl.BlockSpec((1, N_MOD, D_MODEL), mod_idx),
                  whole(lw["norm2"]), whole(lw["wout"]), whole(lw["wr"])],
        out_specs=[row(D_MODEL), row(D_MODEL), row(LANES)],
        out_shape=[jax.ShapeDtypeStruct((n_tok, D_MODEL), F32), jax.ShapeDtypeStruct((n_tok, D_MODEL), BF16),
                   jax.ShapeDtypeStruct((n_tok, LANES), F32)],
        compiler_params=_cparams(("arbitrary",)),
        name="tail_lat" if latent else "tail_ctx",
    )(x, om, osw, ona, mod_l, lw["norm2"], lw["wout"], lw["wr"])


def _moe_kernel(final, h2_ref, gates_ref, x1_ref, mod_ref, nf_ref, wg_ref, wu_ref, wd_ref, o_ref, acc_ref):
    g = pl.program_id(1)
    h2 = h2_ref[...]
    gates = gates_ref[...]
    lane = lax.broadcasted_iota(jnp.int32, gates.shape, 1)
    acts = []
    for e in range(EXPERTS_PER_GROUP):
        ge = jnp.sum(jnp.where(lane == _EXP_LANE0 + g * EXPERTS_PER_GROUP + e, gates, 0.0), axis=-1, keepdims=True)
        hg = _dot(h2, wg_ref[0, e])
        hu = _dot(h2, wu_ref[0, e])
        acts.append((hg * (1.0 / (1.0 + jnp.exp(-hg))) * hu * ge).astype(BF16))
    contrib = _dot(jnp.concatenate(acts, axis=1), wd_ref[0].reshape(EXPERTS_PER_GROUP * EXPERT_FF, D_MODEL))

    @pl.when(g == 0)
    def _():
        acc_ref[...] = contrib

    @pl.when((g > 0) & (g < N_GROUPS - 1))
    def _():
        acc_ref[...] += contrib

    @pl.when(g == N_GROUPS - 1)
    def _():
        y = x1_ref[...] + mod_ref[0][5:6] * (acc_ref[...] + contrib)
        if final:
            y = _rms(y, nf_ref[...])
        o_ref[...] = y


def _moe(h2, gates, x1, mod_l, l, moe_w, norm_final, latent, final):
    n_tok = h2.shape[0]
    tm = TM_MOE
    tiles_per_seq = DEC_SEQ // tm
    mod_idx = (lambda i, g: (1 + i // tiles_per_seq, 0, 0)) if latent else (lambda i, g: (0, 0, 0))
    epg = EXPERTS_PER_GROUP
    return pl.pallas_call(
        functools.partial(_moe_kernel, final),
        grid=(n_tok // tm, N_GROUPS),
        in_specs=[
            pl.BlockSpec((tm, D_MODEL), lambda i, g: (i, 0)),
            pl.BlockSpec((tm, LANES), lambda i, g: (i, 0)),
            pl.BlockSpec((tm, D_MODEL), lambda i, g: (i, 0)),
            pl.BlockSpec((1, N_MOD, D_MODEL), mod_idx),
            pl.BlockSpec((1, D_MODEL), lambda i, g: (0, 0)),
            pl.BlockSpec((1, epg, D_MODEL, EXPERT_FF), lambda i, g: (l, g, 0, 0)),
            pl.BlockSpec((1, epg, D_MODEL, EXPERT_FF), lambda i, g: (l, g, 0, 0)),
            pl.BlockSpec((1, epg, EXPERT_FF, D_MODEL), lambda i, g: (l, g, 0, 0)),
        ],
        out_specs=pl.BlockSpec((tm, D_MODEL), lambda i, g: (i, 0)),
        out_shape=jax.ShapeDtypeStruct((n_tok, D_MODEL), F32),
        scratch_shapes=[pltpu.VMEM((tm, D_MODEL), F32)],
        compiler_params=_cparams(("arbitrary", "arbitrary")),
        name=("moe_lat" if latent else "moe_ctx") + ("_final" if final else ""),
    )(h2, gates, x1, mod_l, norm_final, *moe_w)


def _heads(w, d, order, axis):
    parts = [lax.slice_in_dim(w, h * d, (h + 1) * d, axis=axis) for h in order]
    return jnp.concatenate(parts, axis=axis)


def _layer_weights(norm1, norm2, w_in, g_qa, w_uq, g_kva, w_ukv, w_out, w_router_grp, w_router_exp):
    wi = w_in.transpose(0, 2, 1)
    z64 = jnp.zeros((DEPTH, 64, D_MODEL), F32)
    cq, ckv, kpe = wi[:, 0:256], wi[:, 256:384], wi[:, 384:416]
    qs = _heads(wi[:, 416:672], HEAD_DIM, (0, 2, 1, 3), 1) * (HEAD_SCALE * LOG2E)
    ks, vs = wi[:, 672:800], wi[:, 800:928]
    qn, kn, vn = wi[:, 928:1184] * (HEAD_SCALE * LOG2E), wi[:, 1184:1440], wi[:, 1440:1696]
    kped = jnp.concatenate([kpe, kpe, z64], axis=1)
    win_rows = [cq, ckv, qs, ks, qn, kn, kped, vs, vn]
    wq = w_uq.reshape(DEPTH, MLA_Q_LORA, MLA_HEADS, MLA_NOPE + MLA_ROPE) * (MLA_SCALE * LOG2E)
    nope, ropew = wq[..., :MLA_NOPE], wq[..., MLA_NOPE:]
    z64q = jnp.zeros((DEPTH, MLA_Q_LORA, 64), F32)
    blocks = []
    for i in range(N_PAIRS):
        blocks += [nope[:, :, 2 * i], nope[:, :, 2 * i + 1], ropew[:, :, 2 * i], ropew[:, :, 2 * i + 1], z64q]
    wkv = w_ukv.reshape(DEPTH, MLA_KV_LORA, MLA_HEADS, MLA_NOPE + MLA_V)
    wuk = wkv[..., :MLA_NOPE].reshape(DEPTH, MLA_KV_LORA, -1)
    wuvt = wkv[..., MLA_NOPE:].reshape(DEPTH, MLA_KV_LORA, -1).transpose(0, 2, 1)
    wout = jnp.concatenate([w_out[:, :512], _heads(w_out[:, 512:768], HEAD_DIM, (0, 2, 1, 3), 1), w_out[:, 768:]],
                           axis=1)
    wr = jnp.concatenate([w_router_grp, w_router_exp,
                          jnp.zeros((DEPTH, D_MODEL, LANES - N_GROUPS - N_EXPERTS), F32)], axis=2)
    wr_hi = wr.astype(BF16)
    wr_lo = (wr - wr_hi.astype(F32)).astype(BF16)
    return {
        "norm1": norm1[:, None], "norm2": norm2[:, None], "g_qa": g_qa[:, None], "g_kva": g_kva[:, None],
        "win": jnp.concatenate(win_rows, axis=1).astype(BF16),
        "wuq": jnp.concatenate(blocks, axis=2).astype(BF16),
        "wuk": wuk.astype(BF16),
        "wuvt": wuvt.astype(BF16),
        "wout": wout.astype(BF16),
        "wr": jnp.concatenate([wr_hi, wr_lo], axis=2),
    }


def _rope_tables():
    n_rows = DEC_SEQ // GRID_W
    lane = np.arange(LANES)

    def tab(d, used_lanes):
        hh = d // 4
        i = lane % d
        freq = ROPE_THETA ** (-jnp.asarray(i % hh, F32) / hh)
        by_row = jnp.asarray((i // (2 * hh)) == 0)[None, None, :]
        valid = jnp.asarray(lane < used_lanes)[None, None, :]
        ang_r = jnp.arange(n_rows, dtype=F32)[:, None] * freq[None, :]
        ang_c = jnp.arange(GRID_W, dtype=F32)[:, None] * freq[None, :]

        def terms(fn):
            return [jnp.where(valid[0] & by_row[0], fn(ang_r), 0.0), jnp.where(valid[0] & ~by_row[0], fn(ang_c), 0.0)]

        return terms(jnp.cos) + terms(jnp.sin)

    return jnp.stack(tab(HEAD_DIM, LANES) + tab(MLA_ROPE, 2 * MLA_ROPE))


def kernel(x_prompt, x_sample, cache_mla_ckv, cache_mla_kpe, cache_swa_k, cache_swa_v, cache_na_k, cache_na_v, c, c_ctx, w_mod, b_mod, norm1, norm2, w_in, g_qa, w_uq, g_kva, w_ukv, swa_sink, na_rpb, w_out, w_router_grp, w_router_exp, w_gate, w_up, w_down, norm_final):
    cpad = jnp.concatenate([c_ctx[None], c, jnp.zeros((8 - 1 - DEC_BATCH, D_MODEL), F32)], axis=0)
    mod = _modulation(cpad, w_mod, b_mod).reshape(DEPTH, 8, N_MOD, D_MODEL)
    tabs = _rope_tables()
    nf = norm_final[None]
    xp = x_prompt.reshape(N_CTX_TOK, D_MODEL)
    xs = x_sample.reshape(N_LAT_TOK, D_MODEL)
    new_caches = ()
    moe_w = (w_gate.astype(BF16), w_up.astype(BF16), w_down.astype(BF16))
    lw = _layer_weights(norm1, norm2, w_in, g_qa, w_uq, g_kva, w_ukv, w_out, w_router_grp, w_router_exp)
    cached_k = lambda a: a.reshape(DEC_BATCH, DEPTH, PAST_LEN, -1)
    cached_vt = lambda a: a.reshape(DEC_BATCH, DEPTH, PAST_LEN, -1).transpose(0, 1, 3, 2)
    for l in range(DEPTH):
        final = l == DEPTH - 1
        outs = _projections(xp, mod[l], lw, None, rope=False, layer=l, new_caches=new_caches)
        om, osw, ona = _context_attention(swa_sink[l], *outs[:9])
        new_caches = outs[9:]
        x1, h2, gates = _tail(xp, om, osw, ona, mod[l], lw, l, latent=False)
        xp = _moe(h2, gates, x1, mod[l], l, moe_w, nf, latent=False, final=final)
        qm, km, vm, qs, ks, vs, qn, kn, vn = _projections(xs, mod[l], lw, tabs, rope=True, layer=l)
        kpe_c = cache_mla_kpe[:, l].reshape(DEC_BATCH * PAST_LEN, MLA_ROPE)
        kpe_dup = jnp.concatenate([kpe_c, kpe_c, jnp.zeros((DEC_BATCH * PAST_LEN, 64), F32)], axis=1)
        kmc, vmc = _expand_cached_mla(cache_mla_ckv[:, l].reshape(DEC_BATCH * PAST_LEN, MLA_KV_LORA), kpe_dup,
                                      lw["wuk"], lw["wuvt"], l)
        om = _latent_mla(qm, km, vm, kmc, vmc)
        osw = _latent_swa(swa_sink[l], qs, ks, vs, cached_k(cache_swa_k), cached_vt(cache_swa_v), l)
        ona = _latent_na(na_rpb[l], qn, kn, vn, cached_k(cache_na_k), cached_vt(cache_na_v), l)
        x1, h2, gates = _tail(xs, om, osw, ona, mod[l], lw, l, latent=True)
        xs = _moe(h2, gates, x1, mod[l], l, moe_w, nf, latent=True, final=final)
    def heads_last(a, n_heads):
        if n_heads is None:
            return a.transpose(0, 1, 3, 2)
        return a.reshape(BATCH, DEPTH, n_heads, -1, SEQ).transpose(0, 1, 4, 2, 3)

    ckv_new, kpe_new, ks_new, vs_new, kn_new, vn_new = new_caches
    return (xp.reshape(BATCH, SEQ, D_MODEL), xs.reshape(DEC_BATCH, DEC_SEQ, D_MODEL),
            ckv_new, heads_last(kpe_new, None),
            heads_last(ks_new, SWA_KV_HEADS), heads_last(vs_new, SWA_KV_HEADS),
            heads_last(kn_new, NA_HEADS), heads_last(vn_new, NA_HEADS))
```
